```python
import jax, jax.numpy as jnp
from jax import lax
import numpy as np

D_MODEL = 1024
BATCH = 2
SEQ = 8192
DEPTH = 1
DEC_BATCH = 128
DEC_SEQ = 4
PAST_LEN = 8192
PAGE_SIZE = 128

NSA_HEADS = 8
NSA_GROUPS = 2
NSA_HPG = NSA_HEADS // NSA_GROUPS
HEAD_DIM = 64
CMP_STRIDE = 16
CMP_LEN = 2 * CMP_STRIDE
CMP_HID = HEAD_DIM
SLC_BLOCK = 64
N_SEL = 16
WINDOW = 512
Q_BLOCK = 128
GLA_HEADS = 4
GLA_DK = 64
GLA_DV = 128
GLA_GATE_RANK = 16
GLA_TAU = 16.0
GLA_CHUNK = 64
D_FF = -(-8 * D_MODEL // (3 * 256)) * 256
EPS = 1e-6

NSA_Q = NSA_HEADS * HEAD_DIM
NSA_KV = 2 * NSA_GROUPS * HEAD_DIM
NSA_GATE = 3 * NSA_HEADS
GLA_QK = GLA_HEADS * GLA_DK
GLA_V = GLA_HEADS * GLA_DV
SPLIT_SIZES = (NSA_Q, NSA_KV, NSA_KV, NSA_KV, NSA_GATE, GLA_QK, GLA_QK, GLA_V, GLA_GATE_RANK, GLA_V, 2 * D_MODEL)
SPLIT_POINTS = tuple(int(s) for s in np.cumsum(SPLIT_SIZES)[:-1])
D_IN = sum(SPLIT_SIZES)

kernel_name = 'hybrid_nsa_gla_adaln_decoder_step'


def rmsnorm(x, g):
    xf = x.astype(jnp.float32)
    y = xf * lax.rsqrt(jnp.mean(xf * xf, axis=-1, keepdims=True) + EPS)
    return (y * g.astype(jnp.float32)).astype(x.dtype)


def masked_softmax(s, valid):
    s = jnp.where(valid, s.astype(jnp.float32), -jnp.inf)
    m = jnp.max(s, axis=-1, keepdims=True)
    m = jnp.where(jnp.isfinite(m), m, 0.0)
    e = jnp.exp(s - m)
    d = jnp.sum(e, axis=-1, keepdims=True)
    return e / jnp.where(d > 0, d, 1.0)


def compress(kv, w1, b1, w2, pe):
    B, L = kv.shape[:2]
    n_chunk = L // CMP_STRIDE
    ch = kv[:, :n_chunk * CMP_STRIDE].reshape(B, n_chunk, CMP_STRIDE, NSA_GROUPS, HEAD_DIM)
    w1r = w1.reshape(CMP_LEN, HEAD_DIM, CMP_HID)
    first = jnp.einsum('bnpgd,pdh->bngh', ch + pe[:CMP_STRIDE, None, :], w1r[:CMP_STRIDE])
    second = jnp.einsum('bnpgd,pdh->bngh', ch + pe[CMP_STRIDE:, None, :], w1r[CMP_STRIDE:])
    hid = jax.nn.gelu(first[:, :-1] + second[:, 1:] + b1)
    return jnp.einsum('bngh,hd->bngd', hid, w2)


def overlap_matrix(n_cmp, n_slc):
    n = jnp.arange(n_cmp)[:, None] * CMP_STRIDE
    j = jnp.arange(n_slc)[None, :] * SLC_BLOCK
    return ((n < j + SLC_BLOCK) & (n + CMP_LEN > j)).astype(jnp.float32)


def nsa_core(q, qpos, ck, cv, gather_slc, n_slc, wk, wv, wpos):
    B, T = q.shape[:2]
    n_cmp = ck.shape[1]
    s = jnp.einsum('btghd,bngd->btghn', q, ck)
    cvalid = (jnp.arange(n_cmp) * CMP_STRIDE + CMP_LEN)[None, :] <= qpos[:, None] + 1
    p = masked_softmax(s, cvalid[None, :, None, None, :])
    o_cmp = jnp.einsum('btghn,bngd->btghd', p.astype(cv.dtype), cv)
    imp = jnp.einsum('btgn,nj->btgj', p.sum(axis=3), overlap_matrix(n_cmp, n_slc))
    j = jnp.arange(n_slc)[None, :]
    cur = (qpos // SLC_BLOCK)[:, None]
    forced = ((j == 0) | (j == cur) | (j == cur - 1))[None, :, None, :]
    causal = (j <= cur)[None, :, None, :]
    score = jnp.where(forced, jnp.inf, jnp.where(causal, imp, -jnp.inf))
    _, idx = lax.top_k(score, min(N_SEL, n_slc))
    sk, sv = gather_slc(idx)
    kpos = idx[..., None] * SLC_BLOCK + jnp.arange(SLC_BLOCK)
    svalid = (kpos <= qpos[None, :, None, None, None]).reshape(B, T, NSA_GROUPS, 1, -1)
    s = jnp.einsum('btghd,btgkrd->btghkr', q, sk).reshape(B, T, NSA_GROUPS, NSA_HPG, -1)
    p = masked_softmax(s, svalid)
    o_slc = jnp.einsum('btghm,btgmd->btghd', p.astype(sv.dtype), sv.reshape(B, T, NSA_GROUPS, -1, HEAD_DIM))
    s = jnp.einsum('btghd,bsgd->btghs', q, wk)
    dist = qpos[:, None] - wpos[None, :]
    wvalid = (dist >= 0) & (dist < WINDOW) & (wpos >= 0)[None, :]
    p = masked_softmax(s, wvalid[None, :, None, None, :])
    o_win = jnp.einsum('btghs,bsgd->btghd', p.astype(wv.dtype), wv)
    return o_cmp, o_slc, o_win


def nsa_prompt(q, kv_c, kv_s, kv_w, cmpk, cmpv):
    B, L = q.shape[:2]
    ck = compress(kv_c[:, :, 0], *cmpk)
    cv = compress(kv_c[:, :, 1], *cmpv)
    n_slc = L // SLC_BLOCK
    blocks = kv_s.reshape(B, n_slc, SLC_BLOCK, 2, NSA_GROUPS, HEAD_DIM)
    bi = jnp.arange(B)[:, None, None, None]
    gi = jnp.arange(NSA_GROUPS)[None, None, :, None]

    def gather(idx):
        blk = blocks[bi, idx, :, :, gi, :]
        return blk[..., 0, :], blk[..., 1, :]

    win = jnp.pad(kv_w, ((0, 0), (WINDOW, 0), (0, 0), (0, 0), (0, 0)))

    def one_block(i):
        t0 = i * Q_BLOCK
        qb = lax.dynamic_slice_in_dim(q, t0, Q_BLOCK, axis=1)
        wb = lax.dynamic_slice_in_dim(win, t0, WINDOW + Q_BLOCK, axis=1)
        qpos = t0 + jnp.arange(Q_BLOCK)
        wpos = t0 - WINDOW + jnp.arange(WINDOW + Q_BLOCK)
        return nsa_core(qb, qpos, ck, cv, gather, n_slc, wb[:, :, 0], wb[:, :, 1], wpos)

    outs = lax.map(one_block, jnp.arange(L // Q_BLOCK))
    return tuple(o.swapaxes(0, 1).reshape(q.shape) for o in outs)


def nsa_sample(q, kv_c, kv_s, kv_w, cache_c, cache_s, win_buf, page_table, cmpk, cmpv):
    Bd, S = q.shape[:2]
    P = page_table.shape[1] * PAGE_SIZE
    past_c = cache_c[page_table].reshape(Bd, P, 2, NSA_GROUPS, HEAD_DIM)
    full_c = jnp.concatenate([past_c, kv_c], axis=1)
    ck = compress(full_c[:, :, 0], *cmpk)
    cv = compress(full_c[:, :, 1], *cmpv)
    n_slc = -(-(P + S) // SLC_BLOCK)
    nb_past = P // SLC_BLOCK
    nb_new = n_slc - nb_past
    new_blocks = jnp.pad(kv_s, ((0, 0), (0, nb_new * SLC_BLOCK - S), (0, 0), (0, 0), (0, 0)))
    new_blocks = new_blocks.reshape(Bd, nb_new, SLC_BLOCK, 2, NSA_GROUPS, HEAD_DIM)
    bi = jnp.arange(Bd)[:, None, None, None]
    gi = jnp.arange(NSA_GROUPS)[None, None, :, None]
    r = jnp.arange(SLC_BLOCK)

    def gather(idx):
        start = jnp.minimum(idx, nb_past - 1) * SLC_BLOCK
        phys = page_table[bi, start // PAGE_SIZE]
        past = cache_s[phys[..., None], (start % PAGE_SIZE)[..., None] + r, :, gi[..., None], :]
        new = new_blocks[bi, jnp.clip(idx - nb_past, 0, nb_new - 1), :, :, gi, :]
        blk = jnp.where((idx < nb_past)[..., None, None, None], past, new)
        return blk[..., 0, :], blk[..., 1, :]

    w_buf = win_buf.shape[1]
    wkv = jnp.concatenate([win_buf, kv_w], axis=1)
    wpos = P - w_buf + jnp.arange(w_buf + S)
    qpos = P + jnp.arange(S)
    outs = nsa_core(q, qpos, ck, cv, gather, n_slc, wkv[:, :, 0], wkv[:, :, 1], wpos)
    return outs, wkv[:, S:]


def gla_chunk(s0, q, k, v, lg):
    C = q.shape[1]
    b = jnp.cumsum(lg, axis=1)
    o_inter = jnp.einsum('bthk,bhkv->bthv', q * jnp.exp(b), s0)
    tri = (jnp.arange(C)[:, None] >= jnp.arange(C)[None, :])[None, :, :, None, None]
    dec = jnp.exp(jnp.where(tri, b[:, :, None] - b[:, None, :], -jnp.inf))
    a = jnp.einsum('bthk,btshk,bshk->btsh', q, dec, k)
    o_intra = jnp.einsum('btsh,bshv->bthv', a, v)
    b_last = b[:, -1]
    s_new = jnp.exp(b_last)[..., None] * s0 + jnp.einsum('bshk,bshv->bhkv', k * jnp.exp(b_last[:, None] - b), v)
    return s_new, o_inter + o_intra


def gla_scan(s0, q, k, v, lg):
    B, L = q.shape[:2]
    C = GLA_CHUNK if L % GLA_CHUNK == 0 else L
    n = L // C

    def chunks(a):
        return jnp.moveaxis(a.astype(jnp.float32).reshape((B, n, C) + a.shape[2:]), 1, 0)

    def step(s, inp):
        return gla_chunk(s, *inp)

    s_fin, o = lax.scan(step, s0, (chunks(q), chunks(k), chunks(v), chunks(lg)))
    return s_fin, jnp.moveaxis(o, 0, 1).reshape(B, L, GLA_HEADS, GLA_DV)


def block_forward(x, c, core, w_ada, b_ada, g_pre_mix, g_post_mix, w_in, w_gla_a2, b_gla_a, gla_norm_g,
                  w_nsa_o, w_gla_o, w_out, g_pre_ffn, g_post_ffn, w_ff_gate, w_ff_up, w_ff_down):
    B, L, _ = x.shape
    mod = (c @ w_ada + b_ada)[:, None, :]
    sh1, sc1, ga1, sh2, sc2, ga2 = jnp.split(mod, 6, axis=-1)
    h = rmsnorm(x, g_pre_mix) * (1 + sc1) + sh1
    z = h @ w_in
    q_n, kv_c, kv_s, kv_w, g_n, q_g, k_g, v_g, a_g, r_g, m_g = jnp.split(z, SPLIT_POINTS, axis=-1)
    q_n = (q_n * HEAD_DIM ** -0.5).reshape(B, L, NSA_GROUPS, NSA_HPG, HEAD_DIM)
    kv_c = kv_c.reshape(B, L, 2, NSA_GROUPS, HEAD_DIM)
    kv_s = kv_s.reshape(B, L, 2, NSA_GROUPS, HEAD_DIM)
    kv_w = kv_w.reshape(B, L, 2, NSA_GROUPS, HEAD_DIM)
    q_g = (q_g * GLA_DK ** -0.5).reshape(B, L, GLA_HEADS, GLA_DK)
    k_g = k_g.reshape(B, L, GLA_HEADS, GLA_DK)
    v_g = v_g.reshape(B, L, GLA_HEADS, GLA_DV)
    lg = (jax.nn.log_sigmoid((a_g @ w_gla_a2 + b_gla_a).astype(jnp.float32)) / GLA_TAU).reshape(B, L, GLA_HEADS, GLA_DK)
    o_cmp, o_slc, o_win, o_gla, states = core(q_n, kv_c, kv_s, kv_w, q_g, k_g, v_g, lg)
    gates = jax.nn.sigmoid(g_n).reshape(B, L, 3, NSA_GROUPS, NSA_HPG, 1)
    o_nsa = gates[:, :, 0] * o_cmp + gates[:, :, 1] * o_slc + gates[:, :, 2] * o_win
    y_a = o_nsa.reshape(B, L, NSA_Q) @ w_nsa_o
    o_gla = rmsnorm(o_gla.astype(x.dtype), gla_norm_g) * jax.nn.silu(r_g.reshape(B, L, GLA_HEADS, GLA_DV))
    y_b = o_gla.reshape(B, L, GLA_V) @ w_gla_o
    m_a, m_b = jnp.split(jax.nn.sigmoid(m_g), 2, axis=-1)
    mix = (m_a * y_a + m_b * y_b) @ w_out
    x = x + ga1 * rmsnorm(mix, g_post_mix)
    h = rmsnorm(x, g_pre_ffn) * (1 + sc2) + sh2
    f = (jax.nn.silu(h @ w_ff_gate) * (h @ w_ff_up)) @ w_ff_down
    x = x + ga2 * rmsnorm(f, g_post_ffn)
    return x, states


def setup_inputs(seed: int = 0) -> dict:
    key = jax.random.key(seed)
    keys = list(jax.random.split(key, 48))

    def nrm(shape, scale):
        return jax.random.normal(keys.pop(), shape, jnp.float32) * scale

    def gain(shape):
        return 1.0 + nrm(shape, 0.02)

    D = D_MODEL
    n_pages = PAST_LEN // PAGE_SIZE
    n_used = DEC_BATCH * n_pages
    n_pool = n_used + max(1, n_used // 4)
    page_table = jax.random.permutation(keys.pop(), n_pool)[:n_used].reshape(DEC_BATCH, n_pages).astype(jnp.int32)
    w_buf = min(WINDOW, PAST_LEN)
    cmp_in = CMP_LEN * HEAD_DIM
    return {
        'x_prompt': nrm((BATCH, SEQ, D), 1.0),
        'x_sample': nrm((DEC_BATCH, DEC_SEQ, D), 1.0),
        'cache_cmp_kv': nrm((DEPTH, n_pool, PAGE_SIZE, 2, NSA_GROUPS, HEAD_DIM), 1.0),
        'cache_slc_kv': nrm((DEPTH, n_pool, PAGE_SIZE, 2, NSA_GROUPS, HEAD_DIM), 1.0),
        'state_win_kv': nrm((DEPTH, DEC_BATCH, w_buf, 2, NSA_GROUPS, HEAD_DIM), 1.0),
        'state_gla': nrm((DEPTH, DEC_BATCH, GLA_HEADS, GLA_DK, GLA_DV), 0.5),
        'page_table': page_table,
        'c_prompt': nrm((BATCH, D), 1.0),
        'c_sample': nrm((DEC_BATCH, D), 1.0),
        'w_ada': nrm((DEPTH, D, 6 * D), 0.3 * D ** -0.5),
        'b_ada': nrm((DEPTH, 6 * D), 0.02),
        'g_pre_mix': gain((DEPTH, D)),
        'g_post_mix': gain((DEPTH, D)),
        'w_in': nrm((DEPTH, D, D_IN), D ** -0.5),
        'cmp_w1_k': nrm((DEPTH, cmp_in, CMP_HID), cmp_in ** -0.5),
        'cmp_b1_k': nrm((DEPTH, CMP_HID), 0.02),
        'cmp_w2_k': nrm((DEPTH, CMP_HID, HEAD_DIM), CMP_HID ** -0.5),
        'cmp_pe_k': nrm((DEPTH, CMP_LEN, HEAD_DIM), 0.1),
        'cmp_w1_v': nrm((DEPTH, cmp_in, CMP_HID), cmp_in ** -0.5),
        'cmp_b1_v': nrm((DEPTH, CMP_HID), 0.02),
        'cmp_w2_v': nrm((DEPTH, CMP_HID, HEAD_DIM), CMP_HID ** -0.5),
        'cmp_pe_v': nrm((DEPTH, CMP_LEN, HEAD_DIM), 0.1),
        'w_gla_a2': nrm((DEPTH, GLA_GATE_RANK, GLA_QK), GLA_GATE_RANK ** -0.5),
        'b_gla_a': nrm((DEPTH, GLA_QK), 0.02),
        'gla_norm_g': gain((DEPTH, GLA_DV)),
        'w_nsa_o': nrm((DEPTH, NSA_Q, D), NSA_Q ** -0.5),
        'w_gla_o': nrm((DEPTH, GLA_V, D), GLA_V ** -0.5),
        'w_out': nrm((DEPTH, D, D), D ** -0.5),
        'g_pre_ffn': gain((DEPTH, D)),
        'g_post_ffn': gain((DEPTH, D)),
        'w_ff_gate': nrm((DEPTH, D, D_FF), D ** -0.5),
        'w_ff_up': nrm((DEPTH, D, D_FF), D ** -0.5),
        'w_ff_down': nrm((DEPTH, D_FF, D), D_FF ** -0.5),
    }


def reference(x_prompt, x_sample, cache_cmp_kv, cache_slc_kv, state_win_kv, state_gla, page_table,
              c_prompt, c_sample, w_ada, b_ada, g_pre_mix, g_post_mix, w_in,
              cmp_w1_k, cmp_b1_k, cmp_w2_k, cmp_pe_k, cmp_w1_v, cmp_b1_v, cmp_w2_v, cmp_pe_v,
              w_gla_a2, b_gla_a, gla_norm_g, w_nsa_o, w_gla_o, w_out, g_pre_ffn, g_post_ffn,
              w_ff_gate, w_ff_up, w_ff_down):
    collected = [[] for _ in range(8)]
    yp, ys = x_prompt, x_sample
    for l in range(DEPTH):
        cmpk = (cmp_w1_k[l], cmp_b1_k[l], cmp_w2_k[l], cmp_pe_k[l])
        cmpv = (cmp_w1_v[l], cmp_b1_v[l], cmp_w2_v[l], cmp_pe_v[l])

        def prompt_core(q_n, kv_c, kv_s, kv_w, q_g, k_g, v_g, lg, cmpk=cmpk, cmpv=cmpv):
            o_cmp, o_slc, o_win = nsa_prompt(q_n, kv_c, kv_s, kv_w, cmpk, cmpv)
            s0 = jnp.zeros((q_g.shape[0], GLA_HEADS, GLA_DK, GLA_DV), jnp.float32)
            s_fin, o_gla = gla_scan(s0, q_g, k_g, v_g, lg)
            L = kv_w.shape[1]
            keep = min(WINDOW, L)
            return o_cmp, o_slc, o_win, o_gla, (kv_c, kv_s, kv_w[:, L - keep:], s_fin.astype(state_gla.dtype))

        def sample_core(q_n, kv_c, kv_s, kv_w, q_g, k_g, v_g, lg, l=l, cmpk=cmpk, cmpv=cmpv):
            (o_cmp, o_slc, o_win), win_new = nsa_sample(q_n, kv_c, kv_s, kv_w, cache_cmp_kv[l], cache_slc_kv[l],
                                                        state_win_kv[l], page_table, cmpk, cmpv)
            s_fin, o_gla = gla_scan(state_gla[l].astype(jnp.float32), q_g, k_g, v_g, lg)
            return o_cmp, o_slc, o_win, o_gla, (kv_c, kv_s, win_new, s_fin.astype(state_gla.dtype))

        lw = (w_ada[l], b_ada[l], g_pre_mix[l], g_post_mix[l], w_in[l], w_gla_a2[l], b_gla_a[l], gla_norm_g[l],
              w_nsa_o[l], w_gla_o[l], w_out[l], g_pre_ffn[l], g_post_ffn[l], w_ff_gate[l], w_ff_up[l], w_ff_down[l])
        yp, st_p = block_forward(yp, c_prompt, prompt_core, *lw)
        ys, st_s = block_forward(ys, c_sample, sample_core, *lw)
        for i in range(4):
            collected[2 * i].append(st_p[i])
            collected[2 * i + 1].append(st_s[i])
    cmp_p, cmp_s, slc_p, slc_s, win_p, win_s, gla_p, gla_s = [jnp.stack(c) for c in collected]
    return (yp, ys, cmp_p, cmp_s, slc_p, slc_s, win_p, win_s, gla_p, gla_s)
```

```python
import functools

import jax
import jax.numpy as jnp
from jax import lax
from jax.experimental import pallas as pl
from jax.experimental.pallas import tpu as pltpu

F32 = jnp.float32
BF16 = jnp.bfloat16

NSA_HEADS = 8
NSA_GROUPS = 2
NSA_HPG = NSA_HEADS // NSA_GROUPS
HEAD_DIM = 64
CMP_STRIDE = 16
CMP_LEN = 32
SLC_BLOCK = 64
N_SEL = 16
WINDOW = 512
GLA_HEADS = 4
GLA_DK = 64
GLA_DV = 128
GLA_TAU = 16.0
GLA_CHUNK = 64
GLA_SUB = 16
EPS = 1e-6
PAGE_SIZE = 128
NEG = -1e30
LANES = 128
VMEM_LIMIT = 56 * 1024 * 1024

KV_W = 2 * NSA_GROUPS * HEAD_DIM
NQ = NSA_HEADS * HEAD_DIM
GQK = GLA_HEADS * GLA_DK
GV = GLA_HEADS * GLA_DV


def _dot(a, b):
    return jnp.dot(a, b, preferred_element_type=F32)


def _dot_nt(a, b):
    return lax.dot_general(a, b, (((1,), (1,)), ((), ())), preferred_element_type=F32)


def _dot_f32(a, b):
    return jnp.dot(a, b, preferred_element_type=F32, precision=lax.Precision.HIGHEST)


def _dot01(x, m01):
    hi = x.astype(BF16)
    r1 = x - hi.astype(F32)
    mid = r1.astype(BF16)
    lo = (r1 - mid.astype(F32)).astype(BF16)
    return _dot(hi, m01) + _dot(mid, m01) + _dot(lo, m01)


def _dot01_l(m01, x):
    hi = x.astype(BF16)
    r1 = x - hi.astype(F32)
    mid = r1.astype(BF16)
    lo = (r1 - mid.astype(F32)).astype(BF16)
    return _dot(m01, hi) + _dot(m01, mid) + _dot(m01, lo)


def _sigmoid(x):
    return 1.0 / (1.0 + jnp.exp(-x))


def _params(*sem):
    return pltpu.CompilerParams(dimension_semantics=sem, vmem_limit_bytes=VMEM_LIMIT)


def _iota(shape, dim):
    return lax.broadcasted_iota(jnp.int32, shape, dim)


def _div_pow2(x, n):
    assert n & (n - 1) == 0, n
    return x >> (n.bit_length() - 1)


def _ada_body(c_ref, w_ref, b_ref, o_ref):
    o_ref[...] = _dot_f32(c_ref[...], w_ref[...]) + b_ref[...]


def _ada(c_all, w_ada, b_ada):
    r, d = c_all.shape
    n = w_ada.shape[1]
    tn = 1536 if n % 1536 == 0 else n
    return pl.pallas_call(
        _ada_body,
        grid=(n // tn,),
        in_specs=[pl.BlockSpec((r, d), lambda j: (0, 0)),
                  pl.BlockSpec((d, tn), lambda j: (0, j)),
                  pl.BlockSpec((1, tn), lambda j: (0, j))],
        out_specs=pl.BlockSpec((r, tn), lambda j: (0, j)),
        out_shape=jax.ShapeDtypeStruct((r, n), F32),
        compiler_params=_params("arbitrary"),
        name="ada",
    )(c_all, w_ada, b_ada.reshape(1, n))


def _norm_mod(x, g, sc, sh):
    ms = jnp.mean(x * x, axis=-1, keepdims=True)
    y = x * lax.rsqrt(ms + EPS) * g
    return y * (1.0 + sc) + sh


def _rms(x, g):
    ms = jnp.mean(x * x, axis=-1, keepdims=True)
    return x * lax.rsqrt(ms + EPS) * g


def _pre_nsa_body(x_ref, sc_ref, sh_ref, g_ref, wq_ref, wkv_ref, wg_ref,
                  q_ref, kvc_ref, kvs_ref, kvw_ref, kvsb_ref, kvwb_ref, gate_ref):
    h = _norm_mod(x_ref[...], g_ref[...], sc_ref[...], sh_ref[...]).astype(BF16)
    q_ref[...] = (_dot(h, wq_ref[...]) * HEAD_DIM ** -0.5).astype(BF16)
    kv = _dot(h, wkv_ref[...])
    kvc = kv[:, 0:KV_W]
    kvs = kv[:, KV_W:2 * KV_W]
    kvw = kv[:, 2 * KV_W:3 * KV_W]
    kvc_ref[...] = kvc
    kvs_ref[...] = kvs
    kvw_ref[...] = kvw
    kvsb_ref[...] = kvs.astype(BF16)
    kvwb_ref[...] = kvw.astype(BF16)
    gate_ref[...] = _sigmoid(_dot(h, wg_ref[...]))


def _mod_spec(mod3, tm, rows_per_mod):
    r = mod3.shape[1]
    return pl.BlockSpec((None, r, mod3.shape[2]), lambda i: ((i * tm) // rows_per_mod, 0, 0))


def _pre_nsa(x, sc, sh, g, wq, wkv, wg, tm, rows_per_mod):
    n, d = x.shape
    row = lambda w: pl.BlockSpec((tm, w), lambda i: (i, 0))
    full = lambda a: pl.BlockSpec(a.shape, lambda i: (0, 0))
    outs = [(NQ, BF16), (KV_W, F32), (KV_W, F32), (KV_W, F32), (KV_W, BF16), (KV_W, BF16), (LANES, F32)]
    return pl.pallas_call(
        _pre_nsa_body,
        grid=(n // tm,),
        in_specs=[row(d), _mod_spec(sc, tm, rows_per_mod), _mod_spec(sh, tm, rows_per_mod), full(g),
                  full(wq), full(wkv), full(wg)],
        out_specs=[row(w) for w, _ in outs],
        out_shape=[jax.ShapeDtypeStruct((n, w), t) for w, t in outs],
        compiler_params=_params("parallel"),
        name="pre_nsa",
    )(x, sc, sh, g, wq, wkv, wg)


def _log_sigmoid(x):
    return jnp.minimum(x, 0.0) - jnp.log(1.0 + jnp.exp(-jnp.abs(x)))


def _pre_gla_body(x_ref, sc_ref, sh_ref, g_ref, wqkv_ref, wkt_ref, wa_ref, wat_ref, wr_ref, a2_ref, a2t_ref,
                  ba_ref, bat_ref, q_ref, k_ref, v_ref, kt_ref, lg_ref, lgt_ref, r_ref):
    h = _norm_mod(x_ref[...], g_ref[...], sc_ref[...], sh_ref[...]).astype(BF16)
    qkv = _dot(h, wqkv_ref[...])
    q_ref[...] = qkv[:, 0:GQK] * GLA_DK ** -0.5
    k_ref[...] = qkv[:, GQK:2 * GQK]
    v_ref[...] = qkv[:, 2 * GQK:2 * GQK + GV]
    kt_ref[...] = _dot_nt(wkt_ref[...], h)
    a = _dot(h, wa_ref[...])
    at = _dot_nt(wat_ref[...], h)
    lg_ref[...] = _log_sigmoid(_dot_f32(a, a2_ref[...]) + ba_ref[...]) * (1.0 / GLA_TAU)
    lgt_ref[...] = _log_sigmoid(_dot_f32(a2t_ref[...], at) + bat_ref[...]) * (1.0 / GLA_TAU)
    r = _dot(h, wr_ref[...])
    r_ref[...] = (r * _sigmoid(r)).astype(BF16)


def _pre_gla(x, sc, sh, g, wqkv, wkt, wa, wat, wr, a2, a2t, ba, bat, tm, rows_per_mod):
    n, d = x.shape
    row = lambda w: pl.BlockSpec((tm, w), lambda i: (i, 0))
    col = lambda w: pl.BlockSpec((w, tm), lambda i: (0, i))
    full = lambda a: pl.BlockSpec(a.shape, lambda i: (0, 0))
    return pl.pallas_call(
        _pre_gla_body,
        grid=(n // tm,),
        in_specs=[row(d), _mod_spec(sc, tm, rows_per_mod), _mod_spec(sh, tm, rows_per_mod), full(g),
                  full(wqkv), full(wkt), full(wa), full(wat), full(wr), full(a2), full(a2t), full(ba), full(bat)],
        out_specs=[row(GQK), row(GQK), row(GV), col(GQK), row(GQK), col(GQK), row(GV)],
        out_shape=[jax.ShapeDtypeStruct((n, GQK), F32), jax.ShapeDtypeStruct((n, GQK), F32),
                   jax.ShapeDtypeStruct((n, GV), F32), jax.ShapeDtypeStruct((GQK, n), F32),
                   jax.ShapeDtypeStruct((n, GQK), F32), jax.ShapeDtypeStruct((GQK, n), F32),
                   jax.ShapeDtypeStruct((n, GV), BF16)],
        compiler_params=_params("parallel"),
        name="pre_gla",
    )(x, sc, sh, g, wqkv, wkt, wa, wat, wr, a2, a2t, ba, bat)


def _gelu_tanh(x):
    return 0.5 * x * (1.0 + jnp.tanh(0.7978845608028654 * (x + 0.044715 * x * x * x)))


def _compress_rows(x, pea, peb, w1a, w1b, b1, w2, n_cmp):
    nch = x.shape[0]
    first = _dot((x + pea).astype(BF16), w1a)
    second = _dot((x + peb).astype(BF16), w1b)
    if nch % 8 == 0:
        nxt = pltpu.roll(second, nch - 1, 0)
    else:
        nxt = jnp.concatenate([second[1:], second[:1]], axis=0)
    hid = _gelu_tanh(first + nxt + b1)
    out = _dot(hid.astype(BF16), w2)
    rows = _iota(out.shape, 0)
    return jnp.where(rows < n_cmp, out, 0.0)


def _cmp_body(*refs, n_in, n_cmp, n_prefetch=0):
    refs = refs[n_prefetch:]
    x_refs = refs[:n_in]
    pea_ref, peb_ref, w1a_ref, w1b_ref, b1_ref, w2_ref, o_ref = refs[n_in:]
    if n_in == 1:
        x = x_refs[0][...]
    else:
        x = jnp.concatenate([r[...] for r in x_refs], axis=0)
    o_ref[...] = _compress_rows(x, pea_ref[...], peb_ref[...], w1a_ref[...], w1b_ref[...], b1_ref[...],
                                w2_ref[...], n_cmp).astype(o_ref.dtype)


def _cmp_prompt(kvc, cw):
    b, l, _ = kvc.shape
    nch = l // CMP_STRIDE
    x = kvc[:, :nch * CMP_STRIDE].reshape(b, nch, CMP_STRIDE * KV_W)
    full = lambda a: pl.BlockSpec(a.shape, lambda i: (0, 0))
    return pl.pallas_call(
        functools.partial(_cmp_body, n_in=1, n_cmp=nch - 1),
        grid=(b,),
        in_specs=[pl.BlockSpec((None, nch, CMP_STRIDE * KV_W), lambda i: (i, 0, 0))] + [full(a) for a in cw],
        out_specs=pl.BlockSpec((None, nch, KV_W), lambda i: (i, 0, 0)),
        out_shape=jax.ShapeDtypeStruct((b, nch, KV_W), BF16),
        compiler_params=_params("parallel"),
        name="cmp_prompt",
    )(x, *cw)


def _cmp_sample(cache_c, page_table, cw):
    n_pool = cache_c.shape[0]
    bd, n_pages = page_table.shape
    cpp = PAGE_SIZE // CMP_STRIDE
    nch = n_pages * cpp
    x = cache_c.reshape(n_pool, cpp, CMP_STRIDE * KV_W)
    full = lambda a: pl.BlockSpec(a.shape, lambda i, pt: (0, 0))
    page = lambda k: pl.BlockSpec((None, cpp, CMP_STRIDE * KV_W), lambda i, pt, k=k: (pt[i, k], 0, 0))
    return pl.pallas_call(
        functools.partial(_cmp_body, n_in=n_pages, n_cmp=nch - 1, n_prefetch=1),
        grid_spec=pltpu.PrefetchScalarGridSpec(
            num_scalar_prefetch=1,
            grid=(bd,),
            in_specs=[page(k) for k in range(n_pages)] + [full(a) for a in cw],
            out_specs=pl.BlockSpec((None, nch, KV_W), lambda i, pt: (i, 0, 0)),
        ),
        out_shape=jax.ShapeDtypeStruct((bd, nch, KV_W), BF16),
        compiler_params=_params("parallel"),
        name="cmp_sample",
    )(page_table, *([x] * n_pages), *cw)


def _head_queries(q, rows):
    half = _iota((rows, LANES), 1) >> 6
    out = []
    for hh in range(NSA_HEADS):
        g = hh // NSA_HPG
        blk = q[:, (hh // 2) * LANES:(hh // 2 + 1) * LANES].astype(F32)
        if hh % 2 != g:
            blk = pltpu.roll(blk, HEAD_DIM, 1) if rows % 8 == 0 else jnp.roll(blk, HEAD_DIM, 1)
        out.append(jnp.where(half == g, blk, 0.0).astype(BF16))
    return out


def _topk_select(imp, cur, nb):
    shape = imp.shape
    jl = _iota(shape, 1)
    jf = jl.astype(F32)
    forced = (jl == 0) | (jl == cur) | (jl == cur - 1)
    elig = (jl >= 1) & (jl <= cur - 2)
    sel0 = jnp.where(forced, 1.0, 0.0)
    alive0 = jnp.where(elig, 1.0, 0.0)

    def step(_, carry):
        sel, alive = carry
        live = alive > 0.0
        vals = jnp.where(live, imp, -1.0)
        m = jnp.max(vals, axis=-1, keepdims=True)
        cand = live & (vals == m)
        idx = jnp.min(jnp.where(cand, jf, float(nb)), axis=-1, keepdims=True)
        one = jf == idx
        return jnp.where(one, 1.0, sel), jnp.where(one, 0.0, alive)

    sel, _ = lax.fori_loop(0, N_SEL - 3, step, (sel0, alive0))
    return sel


def _cmp_attend(qh, ck, cv, cvalid, ncols):
    s = _dot_nt(qh, ck)
    s = jnp.where(cvalid, s, NEG)
    m = jnp.max(s, axis=-1, keepdims=True)
    e = jnp.where(cvalid, jnp.exp(s - m), 0.0)
    d = jnp.sum(e, axis=-1, keepdims=True)
    p = e / jnp.where(d > 0.0, d, 1.0)
    return p, _dot(p.astype(BF16), cv)


def _nsa_prompt_body(q_ref, gate_ref, ckv_ref, kvs_ref, kvw_ref, ov_ref, e_ref, o_ref, m_sc, acc_sc, out_sc,
                     *, tq, n_cmp):
    i = pl.program_id(1)
    t0 = i * tq
    nc = ckv_ref.shape[0]
    nb = ov_ref.shape[1]
    qh = _head_queries(q_ref[...], tq)
    half = _iota((tq, LANES), 1) >> 6
    trow = t0 + _iota((tq, 1), 0)
    gates = gate_ref[...]

    def gate(branch, hh):
        c = branch * NSA_HEADS + hh
        return gates[:, c:c + 1]

    ck = ckv_ref[:, 0:LANES]
    cv = ckv_ref[:, LANES:2 * LANES]
    ncol = _iota((tq, nc), 1)
    cvalid = (ncol * CMP_STRIDE + CMP_LEN <= trow + 1) & (ncol < n_cmp)
    psum = [jnp.zeros((tq, nc), F32) for _ in range(NSA_GROUPS)]
    for hh in range(NSA_HEADS):
        g = hh // NSA_HPG
        p, oc = _cmp_attend(qh[hh], ck, cv, cvalid, nc)
        psum[g] = psum[g] + p
        out_sc[hh] = gate(0, hh) * oc
    ov = ov_ref[...]
    imp = jnp.concatenate([_dot01(psum[g], ov) for g in range(NSA_GROUPS)], axis=0)
    cur = jnp.concatenate([trow, trow], axis=0) >> 6
    sel = _topk_select(imp, cur, nb).astype(BF16)

    def tile(hh, k128, vaug, bias):
        s = _dot_nt(qh[hh], k128) + bias
        m_old = m_sc[hh]
        m_new = jnp.maximum(m_old, jnp.max(s, axis=-1, keepdims=True))
        p = jnp.exp(s - m_new[:, 0:1])
        acc_sc[hh] = jnp.exp(m_old - m_new) * acc_sc[hh] + _dot(p.astype(BF16), vaug)
        m_sc[hh] = m_new

    def load_kv(ref, start):
        k128 = ref[pl.ds(start, tq), 0:LANES]
        v128 = ref[pl.ds(start, tq), LANES:2 * LANES]
        lane_half = _iota(v128.shape, 1) >> 6
        vaug = [jnp.where(lane_half == g, v128, jnp.ones_like(v128)) for g in range(NSA_GROUPS)]
        return k128, vaug

    def reset():
        m_sc[...] = jnp.full(m_sc.shape, NEG, F32)
        acc_sc[...] = jnp.zeros(acc_sc.shape, F32)

    def finish(branch):
        for hh in range(NSA_HEADS):
            acc = acc_sc[hh]
            den = pltpu.roll(acc, HEAD_DIM, 1)
            out_sc[hh] = out_sc[hh] + gate(branch, hh) * (acc / den)

    kcol = _iota((tq, tq), 1)
    tloc = _iota((tq, tq), 0)

    def sel_bias(start, causal):
        e = e_ref[:, pl.ds(start, tq)]
        out = []
        for g in range(NSA_GROUPS):
            x = _dot(sel[g * tq:(g + 1) * tq], e)
            b = (x - 1.0) * (-NEG)
            if causal:
                b = jnp.where(kcol <= tloc, b, NEG)
            out.append(b)
        return out

    reset()
    k128, vaug = load_kv(kvs_ref, pl.multiple_of(t0, tq))
    bias = sel_bias(pl.multiple_of(t0, tq), True)
    for hh in range(NSA_HEADS):
        tile(hh, k128, vaug[hh // NSA_HPG], bias[hh // NSA_HPG])

    def slc_step(j, carry):
        start = pl.multiple_of(j * tq, tq)
        k128, vaug = load_kv(kvs_ref, start)
        bias = sel_bias(start, False)
        for hh in range(NSA_HEADS):
            tile(hh, k128, vaug[hh // NSA_HPG], bias[hh // NSA_HPG])
        return carry

    lax.fori_loop(0, i, slc_step, 0)
    finish(1)

    reset()

    def win_tile(jt):
        start = pl.multiple_of(jt * tq, tq)
        k128, vaug = load_kv(kvw_ref, start)
        dist = (t0 - jt * tq) + tloc - kcol
        bias = jnp.where((dist >= 0) & (dist < WINDOW), 0.0, NEG)
        for hh in range(NSA_HEADS):
            tile(hh, k128, vaug[hh // NSA_HPG], bias)

    win_tile(i)
    for back in range(1, -(-WINDOW // tq) + 1):
        @pl.when(i >= back)
        def _():
            win_tile(i - back)
    finish(2)

    for pair in range(NSA_HEADS // 2):
        parts = []
        for hh in (2 * pair, 2 * pair + 1):
            x = out_sc[hh]
            if hh % 2 != hh // NSA_HPG:
                x = pltpu.roll(x, HEAD_DIM, 1)
            parts.append(x)
        o_ref[:, pair * LANES:(pair + 1) * LANES] = jnp.where(half == 0, parts[0], parts[1]).astype(o_ref.dtype)


def _nsa_prompt(q, gates, ckv, kvs_b, kvw_b, ov, e_all, tq):
    b, l, _ = q.shape
    nc = ckv.shape[1]
    nb = l // SLC_BLOCK
    blk = lambda w: pl.BlockSpec((None, tq, w), lambda bi, i: (bi, i, 0))
    seq = lambda a: pl.BlockSpec((None,) + a.shape[1:], lambda bi, i: (bi, 0, 0))
    full = lambda a: pl.BlockSpec(a.shape, lambda bi, i: (0, 0))
    return pl.pallas_call(
        functools.partial(_nsa_prompt_body, tq=tq, n_cmp=l // CMP_STRIDE - 1),
        grid=(b, l // tq),
        in_specs=[blk(NQ), blk(LANES), seq(ckv), seq(kvs_b), seq(kvw_b), full(ov), full(e_all)],
        out_specs=blk(NQ),
        out_shape=jax.ShapeDtypeStruct((b, l, NQ), BF16),
        scratch_shapes=[pltpu.VMEM((NSA_HEADS, tq, LANES), F32), pltpu.VMEM((NSA_HEADS, tq, LANES), F32),
                        pltpu.VMEM((NSA_HEADS, tq, LANES), F32)],
        compiler_params=_params("parallel", "arbitrary"),
        name="nsa_prompt",
    )(q, gates, ckv, kvs_b, kvw_b, ov, e_all)


def _nsa_sample_body(*refs, n_pages, s_new, n_cmp, p_len):
    pt_ref = refs[0]
    del pt_ref
    page_refs = refs[1:1 + n_pages]
    (q_ref, gate_ref, ckv_ref, win_ref, ns_ref, nw_ref, ov_ref, e_ref, o_ref) = refs[1 + n_pages:]
    rows = NSA_HEADS * s_new
    nc = ckv_ref.shape[0]
    nb = ov_ref.shape[1]
    w_buf = win_ref.shape[0]
    q = q_ref[...]
    qh = _head_queries(q, s_new)
    q2 = jnp.concatenate(qh, axis=0)
    gates = gate_ref[...]
    trow = _iota((rows, 1), 0) & (s_new - 1)
    grow = _div_pow2(_iota((rows, 1), 0), s_new * NSA_HPG)
    lane_half = _iota((rows, LANES), 1) >> 6

    def gate_rows(branch):
        cols = [gates[:, branch * NSA_HEADS + hh:branch * NSA_HEADS + hh + 1] for hh in range(NSA_HEADS)]
        return jnp.concatenate(cols, axis=0)

    def pick(acc, den):
        return jnp.where(lane_half == grow, acc, 0.0) / den

    ck = ckv_ref[:, 0:LANES]
    cv = ckv_ref[:, LANES:2 * LANES]
    ncol = _iota((rows, nc), 1)
    cvalid = (ncol * CMP_STRIDE + CMP_LEN <= p_len + trow + 1) & (ncol < n_cmp)
    p_c, o_c = _cmp_attend(q2, ck, cv, cvalid, nc)
    out = gate_rows(0) * jnp.where(lane_half == grow, o_c, 0.0)
    psum = []
    for g in range(NSA_GROUPS):
        acc = jnp.zeros((s_new, nc), F32)
        for h in range(NSA_HPG):
            r0 = (g * NSA_HPG + h) * s_new
            acc = acc + p_c[r0:r0 + s_new]
        psum.append(acc)
    imp = _dot01(jnp.concatenate(psum, axis=0), ov_ref[...])
    cur = jnp.full((NSA_GROUPS * s_new, 1), p_len // SLC_BLOCK, jnp.int32)
    sel = _topk_select(imp, cur, nb)
    sel_rows = jnp.concatenate([sel[g * s_new:(g + 1) * s_new] for g in range(NSA_GROUPS) for _ in range(NSA_HPG)],
                               axis=0).astype(BF16)

    kv = jnp.concatenate([r[...] for r in page_refs], axis=0).astype(BF16)
    s_past = _dot_nt(q2, kv[:, 0:LANES])
    s_past = s_past + (_dot(sel_rows, e_ref[...]) - 1.0) * (-NEG)
    ns = ns_ref[...].astype(BF16)
    s_nw = _dot_nt(q2, ns[:, 0:LANES])
    s_nw = jnp.where(_iota((rows, s_new), 1) <= trow, s_nw, NEG)
    m = jnp.maximum(jnp.max(s_past, axis=-1, keepdims=True), jnp.max(s_nw, axis=-1, keepdims=True))
    e1 = jnp.exp(s_past - m)
    e2 = jnp.exp(s_nw - m)
    den = jnp.sum(e1, axis=-1, keepdims=True) + jnp.sum(e2, axis=-1, keepdims=True)
    acc = _dot(e1.astype(BF16), kv[:, LANES:2 * LANES]) + _dot(e2.astype(BF16), ns[:, LANES:2 * LANES])
    out = out + gate_rows(1) * pick(acc, den)

    wb = win_ref[...].astype(BF16)
    nw = nw_ref[...].astype(BF16)
    s_b = _dot_nt(q2, wb[:, 0:LANES])
    dist = w_buf + trow - _iota((rows, w_buf), 1)
    s_b = jnp.where((dist < WINDOW) & (p_len - w_buf + _iota((rows, w_buf), 1) >= 0), s_b, NEG)
    s_n = _dot_nt(q2, nw[:, 0:LANES])
    s_n = jnp.where(_iota((rows, s_new), 1) <= trow, s_n, NEG)
    m = jnp.maximum(jnp.max(s_b, axis=-1, keepdims=True), jnp.max(s_n, axis=-1, keepdims=True))
    e1 = jnp.exp(s_b - m)
    e2 = jnp.exp(s_n - m)
    den = jnp.sum(e1, axis=-1, keepdims=True) + jnp.sum(e2, axis=-1, keepdims=True)
    acc = _dot(e1.astype(BF16), wb[:, LANES:2 * LANES]) + _dot(e2.astype(BF16), nw[:, LANES:2 * LANES])
    out = out + gate_rows(2) * pick(acc, den)

    for pair in range(NSA_HEADS // 2):
        parts = []
        for hh in (2 * pair, 2 * pair + 1):
            x = out[hh * s_new:(hh + 1) * s_new]
            if hh % 2 != hh // NSA_HPG:
                x = jnp.concatenate([x[:, HEAD_DIM:], x[:, :HEAD_DIM]], axis=1)
            parts.append(x)
        lh = _iota((s_new, LANES), 1) >> 6
        o_ref[:, pair * LANES:(pair + 1) * LANES] = jnp.where(lh == 0, parts[0], parts[1]).astype(o_ref.dtype)


def _nsa_sample(q, gates, ckv, cache_s, page_table, win_buf, new_s, new_w, ov, e_all):
    bd, s_new, _ = q.shape
    n_pages = page_table.shape[1]
    p_len = n_pages * PAGE_SIZE
    full = lambda a: pl.BlockSpec(a.shape, lambda i, pt: (0, 0))
    seq = lambda a: pl.BlockSpec((None,) + a.shape[1:], lambda i, pt: (i, 0, 0))
    page = lambda k: pl.BlockSpec((None, PAGE_SIZE, KV_W), lambda i, pt, k=k: (pt[i, k], 0, 0))
    return pl.pallas_call(
        functools.partial(_nsa_sample_body, n_pages=n_pages, s_new=s_new, n_cmp=p_len // CMP_STRIDE - 1, p_len=p_len),
        grid_spec=pltpu.PrefetchScalarGridSpec(
            num_scalar_prefetch=1,
            grid=(bd,),
            in_specs=[page(k) for k in range(n_pages)] + [seq(q), seq(gates), seq(ckv), seq(win_buf), seq(new_s),
                                                          seq(new_w), full(ov), full(e_all)],
            out_specs=pl.BlockSpec((None, s_new, NQ), lambda i, pt: (i, 0, 0)),
        ),
        out_shape=jax.ShapeDtypeStruct((bd, s_new, NQ), BF16),
        compiler_params=_params("parallel"),
        name="nsa_sample",
    )(page_table, *([cache_s] * n_pages), q, gates, ckv, win_buf, new_s, new_w, ov, e_all)


def _gla_body(q_ref, k_ref, v_ref, kt_ref, lg_ref, lgt_ref, r_ref, gn_ref, s0_ref, o_ref, sfin_ref, s_sc,
              *, tb, chunk, sub):
    j = pl.program_id(1)

    @pl.when(j == 0)
    def _():
        s_sc[...] = s0_ref[...]

    q = q_ref[...]
    k = k_ref[...]
    v = v_ref[...]
    kt = kt_ref[...]
    lg = lg_ref[...]
    lgt = lgt_ref[...]
    nchunk = tb // chunk
    nsub = chunk // sub
    ri = _iota((tb, tb), 0)
    ci = _iota((tb, tb), 1)
    same = _div_pow2(ri, chunk) == _div_pow2(ci, chunk)
    b = _dot01_l(jnp.where(same & (ci <= ri), 1.0, 0.0).astype(BF16), lg)
    bt = _dot01(lgt, jnp.where(same & (ri <= ci), 1.0, 0.0).astype(BF16))
    head_of_lane = _div_pow2(_iota((1, GQK), 1), GLA_DK)
    tcol = _iota((GQK, tb), 1)
    gn = gn_ref[...]

    for c in range(nchunk):
        c0 = c * chunk
        bc = b[c0:c0 + chunk]
        qc = q[c0:c0 + chunk]
        bt_last = bt[:, c0 + chunk - 1:c0 + chunk]
        qdb = qc * jnp.exp(bc)
        in_chunk = (tcol >= c0) & (tcol < c0 + chunk)
        ktl = jnp.where(in_chunk, kt * jnp.exp(jnp.where(in_chunk, bt_last - bt, 0.0)), 0.0).astype(BF16)
        s_all = s_sc[...].astype(BF16)
        o_heads = []
        for h in range(GLA_HEADS):
            hm = head_of_lane == h
            s_h = s_sc[h * GLA_DK:(h + 1) * GLA_DK]
            vh = v[:, h * GLA_DV:(h + 1) * GLA_DV].astype(BF16)
            o_h = _dot(jnp.where(hm, qdb, 0.0).astype(BF16), s_all)
            o_sub = []
            for sb in range(nsub):
                r0 = c0 + sb * sub
                if sb == 0:
                    ref_row = jnp.zeros((1, GQK), F32)
                    ref_col = jnp.zeros((GQK, 1), F32)
                else:
                    ref_row = b[r0 - 1:r0]
                    ref_col = bt[:, r0 - 1:r0]
                qd = jnp.where(hm, q[r0:r0 + sub] * jnp.exp(b[r0:r0 + sub] - ref_row), 0.0).astype(BF16)
                vis = (tcol >= c0) & (tcol < r0 + sub)
                kd = jnp.where(vis, kt * jnp.exp(jnp.where(vis, ref_col - bt, 0.0)), 0.0).astype(BF16)
                a = _dot(qd, kd)
                a = jnp.where(_iota((sub, tb), 1) <= r0 + _iota((sub, tb), 0), a, 0.0)
                o_sub.append(_dot(a.astype(BF16), vh))
            o_intra = o_sub[0] if nsub == 1 else jnp.concatenate(o_sub, axis=0)
            o_heads.append(o_h + o_intra)
            upd = _dot(ktl[h * GLA_DK:(h + 1) * GLA_DK], vh)
            s_sc[h * GLA_DK:(h + 1) * GLA_DK] = jnp.exp(bt_last[h * GLA_DK:(h + 1) * GLA_DK]) * s_h + upd
        for h in range(GLA_HEADS):
            o_ref[c0:c0 + chunk, h * GLA_DV:(h + 1) * GLA_DV] = (
                _rms(o_heads[h], gn) * r_ref[c0:c0 + chunk, h * GLA_DV:(h + 1) * GLA_DV].astype(F32)
            ).astype(o_ref.dtype)

    @pl.when(j == pl.num_programs(1) - 1)
    def _():
        sfin_ref[...] = s_sc[...]


def _gla(q, k, v, kt, lg, lgt, r, gn, s0, tb, chunk):
    s, t, _ = q.shape
    sub = min(GLA_SUB, chunk)
    row = lambda w: pl.BlockSpec((None, tb, w), lambda si, j: (si, j, 0))
    col = lambda w: pl.BlockSpec((None, w, tb), lambda si, j: (si, 0, j))
    return pl.pallas_call(
        functools.partial(_gla_body, tb=tb, chunk=chunk, sub=sub),
        grid=(s, t // tb),
        in_specs=[row(GQK), row(GQK), row(GV), col(GQK), row(GQK), col(GQK), row(GV),
                  pl.BlockSpec(gn.shape, lambda si, j: (0, 0)),
                  pl.BlockSpec((None, GQK, GLA_DV), lambda si, j: (si, 0, 0))],
        out_specs=[row(GV), pl.BlockSpec((None, GQK, GLA_DV), lambda si, j: (si, 0, 0))],
        out_shape=[jax.ShapeDtypeStruct((s, t, GV), BF16), jax.ShapeDtypeStruct((s, GQK, GLA_DV), F32)],
        scratch_shapes=[pltpu.VMEM((GQK, GLA_DV), F32)],
        compiler_params=_params("parallel", "arbitrary"),
        name="gla",
    )(q, k, v, kt, lg, lgt, r, gn, s0)


def _mix_body(x_ref, sc_ref, sh_ref, ga_ref, gpre_ref, gpost_ref, on_ref, og_ref, wm_ref, wn_ref, wgo_ref, wo_ref,
              o_ref):
    x = x_ref[...]
    d = x.shape[1]
    h = _norm_mod(x, gpre_ref[...], sc_ref[...], sh_ref[...]).astype(BF16)
    m = _sigmoid(_dot(h, wm_ref[...]))
    y_a = _dot(on_ref[...], wn_ref[...])
    y_b = _dot(og_ref[...], wgo_ref[...])
    mixin = (m[:, 0:d] * y_a + m[:, d:2 * d] * y_b).astype(BF16)
    mix = _dot(mixin, wo_ref[...])
    o_ref[...] = x + ga_ref[...] * _rms(mix, gpost_ref[...])


def _mix(x, sc, sh, ga, gpre, gpost, o_nsa, o_gla, wm, wn, wgo, wo, tm, rows_per_mod):
    n, d = x.shape
    row = lambda w: pl.BlockSpec((tm, w), lambda i: (i, 0))
    full = lambda a: pl.BlockSpec(a.shape, lambda i: (0, 0))
    ms = lambda a: _mod_spec(a, tm, rows_per_mod)
    return pl.pallas_call(
        _mix_body,
        grid=(n // tm,),
        in_specs=[row(d), ms(sc), ms(sh), ms(ga), full(gpre), full(gpost), row(NQ), row(GV),
                  full(wm), full(wn), full(wgo), full(wo)],
        out_specs=row(d),
        out_shape=jax.ShapeDtypeStruct((n, d), F32),
        compiler_params=_params("parallel"),
        name="mix",
    )(x, sc, sh, ga, gpre, gpost, o_nsa, o_gla, wm, wn, wgo, wo)


def _ffn_body(x_ref, sc_ref, sh_ref, ga_ref, gpre_ref, gpost_ref, wg_ref, wu_ref, wd_ref, o_ref, h_sc, acc_sc):
    j = pl.program_id(1)

    @pl.when(j == 0)
    def _():
        h_sc[...] = _norm_mod(x_ref[...], gpre_ref[...], sc_ref[...], sh_ref[...]).astype(BF16)
        acc_sc[...] = jnp.zeros(acc_sc.shape, F32)

    h = h_sc[...]
    gt = _dot(h, wg_ref[...])
    up = _dot(h, wu_ref[...])
    a = (gt * _sigmoid(gt) * up).astype(BF16)
    acc_sc[...] += _dot(a, wd_ref[...])

    @pl.when(j == pl.num_programs(1) - 1)
    def _():
        o_ref[...] = x_ref[...] + ga_ref[...] * _rms(acc_sc[...], gpost_ref[...])


def _ffn(x, sc, sh, ga, gpre, gpost, wg, wu, wd, tm, rows_per_mod):
    n, d = x.shape
    dff = wg.shape[1]
    tf = dff // 2 if (dff // 2) % LANES == 0 else dff
    row = pl.BlockSpec((tm, d), lambda i, j: (i, 0))
    full = lambda a: pl.BlockSpec(a.shape, lambda i, j: (0, 0))
    ms = lambda a: pl.BlockSpec((None, a.shape[1], a.shape[2]), lambda i, j: ((i * tm) // rows_per_mod, 0, 0))
    return pl.pallas_call(
        _ffn_body,
        grid=(n // tm, dff // tf),
        in_specs=[row, ms(sc), ms(sh), ms(ga), full(gpre), full(gpost),
                  pl.BlockSpec((d, tf), lambda i, j: (0, j)), pl.BlockSpec((d, tf), lambda i, j: (0, j)),
                  pl.BlockSpec((tf, d), lambda i, j: (j, 0))],
        out_specs=row,
        out_shape=jax.ShapeDtypeStruct((n, d), F32),
        scratch_shapes=[pltpu.VMEM((tm, d), BF16), pltpu.VMEM((tm, d), F32)],
        compiler_params=_params("parallel", "arbitrary"),
        name="ffn",
    )(x, sc, sh, ga, gpre, gpost, wg, wu, wd)


def _split_w_in(w_in, d):
    sizes = (NQ, KV_W, KV_W, KV_W, 3 * NSA_HEADS, GQK, GQK, GV, 16, GV, 2 * d)
    out, o = [], 0
    for s in sizes:
        out.append(w_in[:, o:o + s])
        o += s
    return out


def _pad_cols(w, n):
    return jnp.pad(w, ((0, 0), (0, n - w.shape[1])))


def _compress_weights(w1k, b1k, w2k, pek, w1v, b1v, w2v, pev):
    hid = w2k.shape[0]

    def expand_w1(w1, kv, half):
        w = w1.reshape(CMP_LEN, HEAD_DIM, hid)[half * CMP_STRIDE:(half + 1) * CMP_STRIDE]
        out = jnp.zeros((CMP_STRIDE, 2, NSA_GROUPS, HEAD_DIM, 2, NSA_GROUPS, hid), F32)
        for g in range(NSA_GROUPS):
            out = out.at[:, kv, g, :, kv, g, :].set(w)
        return out.reshape(CMP_STRIDE * KV_W, 2 * NSA_GROUPS * hid)

    def expand_pe(pe, half):
        p = pe[half * CMP_STRIDE:(half + 1) * CMP_STRIDE]
        return jnp.broadcast_to(p[:, None, :], (CMP_STRIDE, NSA_GROUPS, HEAD_DIM))

    w1a = (expand_w1(w1k, 0, 0) + expand_w1(w1v, 1, 0)).astype(BF16)
    w1b = (expand_w1(w1k, 0, 1) + expand_w1(w1v, 1, 1)).astype(BF16)
    pea = jnp.stack([expand_pe(pek, 0), expand_pe(pev, 0)], axis=1).reshape(1, CMP_STRIDE * KV_W)
    peb = jnp.stack([expand_pe(pek, 1), expand_pe(pev, 1)], axis=1).reshape(1, CMP_STRIDE * KV_W)
    b1 = jnp.concatenate([jnp.tile(b1k, NSA_GROUPS), jnp.tile(b1v, NSA_GROUPS)]).reshape(1, -1)
    w2 = jnp.zeros((2, NSA_GROUPS, hid, 2, NSA_GROUPS, HEAD_DIM), F32)
    for g in range(NSA_GROUPS):
        w2 = w2.at[0, g, :, 0, g, :].set(w2k)
        w2 = w2.at[1, g, :, 1, g, :].set(w2v)
    w2 = w2.reshape(2 * NSA_GROUPS * hid, KV_W).astype(BF16)
    return pea, peb, w1a, w1b, b1, w2


def _overlap(n_cmp_pad, n_cmp, nb):
    n = jnp.arange(n_cmp_pad)[:, None]
    j = jnp.arange(nb)[None, :] * SLC_BLOCK
    ok = (n * CMP_STRIDE < j + SLC_BLOCK) & (n * CMP_STRIDE + CMP_LEN > j) & (n < n_cmp)
    return ok.astype(BF16)


def _block_expand(nb, nkeys):
    return (jnp.arange(nb)[:, None] == (jnp.arange(nkeys)[None, :] // SLC_BLOCK)).astype(BF16)


def _tile_rows(n, cap):
    t = cap
    while n % t:
        t //= 2
    return t


def kernel(x_prompt, x_sample, cache_cmp_kv, cache_slc_kv, state_win_kv, state_gla, page_table, c_prompt, c_sample, w_ada, b_ada, g_pre_mix, g_post_mix, w_in, cmp_w1_k, cmp_b1_k, cmp_w2_k, cmp_pe_k, cmp_w1_v, cmp_b1_v, cmp_w2_v, cmp_pe_v, w_gla_a2, b_gla_a, gla_norm_g, w_nsa_o, w_gla_o, w_out, g_pre_ffn, g_post_ffn, w_ff_gate, w_ff_up, w_ff_down):
    depth = w_ada.shape[0]
    bp, l, d = x_prompt.shape
    bd, s_new, _ = x_sample.shape
    n_pool = cache_cmp_kv.shape[1]
    n_pages = page_table.shape[1]
    p_len = n_pages * PAGE_SIZE
    w_buf = state_win_kv.shape[2]
    keep = min(WINDOW, l)
    np_, ns_ = bp * l, bd * s_new
    tmp = _tile_rows(np_ // bp, 512)
    tms = _tile_rows(ns_, 512)
    tq = _tile_rows(l, 256)

    yp, ys = x_prompt.reshape(np_, d), x_sample.reshape(ns_, d)
    col = [[] for _ in range(8)]
    for li in range(depth):
        r_all = bp + bd
        r_pad = -(-r_all // 8) * 8
        c_all = jnp.pad(jnp.concatenate([c_prompt, c_sample], axis=0), ((0, r_pad - r_all), (0, 0)))
        mod = _ada(c_all, w_ada[li], b_ada[li])
        mods_p = [m.reshape(bp, 1, d) for m in jnp.split(mod[:bp], 6, axis=-1)]
        mods_s = [jnp.repeat(m, s_new, axis=0).reshape(ns_ // tms, tms, d) for m in jnp.split(mod[bp:r_all], 6, axis=-1)]

        (w_q, w_kc, w_ks, w_kw, w_gn, w_qg, w_kg, w_vg, w_ag, w_rg, w_mg) = _split_w_in(w_in[li], d)
        wq = w_q.astype(BF16)
        wkv = jnp.concatenate([w_kc, w_ks, w_kw], axis=1).astype(BF16)
        wgn = _pad_cols(w_gn, LANES).astype(BF16)
        wqkv = jnp.concatenate([w_qg, w_kg, w_vg], axis=1).astype(BF16)
        wkt = w_kg.T.astype(BF16)
        wa = _pad_cols(w_ag, LANES).astype(BF16)
        wat = wa.T
        wr = w_rg.astype(BF16)
        a2 = jnp.pad(w_gla_a2[li], ((0, LANES - w_gla_a2.shape[1]), (0, 0)))
        a2t = a2.T
        ba = b_gla_a[li].reshape(1, GQK)
        bat = b_gla_a[li].reshape(GQK, 1)
        cw = _compress_weights(cmp_w1_k[li], cmp_b1_k[li], cmp_w2_k[li], cmp_pe_k[li],
                               cmp_w1_v[li], cmp_b1_v[li], cmp_w2_v[li], cmp_pe_v[li])
        gpre = g_pre_mix[li].reshape(1, d)
        gpost = g_post_mix[li].reshape(1, d)
        gpre2 = g_pre_ffn[li].reshape(1, d)
        gpost2 = g_post_ffn[li].reshape(1, d)
        gn = gla_norm_g[li].reshape(1, GLA_DV)
        wm = w_mg.astype(BF16)
        wn = w_nsa_o[li].astype(BF16)
        wgo = w_gla_o[li].astype(BF16)
        wo = w_out[li].astype(BF16)
        wfg = w_ff_gate[li].astype(BF16)
        wfu = w_ff_up[li].astype(BF16)
        wfd = w_ff_down[li].astype(BF16)

        sh1, sc1, ga1, sh2, sc2, ga2 = mods_p
        q, kvc, kvs, kvw, kvs_b, kvw_b, gates = _pre_nsa(yp, sc1, sh1, gpre, wq, wkv, wgn, tmp, l)
        qg, kg, vg, kgt, lg, lgt, rg = _pre_gla(yp, sc1, sh1, gpre, wqkv, wkt, wa, wat, wr, a2, a2t, ba, bat, tmp, l)
        ckv = _cmp_prompt(kvc.reshape(bp, l, KV_W), cw)
        nch = l // CMP_STRIDE
        nb = l // SLC_BLOCK
        o_nsa = _nsa_prompt(q.reshape(bp, l, NQ), gates.reshape(bp, l, LANES), ckv, kvs_b.reshape(bp, l, KV_W),
                            kvw_b.reshape(bp, l, KV_W), _overlap(nch, nch - 1, nb), _block_expand(nb, l), tq)
        chunk = GLA_CHUNK if l % GLA_CHUNK == 0 else l
        tb = LANES if (l % LANES == 0 and LANES % chunk == 0) else chunk
        kgt3 = kgt.reshape(GQK, bp, l).transpose(1, 0, 2)
        lgt3 = lgt.reshape(GQK, bp, l).transpose(1, 0, 2)
        o_gla, s_fin_p = _gla(qg.reshape(bp, l, GQK), kg.reshape(bp, l, GQK), vg.reshape(bp, l, GV), kgt3,
                              lg.reshape(bp, l, GQK), lgt3, rg.reshape(bp, l, GV), gn,
                              jnp.zeros((bp, GQK, GLA_DV), F32), tb, chunk)
        x1 = _mix(yp, sc1, sh1, ga1, gpre, gpost, o_nsa.reshape(np_, NQ), o_gla.reshape(np_, GV), wm, wn, wgo, wo, tmp, l)
        yp = _ffn(x1, sc2, sh2, ga2, gpre2, gpost2, wfg, wfu, wfd, tmp, l)
        col[0].append(kvc.reshape(bp, l, 2, NSA_GROUPS, HEAD_DIM))
        col[2].append(kvs.reshape(bp, l, 2, NSA_GROUPS, HEAD_DIM))
        col[4].append(kvw.reshape(bp, l, KV_W)[:, l - keep:].reshape(bp, keep, 2, NSA_GROUPS, HEAD_DIM))
        col[6].append(s_fin_p.reshape(bp, GLA_HEADS, GLA_DK, GLA_DV).astype(state_gla.dtype))

        sh1, sc1, ga1, sh2, sc2, ga2 = mods_s
        q, kvc, kvs, kvw, _, _, gates = _pre_nsa(ys, sc1, sh1, gpre, wq, wkv, wgn, tms, tms)
        qg, kg, vg, kgt, lg, lgt, rg = _pre_gla(ys, sc1, sh1, gpre, wqkv, wkt, wa, wat, wr, a2, a2t, ba, bat, tms, tms)
        cache_c = cache_cmp_kv[li].reshape(n_pool, PAGE_SIZE, KV_W)
        cache_s = cache_slc_kv[li].reshape(n_pool, PAGE_SIZE, KV_W)
        ckv = _cmp_sample(cache_c, page_table, cw)
        nch = p_len // CMP_STRIDE
        nb = p_len // SLC_BLOCK
        win_buf = state_win_kv[li].reshape(bd, w_buf, KV_W)
        o_nsa = _nsa_sample(q.reshape(bd, s_new, NQ), gates.reshape(bd, s_new, LANES), ckv, cache_s, page_table,
                            win_buf, kvs.reshape(bd, s_new, KV_W), kvw.reshape(bd, s_new, KV_W),
                            _overlap(nch, nch - 1, nb), _block_expand(nb, p_len))
        chunk = GLA_CHUNK if s_new % GLA_CHUNK == 0 else s_new
        kgt3 = kgt.reshape(GQK, bd, s_new).transpose(1, 0, 2)
        lgt3 = lgt.reshape(GQK, bd, s_new).transpose(1, 0, 2)
        o_gla, s_fin_s = _gla(qg.reshape(bd, s_new, GQK), kg.reshape(bd, s_new, GQK), vg.reshape(bd, s_new, GV), kgt3,
                              lg.reshape(bd, s_new, GQK), lgt3, rg.reshape(bd, s_new, GV), gn,
                              state_gla[li].astype(F32).reshape(bd, GQK, GLA_DV), chunk, chunk)
        x1 = _mix(ys, sc1, sh1, ga1, gpre, gpost, o_nsa.reshape(ns_, NQ), o_gla.reshape(ns_, GV), wm, wn, wgo, wo, tms, tms)
        ys = _ffn(x1, sc2, sh2, ga2, gpre2, gpost2, wfg, wfu, wfd, tms, tms)
        win_new = jnp.concatenate([win_buf, kvw.reshape(bd, s_new, KV_W)], axis=1)[:, s_new:]
        col[1].append(kvc.reshape(bd, s_new, 2, NSA_GROUPS, HEAD_DIM))
        col[3].append(kvs.reshape(bd, s_new, 2, NSA_GROUPS, HEAD_DIM))
        col[5].append(win_new.reshape(bd, w_buf, 2, NSA_GROUPS, HEAD_DIM))
        col[7].append(s_fin_s.reshape(bd, GLA_HEADS, GLA_DK, GLA_DV).astype(state_gla.dtype))

    stacked = [jnp.stack(c) for c in col]
    return (yp.reshape(bp, l, d), ys.reshape(bd, s_new, d), *stacked)
```

```python
import functools
import math

import numpy as np
import jax
import jax.numpy as jnp
from jax import lax
from jax.experimental import pallas as pl
from jax.experimental.pallas import tpu as pltpu

F32 = jnp.float32
BF16 = jnp.bfloat16

NSA_HEADS = 8
NSA_GROUPS = 2
NSA_HPG = NSA_HEADS // NSA_GROUPS
HEAD_DIM = 64
CMP_STRIDE = 16
CMP_LEN = 32
SLC_BLOCK = 64
N_SEL = 16
WINDOW = 512
GLA_HEADS = 4
GLA_DK = 64
GLA_DV = 128
GLA_TAU = 16.0
GLA_CHUNK = 64
GLA_SUB = 16
EPS = 1e-6
PAGE_SIZE = 128
NEG = -1e30
LOG2E = math.log2(math.e)
LANES = 128
VMEM_LIMIT = 56 * 1024 * 1024

KV_W = 2 * NSA_GROUPS * HEAD_DIM
NQ = NSA_HEADS * HEAD_DIM
GQK = GLA_HEADS * GLA_DK
GV = GLA_HEADS * GLA_DV


def _dot(a, b):
    return jnp.dot(a, b, preferred_element_type=F32)


def _dot_nt(a, b):
    return lax.dot_general(a, b, (((1,), (1,)), ((), ())), preferred_element_type=F32)


def _dot_f32(a, b):
    return jnp.dot(a, b, preferred_element_type=F32, precision=lax.Precision.HIGHEST)


def _dot01(x, m01):
    hi = x.astype(BF16)
    r1 = x - hi.astype(F32)
    mid = r1.astype(BF16)
    lo = (r1 - mid.astype(F32)).astype(BF16)
    return _dot(hi, m01) + _dot(mid, m01) + _dot(lo, m01)


def _dot01_l(m01, x):
    hi = x.astype(BF16)
    r1 = x - hi.astype(F32)
    mid = r1.astype(BF16)
    lo = (r1 - mid.astype(F32)).astype(BF16)
    return _dot(m01, hi) + _dot(m01, mid) + _dot(m01, lo)


def _sigmoid(x):
    return 1.0 / (1.0 + jnp.exp(-x))


def _params(*sem):
    return pltpu.CompilerParams(dimension_semantics=sem, vmem_limit_bytes=VMEM_LIMIT)


def _iota(shape, dim):
    return lax.broadcasted_iota(jnp.int32, shape, dim)


def _div_pow2(x, n):
    assert n & (n - 1) == 0, n
    return x >> (n.bit_length() - 1)


def _ada_body(c_ref, w_ref, b_ref, o_ref):
    o_ref[...] = _dot_f32(c_ref[...], w_ref[...]) + b_ref[...]


def _ada(c_all, w_ada, b_ada):
    r, d = c_all.shape
    n = w_ada.shape[1]
    tn = 1536 if n % 1536 == 0 else n
    return pl.pallas_call(
        _ada_body,
        grid=(n // tn,),
        in_specs=[pl.BlockSpec((r, d), lambda j: (0, 0)),
                  pl.BlockSpec((d, tn), lambda j: (0, j)),
                  pl.BlockSpec((1, tn), lambda j: (0, j))],
        out_specs=pl.BlockSpec((r, tn), lambda j: (0, j)),
        out_shape=jax.ShapeDtypeStruct((r, n), F32),
        compiler_params=_params("arbitrary"),
        name="ada",
    )(c_all, w_ada, b_ada.reshape(1, n))


def _norm_mod(x, g, sc, sh):
    ms = jnp.mean(x * x, axis=-1, keepdims=True)
    y = x * lax.rsqrt(ms + EPS) * g
    return y * (1.0 + sc) + sh


def _rms(x, g):
    ms = jnp.mean(x * x, axis=-1, keepdims=True)
    return x * lax.rsqrt(ms + EPS) * g


def _pre_nsa_body(x_ref, sc_ref, sh_ref, g_ref, wq_ref, wkv_ref, wg_ref,
                  q_ref, kvc_ref, kvs_ref, kvw_ref, kvsb_ref, kvwb_ref, gate_ref):
    h = _norm_mod(x_ref[...], g_ref[...], sc_ref[...], sh_ref[...]).astype(BF16)
    q_ref[...] = (_dot(h, wq_ref[...]) * (HEAD_DIM ** -0.5 * LOG2E)).astype(BF16)
    kv = _dot(h, wkv_ref[...])
    kvc = kv[:, 0:KV_W]
    kvs = kv[:, KV_W:2 * KV_W]
    kvw = kv[:, 2 * KV_W:3 * KV_W]
    kvc_ref[...] = kvc
    kvs_ref[...] = kvs
    kvw_ref[...] = kvw
    kvsb_ref[...] = kvs.astype(BF16)
    kvwb_ref[...] = kvw.astype(BF16)
    gate_ref[...] = _sigmoid(_dot(h, wg_ref[...]))


def _mod_spec(mod3, tm, rows_per_mod):
    r = mod3.shape[1]
    return pl.BlockSpec((None, r, mod3.shape[2]), lambda i: ((i * tm) // rows_per_mod, 0, 0))


def _pre_nsa(x, sc, sh, g, wq, wkv, wg, tm, rows_per_mod):
    n, d = x.shape
    row = lambda w: pl.BlockSpec((tm, w), lambda i: (i, 0))
    full = lambda a: pl.BlockSpec(a.shape, lambda i: (0, 0))
    outs = [(NQ, BF16), (KV_W, F32), (KV_W, F32), (KV_W, F32), (KV_W, BF16), (KV_W, BF16), (LANES, F32)]
    return pl.pallas_call(
        _pre_nsa_body,
        grid=(n // tm,),
        in_specs=[row(d), _mod_spec(sc, tm, rows_per_mod), _mod_spec(sh, tm, rows_per_mod), full(g),
                  full(wq), full(wkv), full(wg)],
        out_specs=[row(w) for w, _ in outs],
        out_shape=[jax.ShapeDtypeStruct((n, w), t) for w, t in outs],
        compiler_params=_params("parallel"),
        name="pre_nsa",
    )(x, sc, sh, g, wq, wkv, wg)


def _log_sigmoid(x):
    return jnp.minimum(x, 0.0) - jnp.log(1.0 + jnp.exp(-jnp.abs(x)))


def _pre_gla_body(x_ref, sc_ref, sh_ref, g_ref, wqkv_ref, wkt_ref, wa_ref, wat_ref, wr_ref, a2_ref, a2t_ref,
                  ba_ref, bat_ref, q_ref, k_ref, v_ref, kt_ref, lg_ref, lgt_ref, r_ref):
    h = _norm_mod(x_ref[...], g_ref[...], sc_ref[...], sh_ref[...]).astype(BF16)
    qkv = _dot(h, wqkv_ref[...])
    q_ref[...] = qkv[:, 0:GQK] * GLA_DK ** -0.5
    k_ref[...] = qkv[:, GQK:2 * GQK]
    v_ref[...] = qkv[:, 2 * GQK:2 * GQK + GV]
    kt_ref[...] = _dot_nt(wkt_ref[...], h)
    a = _dot(h, wa_ref[...])
    at = _dot_nt(wat_ref[...], h)
    lg_ref[...] = _log_sigmoid(_dot_f32(a, a2_ref[...]) + ba_ref[...]) * (1.0 / GLA_TAU)
    lgt_ref[...] = _log_sigmoid(_dot_f32(a2t_ref[...], at) + bat_ref[...]) * (1.0 / GLA_TAU)
    r = _dot(h, wr_ref[...])
    r_ref[...] = (r * _sigmoid(r)).astype(BF16)


def _pre_gla(x, sc, sh, g, wqkv, wkt, wa, wat, wr, a2, a2t, ba, bat, tm, rows_per_mod):
    n, d = x.shape
    row = lambda w: pl.BlockSpec((tm, w), lambda i: (i, 0))
    col = lambda w: pl.BlockSpec((w, tm), lambda i: (0, i))
    full = lambda a: pl.BlockSpec(a.shape, lambda i: (0, 0))
    return pl.pallas_call(
        _pre_gla_body,
        grid=(n // tm,),
        in_specs=[row(d), _mod_spec(sc, tm, rows_per_mod), _mod_spec(sh, tm, rows_per_mod), full(g),
                  full(wqkv), full(wkt), full(wa), full(wat), full(wr), full(a2), full(a2t), full(ba), full(bat)],
        out_specs=[row(GQK), row(GQK), row(GV), col(GQK), row(GQK), col(GQK), row(GV)],
        out_shape=[jax.ShapeDtypeStruct((n, GQK), F32), jax.ShapeDtypeStruct((n, GQK), F32),
                   jax.ShapeDtypeStruct((n, GV), F32), jax.ShapeDtypeStruct((GQK, n), F32),
                   jax.ShapeDtypeStruct((n, GQK), F32), jax.ShapeDtypeStruct((GQK, n), F32),
                   jax.ShapeDtypeStruct((n, GV), BF16)],
        compiler_params=_params("parallel"),
        name="pre_gla",
    )(x, sc, sh, g, wqkv, wkt, wa, wat, wr, a2, a2t, ba, bat)


def _gelu_tanh(x):
    return 0.5 * x * (1.0 + jnp.tanh(0.7978845608028654 * (x + 0.044715 * x * x * x)))


def _compress(x_refs, nch, pea_ref, peb_ref, w1a_ref, w1b_ref, b1_ref, w2_ref):
    first, second = [], []
    for kv, x_ref in enumerate(x_refs):
        fa = jnp.zeros((nch, LANES), F32)
        sa = jnp.zeros((nch, LANES), F32)
        for p in range(CMP_STRIDE):
            xp = x_ref[pl.ds(p, nch, stride=CMP_STRIDE), :]
            fa = fa + _dot((xp + pea_ref[p:p + 1, kv * LANES:(kv + 1) * LANES]).astype(BF16), w1a_ref[p, kv])
            sa = sa + _dot((xp + peb_ref[p:p + 1, kv * LANES:(kv + 1) * LANES]).astype(BF16), w1b_ref[p, kv])
        first.append(fa)
        second.append(sa)
    first = jnp.concatenate(first, axis=1)
    second = jnp.concatenate(second, axis=1)
    if nch % 8 == 0:
        nxt = pltpu.roll(second, nch - 1, 0)
    else:
        nxt = jnp.concatenate([second[1:], second[:1]], axis=0)
    hid = _gelu_tanh(first + nxt + b1_ref[...])
    out = _dot(hid.astype(BF16), w2_ref[...])
    return jnp.where(_iota(out.shape, 0) < nch - 1, out, 0.0)


def _cmp_prompt_body(xk_ref, xv_ref, pea_ref, peb_ref, w1a_ref, w1b_ref, b1_ref, w2_ref, o_ref, *, nch):
    o_ref[...] = _compress((xk_ref, xv_ref), nch, pea_ref, peb_ref, w1a_ref, w1b_ref, b1_ref,
                           w2_ref).astype(o_ref.dtype)


def _cmp_prompt(kvc, cw):
    b, l, _ = kvc.shape
    nch = l // CMP_STRIDE
    full = lambda a: pl.BlockSpec(a.shape, lambda i: (0,) * a.ndim)
    return pl.pallas_call(
        functools.partial(_cmp_prompt_body, nch=nch),
        grid=(b,),
        in_specs=[pl.BlockSpec((None, l, LANES), lambda i: (i, 0, 0)),
                  pl.BlockSpec((None, l, LANES), lambda i: (i, 0, 1))] + [full(a) for a in cw],
        out_specs=pl.BlockSpec((None, nch, KV_W), lambda i: (i, 0, 0)),
        out_shape=jax.ShapeDtypeStruct((b, nch, KV_W), BF16),
        compiler_params=_params("parallel"),
        name="cmp_prompt",
    )(kvc, kvc, *cw)


def _cmp_sample_body(*refs, n_pages, nch):
    page_refs = refs[1:1 + n_pages]
    pea_ref, peb_ref, w1a_ref, w1b_ref, b1_ref, w2_ref, o_ref, xk_sc, xv_sc = refs[1 + n_pages:]
    for k in range(n_pages):
        xk_sc[k * PAGE_SIZE:(k + 1) * PAGE_SIZE, :] = page_refs[k][0:LANES, :].T
        xv_sc[k * PAGE_SIZE:(k + 1) * PAGE_SIZE, :] = page_refs[k][LANES:2 * LANES, :].T
    o_ref[...] = _compress((xk_sc, xv_sc), nch, pea_ref, peb_ref, w1a_ref, w1b_ref, b1_ref,
                           w2_ref).astype(o_ref.dtype)


def _cmp_sample(cache_t, page_table, cw):
    bd, n_pages = page_table.shape
    p_len = n_pages * PAGE_SIZE
    nch = p_len // CMP_STRIDE
    full = lambda a: pl.BlockSpec(a.shape, lambda i, pt: (0,) * a.ndim)
    page = lambda k: pl.BlockSpec((None, KV_W, PAGE_SIZE), lambda i, pt, k=k: (pt[i, k], 0, 0))
    return pl.pallas_call(
        functools.partial(_cmp_sample_body, n_pages=n_pages, nch=nch),
        grid_spec=pltpu.PrefetchScalarGridSpec(
            num_scalar_prefetch=1,
            grid=(bd,),
            in_specs=[page(k) for k in range(n_pages)] + [full(a) for a in cw],
            out_specs=pl.BlockSpec((None, nch, KV_W), lambda i, pt: (i, 0, 0)),
            scratch_shapes=[pltpu.VMEM((p_len, LANES), F32), pltpu.VMEM((p_len, LANES), F32)],
        ),
        out_shape=jax.ShapeDtypeStruct((bd, nch, KV_W), BF16),
        compiler_params=_params("parallel"),
        name="cmp_sample",
    )(page_table, *([cache_t] * n_pages), *cw)


def _head_queries(q, rows):
    half = _iota((rows, LANES), 1) >> 6
    out = []
    for hh in range(NSA_HEADS):
        g = hh // NSA_HPG
        blk = q[:, (hh // 2) * LANES:(hh // 2 + 1) * LANES].astype(F32)
        if hh % 2 != g:
            blk = pltpu.roll(blk, HEAD_DIM, 1) if rows % 8 == 0 else jnp.roll(blk, HEAD_DIM, 1)
        out.append(jnp.where(half == g, blk, 0.0).astype(BF16))
    return out


def _topk_select(imp, cur, nb):
    shape = imp.shape
    jl = _iota(shape, 1)
    jf = jl.astype(F32)
    forced = (jl == 0) | (jl == cur) | (jl == cur - 1)
    elig = (jl >= 1) & (jl <= cur - 2)
    sel0 = jnp.where(forced, 1.0, 0.0)
    alive0 = jnp.where(elig, 1.0, 0.0)

    def step(_, carry):
        sel, alive = carry
        live = alive > 0.0
        vals = jnp.where(live, imp, -1.0)
        m = jnp.max(vals, axis=-1, keepdims=True)
        cand = live & (vals == m)
        idx = jnp.min(jnp.where(cand, jf, float(nb)), axis=-1, keepdims=True)
        one = jf == idx
        return jnp.where(one, 1.0, sel), jnp.where(one, 0.0, alive)

    sel, _ = lax.fori_loop(0, N_SEL - 3, step, (sel0, alive0))
    return sel


def _masked_softmax2(s, valid):
    s = jnp.where(valid, s, NEG)
    m = jnp.max(s, axis=-1, keepdims=True)
    e = jnp.where(valid, jnp.exp2(s - m), 0.0)
    d = jnp.sum(e, axis=-1, keepdims=True)
    return e / jnp.where(d > 0.0, d, 1.0)


def _nsa_prompt_body(q_ref, gate_ref, ckv_ref, kvs_ref, kvw_ref, ov_ref, et_ref, o_ref, qa_sc, m_sc, acc_sc, out_sc,
                     *, tq, tk, n_cmp):
    i = pl.program_id(1)
    t0 = i * tq
    nc = ckv_ref.shape[0]
    nb = ov_ref.shape[1]
    nh = NSA_HEADS
    qh = _head_queries(q_ref[...], tq)
    for hh in range(nh):
        qa_sc[hh * tq:(hh + 1) * tq, 0:LANES] = qh[hh]
    half = _iota((tq, LANES), 1) >> 6
    trow = t0 + _iota((1, tq, 1), 1)
    gates = gate_ref[...]

    def gate(branch, hh):
        c = branch * nh + hh
        return gates[:, c:c + 1]

    ck = ckv_ref[:, 0:LANES]
    cv = ckv_ref[:, LANES:2 * LANES]
    s = _dot_nt(qa_sc[:, 0:LANES], ck).reshape(nh, tq, nc)
    ncol = _iota((1, tq, nc), 2)
    cvalid = (ncol * CMP_STRIDE + CMP_LEN <= trow + 1) & (ncol < n_cmp)
    p = _masked_softmax2(s, cvalid)
    for hh in range(nh):
        out_sc[hh] = gate(0, hh) * _dot(p[hh].astype(BF16), cv)
    ov = ov_ref[...]
    imp = []
    for g in range(NSA_GROUPS):
        psum = p[g * NSA_HPG]
        for h in range(1, NSA_HPG):
            psum = psum + p[g * NSA_HPG + h]
        imp.append(_dot01(psum, ov))
    imp = jnp.concatenate(imp, axis=0)
    trow2 = t0 + _iota((tq, 1), 0)
    cur = jnp.concatenate([trow2, trow2], axis=0) >> 6
    sel = _topk_select(imp, cur, nb)
    selneg = jnp.where(sel > 0.0, 0.0, NEG).astype(BF16)
    for hh in range(nh):
        g = hh // NSA_HPG
        qa_sc[hh * tq:(hh + 1) * tq, LANES:LANES + nb] = selneg[g * tq:(g + 1) * tq]

    def load_v(ref, start, width):
        v128 = ref[pl.ds(start, width), LANES:2 * LANES]
        lane_half = _iota(v128.shape, 1) >> 6
        return [jnp.where(lane_half == g, v128, jnp.ones_like(v128)) for g in range(NSA_GROUPS)]

    def online(s3, vaug, first):
        width = s3.shape[-1]
        tmax = jnp.max(s3, axis=-1, keepdims=True)
        if first:
            m_new = jnp.broadcast_to(tmax, (nh, tq, LANES))
        else:
            m_old = m_sc[...]
            m_new = jnp.maximum(m_old, tmax)
            alpha = jnp.exp2(m_old - m_new)
        pexp = jnp.concatenate([jnp.exp2((s3[:, :, c * LANES:(c + 1) * LANES] - m_new).astype(BF16))
                                for c in range(width // LANES)], axis=-1)
        for hh in range(nh):
            pv = _dot(pexp[hh], vaug[hh // NSA_HPG])
            acc_sc[hh] = pv if first else alpha[hh] * acc_sc[hh] + pv
        m_sc[...] = m_new

    def finish(branch):
        for hh in range(nh):
            acc = acc_sc[hh]
            den = pltpu.roll(acc, HEAD_DIM, 1)
            out_sc[hh] = out_sc[hh] + gate(branch, hh) * (acc / den)

    def slc_scores(start):
        kaug = jnp.concatenate([kvs_ref[pl.ds(start, tk), 0:LANES], et_ref[pl.ds(start, tk), :]], axis=1)
        return _dot_nt(qa_sc[...], kaug).reshape(nh, tq, tk)

    n_full = i // (tk // tq)
    last = pl.multiple_of(n_full * tk, tk)
    s3 = slc_scores(last)
    s3 = jnp.where(last + _iota((1, tq, tk), 2) <= trow, s3, NEG)
    online(s3, load_v(kvs_ref, last, tk), True)

    def slc_step(j, carry):
        start = pl.multiple_of(j * tk, tk)
        online(slc_scores(start), load_v(kvs_ref, start, tk), False)
        return carry

    lax.fori_loop(0, n_full, slc_step, 0)
    finish(1)

    d0 = pl.multiple_of(t0, tq)
    s3 = _dot_nt(qa_sc[:, 0:LANES], kvw_ref[pl.ds(d0, tq), 0:LANES]).reshape(nh, tq, tq)
    s3 = jnp.where(_iota((1, tq, tq), 2) <= _iota((1, tq, tq), 1), s3, NEG)
    online(s3, load_v(kvw_ref, d0, tq), True)

    @pl.when(i >= 1)
    def _():
        start = pl.multiple_of(jnp.maximum(t0 - WINDOW, 0), tq)
        s3 = _dot_nt(qa_sc[:, 0:LANES], kvw_ref[pl.ds(start, WINDOW), 0:LANES]).reshape(nh, tq, WINDOW)
        kpos = start + _iota((1, tq, WINDOW), 2)
        s3 = jnp.where((kpos < t0) & (trow - kpos < WINDOW), s3, NEG)
        online(s3, load_v(kvw_ref, start, WINDOW), False)

    finish(2)

    for pair in range(nh // 2):
        parts = []
        for hh in (2 * pair, 2 * pair + 1):
            x = out_sc[hh]
            if hh % 2 != hh // NSA_HPG:
                x = pltpu.roll(x, HEAD_DIM, 1)
            parts.append(x)
        o_ref[:, pair * LANES:(pair + 1) * LANES] = jnp.where(half == 0, parts[0], parts[1]).astype(o_ref.dtype)


def _nsa_prompt(q, gates, ckv, kvs_b, kvw_b, ov, et, tq, tk):
    b, l, _ = q.shape
    nb = l // SLC_BLOCK
    assert l % tk == 0 and tk % tq == 0 and WINDOW % tq == 0 and l >= WINDOW
    blk = lambda w: pl.BlockSpec((None, tq, w), lambda bi, i: (bi, i, 0))
    seq = lambda a: pl.BlockSpec((None,) + a.shape[1:], lambda bi, i: (bi, 0, 0))
    full = lambda a: pl.BlockSpec(a.shape, lambda bi, i: (0, 0))
    hs = (NSA_HEADS, tq, LANES)
    return pl.pallas_call(
        functools.partial(_nsa_prompt_body, tq=tq, tk=tk, n_cmp=l // CMP_STRIDE - 1),
        grid=(b, l // tq),
        in_specs=[blk(NQ), blk(LANES), seq(ckv), seq(kvs_b), seq(kvw_b), full(ov), full(et)],
        out_specs=blk(NQ),
        out_shape=jax.ShapeDtypeStruct((b, l, NQ), BF16),
        scratch_shapes=[pltpu.VMEM((NSA_HEADS * tq, LANES + nb), BF16), pltpu.VMEM(hs, F32), pltpu.VMEM(hs, F32),
                        pltpu.VMEM(hs, F32)],
        compiler_params=_params("parallel", "arbitrary"),
        name="nsa_prompt",
    )(q, gates, ckv, kvs_b, kvw_b, ov, et)


def _nsa_sample_body(*refs, n_pages, s_new, n_cmp, p_len):
    page_refs = refs[1:1 + n_pages]
    (q_ref, gate_ref, ckv_ref, win_ref, ns_ref, nw_ref, ov_ref, e_ref, o_ref) = refs[1 + n_pages:]
    rows = NSA_HEADS * s_new
    nc = ckv_ref.shape[0]
    nb = ov_ref.shape[1]
    w_buf = win_ref.shape[1]
    q2 = jnp.concatenate(_head_queries(q_ref[...], s_new), axis=0)
    gates = gate_ref[...]
    trow = _iota((rows, 1), 0) & (s_new - 1)
    grow = _div_pow2(_iota((rows, 1), 0), s_new * NSA_HPG)
    lane_half = _iota((rows, LANES), 1) >> 6

    def gate_rows(branch):
        cols = [gates[:, branch * NSA_HEADS + hh:branch * NSA_HEADS + hh + 1] for hh in range(NSA_HEADS)]
        return jnp.concatenate(cols, axis=0)

    def pick(acc, den):
        return jnp.where(lane_half == grow, acc, 0.0) / den

    def attend(s_old, s_nw, vt_old, new_rows):
        s_nw = jnp.where(_iota((rows, s_new), 1) <= trow, s_nw, NEG)
        m = jnp.maximum(jnp.max(s_old, axis=-1, keepdims=True), jnp.max(s_nw, axis=-1, keepdims=True))
        e1 = jnp.exp2(s_old - m)
        e2 = jnp.exp2(s_nw - m)
        den = jnp.sum(e1, axis=-1, keepdims=True) + jnp.sum(e2, axis=-1, keepdims=True)
        acc = _dot_nt(e1.astype(BF16), vt_old) + _dot(e2.astype(BF16), new_rows[:, LANES:2 * LANES])
        return pick(acc, den)

    ck = ckv_ref[:, 0:LANES]
    cv = ckv_ref[:, LANES:2 * LANES]
    ncol = _iota((rows, nc), 1)
    cvalid = (ncol * CMP_STRIDE + CMP_LEN <= p_len + trow + 1) & (ncol < n_cmp)
    p_c = _masked_softmax2(_dot_nt(q2, ck), cvalid)
    out = gate_rows(0) * jnp.where(lane_half == grow, _dot(p_c.astype(BF16), cv), 0.0)
    psum = []
    for g in range(NSA_GROUPS):
        acc = jnp.zeros((s_new, nc), F32)
        for h in range(NSA_HPG):
            r0 = (g * NSA_HPG + h) * s_new
            acc = acc + p_c[r0:r0 + s_new]
        psum.append(acc)
    imp = _dot01(jnp.concatenate(psum, axis=0), ov_ref[...])
    cur = jnp.full((NSA_GROUPS * s_new, 1), p_len // SLC_BLOCK, jnp.int32)
    sel = _topk_select(imp, cur, nb)
    selneg = jnp.where(sel > 0.0, 0.0, NEG).astype(BF16)
    selneg_rows = jnp.concatenate([selneg[g * s_new:(g + 1) * s_new] for g in range(NSA_GROUPS)
                                   for _ in range(NSA_HPG)], axis=0)

    kvt = jnp.concatenate([r[...] for r in page_refs], axis=1).astype(BF16)
    kaug = jnp.concatenate([kvt[0:LANES], e_ref[...]], axis=0)
    s_past = _dot(jnp.concatenate([q2, selneg_rows], axis=1), kaug)
    ns = ns_ref[...].astype(BF16)
    out = out + gate_rows(1) * attend(s_past, _dot_nt(q2, ns[:, 0:LANES]), kvt[LANES:2 * LANES], ns)

    wt = win_ref[...].astype(BF16)
    nw = nw_ref[...].astype(BF16)
    s_b = _dot(q2, wt[0:LANES])
    scol = _iota((rows, w_buf), 1)
    s_b = jnp.where((w_buf + trow - scol < WINDOW) & (p_len - w_buf + scol >= 0), s_b, NEG)
    out = out + gate_rows(2) * attend(s_b, _dot_nt(q2, nw[:, 0:LANES]), wt[LANES:2 * LANES], nw)

    for pair in range(NSA_HEADS // 2):
        parts = []
        for hh in (2 * pair, 2 * pair + 1):
            x = out[hh * s_new:(hh + 1) * s_new]
            if hh % 2 != hh // NSA_HPG:
                x = jnp.concatenate([x[:, HEAD_DIM:], x[:, :HEAD_DIM]], axis=1)
            parts.append(x)
        lh = _iota((s_new, LANES), 1) >> 6
        o_ref[:, pair * LANES:(pair + 1) * LANES] = jnp.where(lh == 0, parts[0], parts[1]).astype(o_ref.dtype)


def _nsa_sample(q, gates, ckv, cache_t, page_table, win_t, new_s, new_w, ov, e_all):
    bd, s_new, _ = q.shape
    n_pages = page_table.shape[1]
    p_len = n_pages * PAGE_SIZE
    assert s_new & (s_new - 1) == 0
    full = lambda a: pl.BlockSpec(a.shape, lambda i, pt: (0, 0))
    seq = lambda a: pl.BlockSpec((None,) + a.shape[1:], lambda i, pt: (i, 0, 0))
    page = lambda k: pl.BlockSpec((None, KV_W, PAGE_SIZE), lambda i, pt, k=k: (pt[i, k], 0, 0))
    return pl.pallas_call(
        functools.partial(_nsa_sample_body, n_pages=n_pages, s_new=s_new, n_cmp=p_len // CMP_STRIDE - 1, p_len=p_len),
        grid_spec=pltpu.PrefetchScalarGridSpec(
            num_scalar_prefetch=1,
            grid=(bd,),
            in_specs=[page(k) for k in range(n_pages)] + [seq(q), seq(gates), seq(ckv), seq(win_t), seq(new_s),
                                                          seq(new_w), full(ov), full(e_all)],
            out_specs=pl.BlockSpec((None, s_new, NQ), lambda i, pt: (i, 0, 0)),
        ),
        out_shape=jax.ShapeDtypeStruct((bd, s_new, NQ), BF16),
        compiler_params=_params("parallel"),
        name="nsa_sample",
    )(page_table, *([cache_t] * n_pages), q, gates, ckv, win_t, new_s, new_w, ov, e_all)


def _gla_body(q_ref, k_ref, v_ref, kt_ref, lg_ref, lgt_ref, r_ref, gn_ref, s0_ref, o_ref, sfin_ref, s_sc,
              *, tb, chunk, sub):
    j = pl.program_id(1)

    @pl.when(j == 0)
    def _():
        s_sc[...] = s0_ref[...]

    q = q_ref[...]
    v = v_ref[...]
    kt = kt_ref[...]
    lg = lg_ref[...]
    lgt = lgt_ref[...]
    nchunk = tb // chunk
    nsub = chunk // sub
    ri = _iota((tb, tb), 0)
    ci = _iota((tb, tb), 1)
    same = _div_pow2(ri, chunk) == _div_pow2(ci, chunk)
    b = _dot01_l(jnp.where(same & (ci <= ri), 1.0, 0.0).astype(BF16), lg)
    bt = _dot01(lgt, jnp.where(same & (ri <= ci), 1.0, 0.0).astype(BF16))
    head_of_lane = _div_pow2(_iota((1, GQK), 1), GLA_DK)
    tcol = _iota((GQK, tb), 1)
    gn = gn_ref[...]

    for c in range(nchunk):
        c0 = c * chunk
        bc = b[c0:c0 + chunk]
        qc = q[c0:c0 + chunk]
        bt_last = bt[:, c0 + chunk - 1:c0 + chunk]
        qdb = qc * jnp.exp(bc)
        in_chunk = (tcol >= c0) & (tcol < c0 + chunk)
        ktl = jnp.where(in_chunk, kt * jnp.exp(jnp.where(in_chunk, bt_last - bt, 0.0)), 0.0).astype(BF16)
        s_all = s_sc[...].astype(BF16)
        o_heads = []
        for h in range(GLA_HEADS):
            hm = head_of_lane == h
            s_h = s_sc[h * GLA_DK:(h + 1) * GLA_DK]
            vh = v[:, h * GLA_DV:(h + 1) * GLA_DV].astype(BF16)
            o_h = _dot(jnp.where(hm, qdb, 0.0).astype(BF16), s_all)
            o_sub = []
            for sb in range(nsub):
                r0 = c0 + sb * sub
                if sb == 0:
                    ref_row = jnp.zeros((1, GQK), F32)
                    ref_col = jnp.zeros((GQK, 1), F32)
                else:
                    ref_row = b[r0 - 1:r0]
                    ref_col = bt[:, r0 - 1:r0]
                qd = jnp.where(hm, q[r0:r0 + sub] * jnp.exp(b[r0:r0 + sub] - ref_row), 0.0).astype(BF16)
                vis = (tcol >= c0) & (tcol < r0 + sub)
                kd = jnp.where(vis, kt * jnp.exp(jnp.where(vis, ref_col - bt, 0.0)), 0.0).astype(BF16)
                a = _dot(qd, kd)
                a = jnp.where(_iota((sub, tb), 1) <= r0 + _iota((sub, tb), 0), a, 0.0)
                o_sub.append(_dot(a.astype(BF16), vh))
            o_intra = o_sub[0] if nsub == 1 else jnp.concatenate(o_sub, axis=0)
            o_heads.append(o_h + o_intra)
            upd = _dot(ktl[h * GLA_DK:(h + 1) * GLA_DK], vh)
            s_sc[h * GLA_DK:(h + 1) * GLA_DK] = jnp.exp(bt_last[h * GLA_DK:(h + 1) * GLA_DK]) * s_h + upd
        for h in range(GLA_HEADS):
            o_ref[c0:c0 + chunk, h * GLA_DV:(h + 1) * GLA_DV] = (
                _rms(o_heads[h], gn) * r_ref[c0:c0 + chunk, h * GLA_DV:(h + 1) * GLA_DV].astype(F32)
            ).astype(o_ref.dtype)

    @pl.when(j == pl.num_programs(1) - 1)
    def _():
        sfin_ref[...] = s_sc[...]


def _gla(q, k, v, kt, lg, lgt, r, gn, s0, tb, chunk):
    s, t, _ = q.shape
    sub = min(GLA_SUB, chunk)
    nblk = t // tb
    row = lambda w: pl.BlockSpec((None, tb, w), lambda si, j: (si, j, 0))
    if kt.ndim == 2:
        col = pl.BlockSpec((GQK, tb), lambda si, j: (0, si * nblk + j))
    else:
        col = pl.BlockSpec((None, GQK, tb), lambda si, j: (si, 0, j))
    return pl.pallas_call(
        functools.partial(_gla_body, tb=tb, chunk=chunk, sub=sub),
        grid=(s, nblk),
        in_specs=[row(GQK), row(GQK), row(GV), col, row(GQK), col, row(GV),
                  pl.BlockSpec(gn.shape, lambda si, j: (0, 0)),
                  pl.BlockSpec((None, GQK, GLA_DV), lambda si, j: (si, 0, 0))],
        out_specs=[row(GV), pl.BlockSpec((None, GQK, GLA_DV), lambda si, j: (si, 0, 0))],
        out_shape=[jax.ShapeDtypeStruct((s, t, GV), BF16), jax.ShapeDtypeStruct((s, GQK, GLA_DV), F32)],
        scratch_shapes=[pltpu.VMEM((GQK, GLA_DV), F32)],
        compiler_params=_params("parallel", "arbitrary"),
        name="gla",
    )(q, k, v, kt, lg, lgt, r, gn, s0)


def _mix_body(x_ref, sc_ref, sh_ref, ga_ref, gpre_ref, gpost_ref, on_ref, og_ref, wm_ref, wn_ref, wgo_ref, wo_ref,
              o_ref):
    x = x_ref[...]
    d = x.shape[1]
    h = _norm_mod(x, gpre_ref[...], sc_ref[...], sh_ref[...]).astype(BF16)
    m = _sigmoid(_dot(h, wm_ref[...]))
    y_a = _dot(on_ref[...], wn_ref[...])
    y_b = _dot(og_ref[...], wgo_ref[...])
    mixin = (m[:, 0:d] * y_a + m[:, d:2 * d] * y_b).astype(BF16)
    mix = _dot(mixin, wo_ref[...])
    o_ref[...] = x + ga_ref[...] * _rms(mix, gpost_ref[...])


def _mix(x, sc, sh, ga, gpre, gpost, o_nsa, o_gla, wm, wn, wgo, wo, tm, rows_per_mod):
    n, d = x.shape
    row = lambda w: pl.BlockSpec((tm, w), lambda i: (i, 0))
    full = lambda a: pl.BlockSpec(a.shape, lambda i: (0, 0))
    ms = lambda a: _mod_spec(a, tm, rows_per_mod)
    return pl.pallas_call(
        _mix_body,
        grid=(n // tm,),
        in_specs=[row(d), ms(sc), ms(sh), ms(ga), full(gpre), full(gpost), row(NQ), row(GV),
                  full(wm), full(wn), full(wgo), full(wo)],
        out_specs=row(d),
        out_shape=jax.ShapeDtypeStruct((n, d), F32),
        compiler_params=_params("parallel"),
        name="mix",
    )(x, sc, sh, ga, gpre, gpost, o_nsa, o_gla, wm, wn, wgo, wo)


def _ffn_body(x_ref, sc_ref, sh_ref, ga_ref, gpre_ref, gpost_ref, wg_ref, wu_ref, wd_ref, o_ref, h_sc, acc_sc):
    j = pl.program_id(1)

    @pl.when(j == 0)
    def _():
        h_sc[...] = _norm_mod(x_ref[...], gpre_ref[...], sc_ref[...], sh_ref[...]).astype(BF16)
        acc_sc[...] = jnp.zeros(acc_sc.shape, F32)

    h = h_sc[...]
    gt = _dot(h, wg_ref[...])
    up = _dot(h, wu_ref[...])
    a = (gt * _sigmoid(gt) * up).astype(BF16)
    acc_sc[...] += _dot(a, wd_ref[...])

    @pl.when(j == pl.num_programs(1) - 1)
    def _():
        o_ref[...] = x_ref[...] + ga_ref[...] * _rms(acc_sc[...], gpost_ref[...])


def _ffn(x, sc, sh, ga, gpre, gpost, wg, wu, wd, tm, rows_per_mod):
    n, d = x.shape
    dff = wg.shape[1]
    tf = dff // 2 if (dff // 2) % LANES == 0 else dff
    row = pl.BlockSpec((tm, d), lambda i, j: (i, 0))
    full = lambda a: pl.BlockSpec(a.shape, lambda i, j: (0, 0))
    ms = lambda a: pl.BlockSpec((None, a.shape[1], a.shape[2]), lambda i, j: ((i * tm) // rows_per_mod, 0, 0))
    return pl.pallas_call(
        _ffn_body,
        grid=(n // tm, dff // tf),
        in_specs=[row, ms(sc), ms(sh), ms(ga), full(gpre), full(gpost),
                  pl.BlockSpec((d, tf), lambda i, j: (0, j)), pl.BlockSpec((d, tf), lambda i, j: (0, j)),
                  pl.BlockSpec((tf, d), lambda i, j: (j, 0))],
        out_specs=row,
        out_shape=jax.ShapeDtypeStruct((n, d), F32),
        scratch_shapes=[pltpu.VMEM((tm, d), BF16), pltpu.VMEM((tm, d), F32)],
        compiler_params=_params("parallel", "arbitrary"),
        name="ffn",
    )(x, sc, sh, ga, gpre, gpost, wg, wu, wd)


def _split_w_in(w_in, d):
    sizes = (NQ, KV_W, KV_W, KV_W, 3 * NSA_HEADS, GQK, GQK, GV, 16, GV, 2 * d)
    out, o = [], 0
    for s in sizes:
        out.append(w_in[:, o:o + s])
        o += s
    return out


def _pad_cols(w, n):
    return jnp.pad(w, ((0, 0), (0, n - w.shape[1])))


def _compress_weights(w1k, b1k, w2k, pek, w1v, b1v, w2v, pev):
    hid = w2k.shape[0]

    def expand_w1(half):
        out = jnp.zeros((CMP_STRIDE, 2, NSA_GROUPS, HEAD_DIM, NSA_GROUPS, hid), F32)
        for kv, w1 in enumerate((w1k, w1v)):
            w = w1.reshape(CMP_LEN, HEAD_DIM, hid)[half * CMP_STRIDE:(half + 1) * CMP_STRIDE]
            for g in range(NSA_GROUPS):
                out = out.at[:, kv, g, :, g, :].set(w)
        return out.reshape(CMP_STRIDE, 2, NSA_GROUPS * HEAD_DIM, NSA_GROUPS * hid).astype(BF16)

    def expand_pe(half):
        rows = [jnp.broadcast_to(pe[half * CMP_STRIDE:(half + 1) * CMP_STRIDE, None, :],
                                 (CMP_STRIDE, NSA_GROUPS, HEAD_DIM)) for pe in (pek, pev)]
        return jnp.stack(rows, axis=1).reshape(CMP_STRIDE, KV_W)

    b1 = jnp.concatenate([jnp.tile(b1k, NSA_GROUPS), jnp.tile(b1v, NSA_GROUPS)]).reshape(1, -1)
    w2 = jnp.zeros((2, NSA_GROUPS, hid, 2, NSA_GROUPS, HEAD_DIM), F32)
    for g in range(NSA_GROUPS):
        w2 = w2.at[0, g, :, 0, g, :].set(w2k)
        w2 = w2.at[1, g, :, 1, g, :].set(w2v)
    w2 = w2.reshape(2 * NSA_GROUPS * hid, KV_W).astype(BF16)
    return expand_pe(0), expand_pe(1), expand_w1(0), expand_w1(1), b1, w2


def _overlap(n_cmp_pad, n_cmp, nb):
    n = np.arange(n_cmp_pad)[:, None]
    j = np.arange(nb)[None, :] * SLC_BLOCK
    ok = (n * CMP_STRIDE < j + SLC_BLOCK) & (n * CMP_STRIDE + CMP_LEN > j) & (n < n_cmp)
    return ok.astype(BF16)


def _block_onehot(nb, nkeys):
    return (np.arange(nb)[:, None] == (np.arange(nkeys)[None, :] // SLC_BLOCK)).astype(BF16)


def _tile_rows(n, cap):
    t = cap
    while n % t:
        t //= 2
    return t


def _feature_major(a):
    nd = a.ndim
    a = jnp.moveaxis(a, nd - 4, nd - 1)
    return a.reshape(a.shape[:nd - 4] + (KV_W, a.shape[-1]))


def kernel(x_prompt, x_sample, cache_cmp_kv, cache_slc_kv, state_win_kv, state_gla, page_table, c_prompt, c_sample, w_ada, b_ada, g_pre_mix, g_post_mix, w_in, cmp_w1_k, cmp_b1_k, cmp_w2_k, cmp_pe_k, cmp_w1_v, cmp_b1_v, cmp_w2_v, cmp_pe_v, w_gla_a2, b_gla_a, gla_norm_g, w_nsa_o, w_gla_o, w_out, g_pre_ffn, g_post_ffn, w_ff_gate, w_ff_up, w_ff_down):
    depth = w_ada.shape[0]
    bp, l, d = x_prompt.shape
    bd, s_new, _ = x_sample.shape
    n_pages = page_table.shape[1]
    p_len = n_pages * PAGE_SIZE
    w_buf = state_win_kv.shape[2]
    keep = min(WINDOW, l)
    np_, ns_ = bp * l, bd * s_new
    tmp = _tile_rows(np_ // bp, 512)
    tms = _tile_rows(ns_, 512)
    tq = _tile_rows(l, 256)
    tk = 2 * tq

    yp, ys = x_prompt.reshape(np_, d), x_sample.reshape(ns_, d)
    col = [[] for _ in range(8)]
    for li in range(depth):
        r_all = bp + bd
        r_pad = -(-r_all // 8) * 8
        c_all = jnp.pad(jnp.concatenate([c_prompt, c_sample], axis=0), ((0, r_pad - r_all), (0, 0)))
        mod = _ada(c_all, w_ada[li], b_ada[li])
        mods_p = [m.reshape(bp, 1, d) for m in jnp.split(mod[:bp], 6, axis=-1)]
        mods_s = [jnp.repeat(m, s_new, axis=0).reshape(ns_ // tms, tms, d) for m in jnp.split(mod[bp:r_all], 6, axis=-1)]

        (w_q, w_kc, w_ks, w_kw, w_gn, w_qg, w_kg, w_vg, w_ag, w_rg, w_mg) = _split_w_in(w_in[li], d)
        wq = w_q.astype(BF16)
        wkv = jnp.concatenate([w_kc, w_ks, w_kw], axis=1).astype(BF16)
        wgn = _pad_cols(w_gn, LANES).astype(BF16)
        wqkv = jnp.concatenate([w_qg, w_kg, w_vg], axis=1).astype(BF16)
        wkt = w_kg.T.astype(BF16)
        wa = _pad_cols(w_ag, LANES).astype(BF16)
        wat = wa.T
        wr = w_rg.astype(BF16)
        a2 = jnp.pad(w_gla_a2[li], ((0, LANES - w_gla_a2.shape[1]), (0, 0)))
        a2t = a2.T
        ba = b_gla_a[li].reshape(1, GQK)
        bat = b_gla_a[li].reshape(GQK, 1)
        cw = _compress_weights(cmp_w1_k[li], cmp_b1_k[li], cmp_w2_k[li], cmp_pe_k[li],
                               cmp_w1_v[li], cmp_b1_v[li], cmp_w2_v[li], cmp_pe_v[li])
        gpre = g_pre_mix[li].reshape(1, d)
        gpost = g_post_mix[li].reshape(1, d)
        gpre2 = g_pre_ffn[li].reshape(1, d)
        gpost2 = g_post_ffn[li].reshape(1, d)
        gn = gla_norm_g[li].reshape(1, GLA_DV)
        wm = w_mg.astype(BF16)
        wn = w_nsa_o[li].astype(BF16)
        wgo = w_gla_o[li].astype(BF16)
        wo = w_out[li].astype(BF16)
        wfg = w_ff_gate[li].astype(BF16)
        wfu = w_ff_up[li].astype(BF16)
        wfd = w_ff_down[li].astype(BF16)

        sh1, sc1, ga1, sh2, sc2, ga2 = mods_p
        q, kvc, kvs, kvw, kvs_b, kvw_b, gates = _pre_nsa(yp, sc1, sh1, gpre, wq, wkv, wgn, tmp, l)
        qg, kg, vg, kgt, lg, lgt, rg = _pre_gla(yp, sc1, sh1, gpre, wqkv, wkt, wa, wat, wr, a2, a2t, ba, bat, tmp, l)
        ckv = _cmp_prompt(kvc.reshape(bp, l, KV_W), cw)
        nch = l // CMP_STRIDE
        nb = l // SLC_BLOCK
        o_nsa = _nsa_prompt(q.reshape(bp, l, NQ), gates.reshape(bp, l, LANES), ckv, kvs_b.reshape(bp, l, KV_W),
                            kvw_b.reshape(bp, l, KV_W), _overlap(nch, nch - 1, nb), _block_onehot(nb, l).T, tq, tk)
        chunk = GLA_CHUNK if l % GLA_CHUNK == 0 else l
        tb = LANES if (l % LANES == 0 and LANES % chunk == 0) else chunk
        if tb % LANES:
            kgt = kgt.reshape(GQK, bp, l).transpose(1, 0, 2)
            lgt = lgt.reshape(GQK, bp, l).transpose(1, 0, 2)
        o_gla, s_fin_p = _gla(qg.reshape(bp, l, GQK), kg.reshape(bp, l, GQK), vg.reshape(bp, l, GV), kgt,
                              lg.reshape(bp, l, GQK), lgt, rg.reshape(bp, l, GV), gn,
                              jnp.zeros((bp, GQK, GLA_DV), F32), tb, chunk)
        x1 = _mix(yp, sc1, sh1, ga1, gpre, gpost, o_nsa.reshape(np_, NQ), o_gla.reshape(np_, GV), wm, wn, wgo, wo, tmp, l)
        yp = _ffn(x1, sc2, sh2, ga2, gpre2, gpost2, wfg, wfu, wfd, tmp, l)
        col[0].append(kvc.reshape(bp, l, 2, NSA_GROUPS, HEAD_DIM))
        col[2].append(kvs.reshape(bp, l, 2, NSA_GROUPS, HEAD_DIM))
        col[4].append(kvw.reshape(bp, l, KV_W)[:, l - keep:].reshape(bp, keep, 2, NSA_GROUPS, HEAD_DIM))
        col[6].append(s_fin_p.reshape(bp, GLA_HEADS, GLA_DK, GLA_DV).astype(state_gla.dtype))

        sh1, sc1, ga1, sh2, sc2, ga2 = mods_s
        q, kvc, kvs, kvw, _, _, gates = _pre_nsa(ys, sc1, sh1, gpre, wq, wkv, wgn, tms, tms)
        qg, kg, vg, kgt, lg, lgt, rg = _pre_gla(ys, sc1, sh1, gpre, wqkv, wkt, wa, wat, wr, a2, a2t, ba, bat, tms, tms)
        ckv = _cmp_sample(_feature_major(cache_cmp_kv[li]), page_table, cw)
        nch = p_len // CMP_STRIDE
        nb = p_len // SLC_BLOCK
        win_t = _feature_major(state_win_kv[li])
        kvw3 = kvw.reshape(bd, s_new, KV_W)
        o_nsa = _nsa_sample(q.reshape(bd, s_new, NQ), gates.reshape(bd, s_new, LANES), ckv,
                            _feature_major(cache_slc_kv[li]), page_table, win_t, kvs.reshape(bd, s_new, KV_W), kvw3,
                            _overlap(nch, nch - 1, nb), _block_onehot(nb, p_len))
        chunk = GLA_CHUNK if s_new % GLA_CHUNK == 0 else s_new
        kgt3 = kgt.reshape(GQK, bd, s_new).transpose(1, 0, 2)
        lgt3 = lgt.reshape(GQK, bd, s_new).transpose(1, 0, 2)
        o_gla, s_fin_s = _gla(qg.reshape(bd, s_new, GQK), kg.reshape(bd, s_new, GQK), vg.reshape(bd, s_new, GV), kgt3,
                              lg.reshape(bd, s_new, GQK), lgt3, rg.reshape(bd, s_new, GV), gn,
                              state_gla[li].astype(F32).reshape(bd, GQK, GLA_DV), chunk, chunk)
        x1 = _mix(ys, sc1, sh1, ga1, gpre, gpost, o_nsa.reshape(ns_, NQ), o_gla.reshape(ns_, GV), wm, wn, wgo, wo, tms, tms)
        ys = _ffn(x1, sc2, sh2, ga2, gpre2, gpost2, wfg, wfu, wfd, tms, tms)
        win_new_t = jnp.concatenate([win_t, kvw3.transpose(0, 2, 1)], axis=2)[:, :, s_new:]
        win_new = jnp.moveaxis(win_new_t.reshape(bd, 2, NSA_GROUPS, HEAD_DIM, w_buf), 4, 1)
        col[1].append(kvc.reshape(bd, s_new, 2, NSA_GROUPS, HEAD_DIM))
        col[3].append(kvs.reshape(bd, s_new, 2, NSA_GROUPS, HEAD_DIM))
        col[5].append(win_new)
        col[7].append(s_fin_s.reshape(bd, GLA_HEADS, GLA_DK, GLA_DV).astype(state_gla.dtype))

    stacked = [jnp.stack(c) for c in col]
    return (yp.reshape(bp, l, d), ys.reshape(bd, s_new, d), *stacked)
```

```python
import functools
import math

import numpy as np
import jax
import jax.numpy as jnp
from jax import lax
from jax.experimental import pallas as pl
from jax.experimental.pallas import tpu as pltpu

F32 = jnp.float32
BF16 = jnp.bfloat16

NSA_HEADS = 8
NSA_GROUPS = 2
NSA_HPG = NSA_HEADS // NSA_GROUPS
HEAD_DIM = 64
CMP_STRIDE = 16
CMP_LEN = 32
SLC_BLOCK = 64
N_SEL = 16
WINDOW = 512
GLA_HEADS = 4
GLA_DK = 64
GLA_DV = 128
GLA_TAU = 16.0
GLA_CHUNK = 64
GLA_SUB = 16
PE_ROWS = 16
EPS = 1e-6
PAGE_SIZE = 128
NEG = -1e30
LOG2E = math.log2(math.e)
LANES = 128
VMEM_LIMIT = 56 * 1024 * 1024

KV_W = 2 * NSA_GROUPS * HEAD_DIM
NQ = NSA_HEADS * HEAD_DIM
GQK = GLA_HEADS * GLA_DK
GV = GLA_HEADS * GLA_DV


def _dot(a, b):
    return jnp.dot(a, b, preferred_element_type=F32)


def _dot_nt(a, b):
    return lax.dot_general(a, b, (((1,), (1,)), ((), ())), preferred_element_type=F32)


def _dot_f32(a, b):
    return jnp.dot(a, b, preferred_element_type=F32, precision=lax.Precision.HIGHEST)


def _dot01(x, m01):
    hi = x.astype(BF16)
    r1 = x - hi.astype(F32)
    mid = r1.astype(BF16)
    lo = (r1 - mid.astype(F32)).astype(BF16)
    return _dot(hi, m01) + _dot(mid, m01) + _dot(lo, m01)


def _dot01_l(m01, x):
    hi = x.astype(BF16)
    r1 = x - hi.astype(F32)
    mid = r1.astype(BF16)
    lo = (r1 - mid.astype(F32)).astype(BF16)
    return _dot(m01, hi) + _dot(m01, mid) + _dot(m01, lo)


def _dot_tn(a, b):
    return lax.dot_general(a, b, (((0,), (0,)), ((), ())), preferred_element_type=F32)


def _dot01_tn(x, m01):
    hi = x.astype(BF16)
    r1 = x - hi.astype(F32)
    mid = r1.astype(BF16)
    lo = (r1 - mid.astype(F32)).astype(BF16)
    return _dot_tn(hi, m01) + _dot_tn(mid, m01) + _dot_tn(lo, m01)


def _sigmoid(x):
    return 1.0 / (1.0 + jnp.exp(-x))


def _params(*sem):
    return pltpu.CompilerParams(dimension_semantics=sem, vmem_limit_bytes=VMEM_LIMIT)


def _iota(shape, dim):
    return lax.broadcasted_iota(jnp.int32, shape, dim)


def _div_pow2(x, n):
    assert n & (n - 1) == 0, n
    return x >> (n.bit_length() - 1)


def _ada_body(c_ref, w_ref, b_ref, o_ref):
    o_ref[...] = _dot_f32(c_ref[...], w_ref[...]) + b_ref[...]


def _ada(c_all, w_ada, b_ada):
    r, d = c_all.shape
    n = w_ada.shape[1]
    tn = 1536 if n % 1536 == 0 else n
    return pl.pallas_call(
        _ada_body,
        grid=(n // tn,),
        in_specs=[pl.BlockSpec((r, d), lambda j: (0, 0)),
                  pl.BlockSpec((d, tn), lambda j: (0, j)),
                  pl.BlockSpec((1, tn), lambda j: (0, j))],
        out_specs=pl.BlockSpec((r, tn), lambda j: (0, j)),
        out_shape=jax.ShapeDtypeStruct((r, n), F32),
        compiler_params=_params("arbitrary"),
        name="ada",
    )(c_all, w_ada, b_ada.reshape(1, n))


def _norm_mod(x, g, sc, sh):
    ms = jnp.mean(x * x, axis=-1, keepdims=True)
    y = x * lax.rsqrt(ms + EPS) * g
    return y * (1.0 + sc) + sh


def _rms(x, g):
    ms = jnp.mean(x * x, axis=-1, keepdims=True)
    return x * lax.rsqrt(ms + EPS) * g


def _pre_nsa_body(x_ref, sc_ref, sh_ref, g_ref, wq_ref, wkv_ref, wg_ref,
                  q_ref, kvc_ref, kvs_ref, kvw_ref, kvsb_ref, kvwb_ref, gate_ref):
    h = _norm_mod(x_ref[...], g_ref[...], sc_ref[...], sh_ref[...]).astype(BF16)
    q_ref[...] = (_dot(h, wq_ref[...]) * (HEAD_DIM ** -0.5 * LOG2E)).astype(BF16)
    kv = _dot(h, wkv_ref[...])
    kvc = kv[:, 0:KV_W]
    kvs = kv[:, KV_W:2 * KV_W]
    kvw = kv[:, 2 * KV_W:3 * KV_W]
    kvc_ref[...] = kvc
    kvs_ref[...] = kvs
    kvw_ref[...] = kvw
    kvsb_ref[...] = kvs.astype(BF16)
    kvwb_ref[...] = kvw.astype(BF16)
    gate_ref[...] = _sigmoid(_dot(h, wg_ref[...]))


def _mod_spec(mod3, tm, rows_per_mod):
    r = mod3.shape[1]
    return pl.BlockSpec((None, r, mod3.shape[2]), lambda i: ((i * tm) // rows_per_mod, 0, 0))


def _pre_nsa(x, sc, sh, g, wq, wkv, wg, tm, rows_per_mod):
    n, d = x.shape
    row = lambda w: pl.BlockSpec((tm, w), lambda i: (i, 0))
    full = lambda a: pl.BlockSpec(a.shape, lambda i: (0, 0))
    outs = [(NQ, BF16), (KV_W, F32), (KV_W, F32), (KV_W, F32), (KV_W, BF16), (KV_W, BF16), (LANES, F32)]
    return pl.pallas_call(
        _pre_nsa_body,
        grid=(n // tm,),
        in_specs=[row(d), _mod_spec(sc, tm, rows_per_mod), _mod_spec(sh, tm, rows_per_mod), full(g),
                  full(wq), full(wkv), full(wg)],
        out_specs=[row(w) for w, _ in outs],
        out_shape=[jax.ShapeDtypeStruct((n, w), t) for w, t in outs],
        compiler_params=_params("parallel"),
        name="pre_nsa",
    )(x, sc, sh, g, wq, wkv, wg)


def _log_sigmoid(x):
    return jnp.minimum(x, 0.0) - jnp.log(1.0 + jnp.exp(-jnp.abs(x)))


def _pre_gla_body(x_ref, sc_ref, sh_ref, g_ref, wqkv_ref, wa_ref, wr_ref, a2_ref, ba_ref,
                  q_ref, k_ref, v_ref, lg_ref, r_ref):
    h = _norm_mod(x_ref[...], g_ref[...], sc_ref[...], sh_ref[...]).astype(BF16)
    qkv = _dot(h, wqkv_ref[...])
    q_ref[...] = qkv[:, 0:GQK] * GLA_DK ** -0.5
    k_ref[...] = qkv[:, GQK:2 * GQK]
    v_ref[...] = qkv[:, 2 * GQK:2 * GQK + GV]
    a = _dot(h, wa_ref[...])
    lg_ref[...] = _log_sigmoid(_dot_f32(a, a2_ref[...]) + ba_ref[...]) * (1.0 / GLA_TAU)
    r = _dot(h, wr_ref[...])
    r_ref[...] = (r * _sigmoid(r)).astype(BF16)


def _pre_gla(x, sc, sh, g, wqkv, wa, wr, a2, ba, tm, rows_per_mod):
    n, d = x.shape
    row = lambda w: pl.BlockSpec((tm, w), lambda i: (i, 0))
    full = lambda a: pl.BlockSpec(a.shape, lambda i: (0, 0))
    return pl.pallas_call(
        _pre_gla_body,
        grid=(n // tm,),
        in_specs=[row(d), _mod_spec(sc, tm, rows_per_mod), _mod_spec(sh, tm, rows_per_mod), full(g),
                  full(wqkv), full(wa), full(wr), full(a2), full(ba)],
        out_specs=[row(GQK), row(GQK), row(GV), row(GQK), row(GV)],
        out_shape=[jax.ShapeDtypeStruct((n, GQK), F32), jax.ShapeDtypeStruct((n, GQK), F32),
                   jax.ShapeDtypeStruct((n, GV), F32), jax.ShapeDtypeStruct((n, GQK), F32),
                   jax.ShapeDtypeStruct((n, GV), BF16)],
        compiler_params=_params("parallel"),
        name="pre_gla",
    )(x, sc, sh, g, wqkv, wa, wr, a2, ba)


def _gelu_tanh(x):
    return 0.5 * x * (1.0 + jnp.tanh(0.7978845608028654 * (x + 0.044715 * x * x * x)))


def _compress(get_xp, nch, pe_ref, w1_ref, b1_ref, w2_ref):
    first, second = [], []
    for kv in range(2):
        acc = jnp.zeros((nch + PE_ROWS, 2 * LANES), F32)
        for pp in range(CMP_STRIDE // 2):
            lhs = jnp.concatenate([get_xp(kv, 2 * pp), get_xp(kv, 2 * pp + 1)], axis=1).astype(BF16)
            acc = acc + _dot(jnp.concatenate([lhs, pe_ref[kv, pp]], axis=0), w1_ref[kv, pp])
        first.append(acc[0:nch, 0:LANES] + acc[nch:nch + 1, 0:LANES])
        second.append(acc[0:nch, LANES:2 * LANES] + acc[nch + 1:nch + 2, LANES:2 * LANES])
    first = jnp.concatenate(first, axis=1)
    second = jnp.concatenate(second, axis=1)
    if nch % 8 == 0:
        nxt = pltpu.roll(second, nch - 1, 0)
    else:
        nxt = jnp.concatenate([second[1:], second[:1]], axis=0)
    hid = _gelu_tanh(first + nxt + b1_ref[...])
    out = _dot(hid.astype(BF16), w2_ref[...])
    return jnp.where(_iota(out.shape, 0) < nch - 1, out, 0.0)


def _cmp_prompt_body(xk_ref, xv_ref, pe_ref, w1_ref, b1_ref, w2_ref, o_ref, *, nch):
    x_refs = (xk_ref, xv_ref)

    def get_xp(kv, p):
        return x_refs[kv][pl.ds(p, nch, stride=CMP_STRIDE), :]

    o_ref[...] = _compress(get_xp, nch, pe_ref, w1_ref, b1_ref, w2_ref).astype(o_ref.dtype)


def _cmp_prompt(kvc, cw):
    b, l, _ = kvc.shape
    nch = l // CMP_STRIDE
    full = lambda a: pl.BlockSpec(a.shape, lambda i: (0,) * a.ndim)
    return pl.pallas_call(
        functools.partial(_cmp_prompt_body, nch=nch),
        grid=(b,),
        in_specs=[pl.BlockSpec((None, l, LANES), lambda i: (i, 0, 0)),
                  pl.BlockSpec((None, l, LANES), lambda i: (i, 0, 1))] + [full(a) for a in cw],
        out_specs=pl.BlockSpec((None, nch, KV_W), lambda i: (i, 0, 0)),
        out_shape=jax.ShapeDtypeStruct((b, nch, KV_W), BF16),
        compiler_params=_params("parallel"),
        name="cmp_prompt",
    )(kvc, kvc, *cw)


def _cmp_sample_body(*refs, n_pages, nch):
    page_refs = refs[1:1 + n_pages]
    pe_ref, w1_ref, b1_ref, w2_ref, o_ref, xk_sc, xv_sc = refs[1 + n_pages:]
    cpp = PAGE_SIZE // CMP_STRIDE
    x_scs = (xk_sc, xv_sc)
    for k in range(n_pages):
        for kv in range(2):
            xt = page_refs[k][kv * LANES:(kv + 1) * LANES, :].T
            for n in range(cpp):
                for a in range(CMP_STRIDE // 8):
                    r0 = n * CMP_STRIDE + 8 * a
                    x_scs[kv][k, pl.ds(8 * a * cpp + n, 8, stride=cpp), :] = xt[r0:r0 + 8, :]

    def get_xp(kv, p):
        return x_scs[kv][:, p * cpp:(p + 1) * cpp, :].reshape(nch, LANES)

    o_ref[...] = _compress(get_xp, nch, pe_ref, w1_ref, b1_ref, w2_ref).astype(o_ref.dtype)


def _cmp_sample(cache_t, page_table, cw):
    bd, n_pages = page_table.shape
    p_len = n_pages * PAGE_SIZE
    nch = p_len // CMP_STRIDE
    full = lambda a: pl.BlockSpec(a.shape, lambda i, pt: (0,) * a.ndim)
    page = lambda k: pl.BlockSpec((None, KV_W, PAGE_SIZE), lambda i, pt, k=k: (pt[i, k], 0, 0))
    return pl.pallas_call(
        functools.partial(_cmp_sample_body, n_pages=n_pages, nch=nch),
        grid_spec=pltpu.PrefetchScalarGridSpec(
            num_scalar_prefetch=1,
            grid=(bd,),
            in_specs=[page(k) for k in range(n_pages)] + [full(a) for a in cw],
            out_specs=pl.BlockSpec((None, nch, KV_W), lambda i, pt: (i, 0, 0)),
            scratch_shapes=[pltpu.VMEM((n_pages, PAGE_SIZE, LANES), F32)] * 2,
        ),
        out_shape=jax.ShapeDtypeStruct((bd, nch, KV_W), BF16),
        compiler_params=_params("parallel"),
        name="cmp_sample",
    )(page_table, *([cache_t] * n_pages), *cw)


def _head_queries(q, rows):
    half = _iota((rows, LANES), 1) >> 6
    out = []
    for hh in range(NSA_HEADS):
        g = hh // NSA_HPG
        blk = q[:, (hh // 2) * LANES:(hh // 2 + 1) * LANES].astype(F32)
        if hh % 2 != g:
            blk = pltpu.roll(blk, HEAD_DIM, 1) if rows % 8 == 0 else jnp.roll(blk, HEAD_DIM, 1)
        out.append(jnp.where(half == g, blk, 0.0).astype(BF16))
    return out


def _topk_select(imp, cur):
    impt = imp.T
    nb = impt.shape[0]
    jl = _iota(impt.shape, 0)
    jf = jl.astype(F32)
    forced = (jl == 0) | (jl == cur) | (jl == cur - 1)
    elig = (jl >= 1) & (jl <= cur - 2)
    sel0 = jnp.where(forced, 1.0, 0.0)
    alive0 = jnp.where(elig, 1.0, 0.0)

    def step(_, carry):
        sel, alive = carry
        live = alive > 0.0
        vals = jnp.where(live, impt, -1.0)
        m = jnp.max(vals, axis=0, keepdims=True)
        cand = live & (vals == m)
        idx = jnp.min(jnp.where(cand, jf, float(nb)), axis=0, keepdims=True)
        one = jf == idx
        return jnp.where(one, 1.0, sel), jnp.where(one, 0.0, alive)

    sel, _ = lax.fori_loop(0, N_SEL - 3, step, (sel0, alive0))
    return sel.T


def _masked_softmax2(s, valid):
    s = jnp.where(valid, s, NEG)
    m = jnp.max(s, axis=-1, keepdims=True)
    e = jnp.where(valid, jnp.exp2(s - m), 0.0)
    d = jnp.sum(e, axis=-1, keepdims=True)
    return e / jnp.where(d > 0.0, d, 1.0)


def _nsa_prompt_body(q_ref, gate_ref, ckv_ref, kvs_ref, kvw_ref, ov_ref, et_ref, o_ref, qa_sc, m_sc, acc_sc, out_sc,
                     *, tq, tk, n_cmp):
    i = pl.program_id(1)
    t0 = i * tq
    nc = ckv_ref.shape[0]
    nb = ov_ref.shape[1]
    nh = NSA_HEADS
    qh = _head_queries(q_ref[...], tq)
    for hh in range(nh):
        qa_sc[hh * tq:(hh + 1) * tq, 0:LANES] = qh[hh]
    half = _iota((tq, LANES), 1) >> 6
    trow = t0 + _iota((1, tq, 1), 1)
    gates = gate_ref[...]

    def gate(branch, hh):
        c = branch * nh + hh
        return gates[:, c:c + 1]

    ck = ckv_ref[:, 0:LANES]
    cv = ckv_ref[:, LANES:2 * LANES]
    s = _dot_nt(qa_sc[:, 0:LANES], ck).reshape(nh, tq, nc)
    ncol = _iota((1, tq, nc), 2)
    cvalid = (ncol * CMP_STRIDE + CMP_LEN <= trow + 1) & (ncol < n_cmp)
    p = _masked_softmax2(s, cvalid)
    for hh in range(nh):
        out_sc[hh] = gate(0, hh) * _dot(p[hh].astype(BF16), cv)
    ov = ov_ref[...]
    imp = []
    for g in range(NSA_GROUPS):
        psum = p[g * NSA_HPG]
        for h in range(1, NSA_HPG):
            psum = psum + p[g * NSA_HPG + h]
        imp.append(_dot01(psum, ov))
    imp = jnp.concatenate(imp, axis=0)
    cur = (t0 + (_iota((1, NSA_GROUPS * tq), 1) & (tq - 1))) >> 6
    sel = _topk_select(imp, cur)
    selneg = jnp.where(sel > 0.0, 0.0, NEG).astype(BF16)
    for hh in range(nh):
        g = hh // NSA_HPG
        qa_sc[hh * tq:(hh + 1) * tq, LANES:LANES + nb] = selneg[g * tq:(g + 1) * tq]

    def load_v(ref, start, width):
        v128 = ref[pl.ds(start, width), LANES:2 * LANES]
        lane_half = _iota(v128.shape, 1) >> 6
        return [jnp.where(lane_half == g, v128, jnp.ones_like(v128)) for g in range(NSA_GROUPS)]

    def online(s3, vaug, first):
        width = s3.shape[-1]
        tmax = jnp.max(s3, axis=-1, keepdims=True)
        if first:
            m_new = jnp.broadcast_to(tmax, (nh, tq, LANES))
        else:
            m_old = m_sc[...]
            m_new = jnp.maximum(m_old, tmax)
            alpha = jnp.exp2(m_old - m_new)
        pexp = jnp.concatenate([jnp.exp2((s3[:, :, c * LANES:(c + 1) * LANES] - m_new).astype(BF16))
                                for c in range(width // LANES)], axis=-1)
        for hh in range(nh):
            pv = _dot(pexp[hh], vaug[hh // NSA_HPG])
            acc_sc[hh] = pv if first else alpha[hh] * acc_sc[hh] + pv
        m_sc[...] = m_new

    def finish(branch):
        for hh in range(nh):
            acc = acc_sc[hh]
            den = pltpu.roll(acc, HEAD_DIM, 1)
            out_sc[hh] = out_sc[hh] + gate(branch, hh) * (acc / den)

    def slc_scores(start):
        kaug = jnp.concatenate([kvs_ref[pl.ds(start, tk), 0:LANES], et_ref[pl.ds(start, tk), :]], axis=1)
        return _dot_nt(qa_sc[...], kaug).reshape(nh, tq, tk)

    n_full = i // (tk // tq)
    last = pl.multiple_of(n_full * tk, tk)
    s3 = slc_scores(last)
    s3 = jnp.where(last + _iota((1, tq, tk), 2) <= trow, s3, NEG)
    online(s3, load_v(kvs_ref, last, tk), True)

    def slc_step(j, carry):
        start = pl.multiple_of(j * tk, tk)
        online(slc_scores(start), load_v(kvs_ref, start, tk), False)
        return carry

    lax.fori_loop(0, n_full, slc_step, 0)
    finish(1)

    d0 = pl.multiple_of(t0, tq)
    s3 = _dot_nt(qa_sc[:, 0:LANES], kvw_ref[pl.ds(d0, tq), 0:LANES]).reshape(nh, tq, tq)
    s3 = jnp.where(_iota((1, tq, tq), 2) <= _iota((1, tq, tq), 1), s3, NEG)
    online(s3, load_v(kvw_ref, d0, tq), True)

    @pl.when(i >= 1)
    def _():
        start = pl.multiple_of(jnp.maximum(t0 - WINDOW, 0), tq)
        s3 = _dot_nt(qa_sc[:, 0:LANES], kvw_ref[pl.ds(start, WINDOW), 0:LANES]).reshape(nh, tq, WINDOW)
        kpos = start + _iota((1, tq, WINDOW), 2)
        s3 = jnp.where((kpos < t0) & (trow - kpos < WINDOW), s3, NEG)
        online(s3, load_v(kvw_ref, start, WINDOW), False)

    finish(2)

    for pair in range(nh // 2):
        parts = []
        for hh in (2 * pair, 2 * pair + 1):
            x = out_sc[hh]
            if hh % 2 != hh // NSA_HPG:
                x = pltpu.roll(x, HEAD_DIM, 1)
            parts.append(x)
        o_ref[:, pair * LANES:(pair + 1) * LANES] = jnp.where(half == 0, parts[0], parts[1]).astype(o_ref.dtype)


def _nsa_prompt(q, gates, ckv, kvs_b, kvw_b, ov, et, tq, tk):
    b, l, _ = q.shape
    nb = l // SLC_BLOCK
    assert l % tk == 0 and tk % tq == 0 and WINDOW % tq == 0 and l >= WINDOW
    blk = lambda w: pl.BlockSpec((None, tq, w), lambda bi, i: (bi, i, 0))
    seq = lambda a: pl.BlockSpec((None,) + a.shape[1:], lambda bi, i: (bi, 0, 0))
    full = lambda a: pl.BlockSpec(a.shape, lambda bi, i: (0, 0))
    hs = (NSA_HEADS, tq, LANES)
    return pl.pallas_call(
        functools.partial(_nsa_prompt_body, tq=tq, tk=tk, n_cmp=l // CMP_STRIDE - 1),
        grid=(b, l // tq),
        in_specs=[blk(NQ), blk(LANES), seq(ckv), seq(kvs_b), seq(kvw_b), full(ov), full(et)],
        out_specs=blk(NQ),
        out_shape=jax.ShapeDtypeStruct((b, l, NQ), BF16),
        scratch_shapes=[pltpu.VMEM((NSA_HEADS * tq, LANES + nb), BF16), pltpu.VMEM(hs, F32), pltpu.VMEM(hs, F32),
                        pltpu.VMEM(hs, F32)],
        compiler_params=_params("parallel", "arbitrary"),
        name="nsa_prompt",
    )(q, gates, ckv, kvs_b, kvw_b, ov, et)


def _nsa_sample_body(*refs, n_pages, s_new, n_cmp, p_len):
    page_refs = refs[1:1 + n_pages]
    (q_ref, gate_ref, ckv_ref, win_ref, ns_ref, nw_ref, ov_ref, e_ref, o_ref) = refs[1 + n_pages:]
    rows = NSA_HEADS * s_new
    nc = ckv_ref.shape[0]
    nb = ov_ref.shape[1]
    w_buf = win_ref.shape[1]
    q2 = jnp.concatenate(_head_queries(q_ref[...], s_new), axis=0)
    gates = gate_ref[...]
    trow = _iota((rows, 1), 0) & (s_new - 1)
    grow = _div_pow2(_iota((rows, 1), 0), s_new * NSA_HPG)
    lane_half = _iota((rows, LANES), 1) >> 6

    def gate_rows(branch):
        cols = [gates[:, branch * NSA_HEADS + hh:branch * NSA_HEADS + hh + 1] for hh in range(NSA_HEADS)]
        return jnp.concatenate(cols, axis=0)

    def pick(acc, den):
        return jnp.where(lane_half == grow, acc, 0.0) / den

    def attend(s_old, s_nw, vt_old, new_rows):
        s_nw = jnp.where(_iota((rows, s_new), 1) <= trow, s_nw, NEG)
        m = jnp.maximum(jnp.max(s_old, axis=-1, keepdims=True), jnp.max(s_nw, axis=-1, keepdims=True))
        e1 = jnp.exp2(s_old - m)
        e2 = jnp.exp2(s_nw - m)
        den = jnp.sum(e1, axis=-1, keepdims=True) + jnp.sum(e2, axis=-1, keepdims=True)
        acc = _dot_nt(vt_old, e1.astype(BF16)).T + _dot(e2.astype(BF16), new_rows[:, LANES:2 * LANES])
        return pick(acc, den)

    ck = ckv_ref[:, 0:LANES]
    cv = ckv_ref[:, LANES:2 * LANES]
    ncol = _iota((rows, nc), 1)
    cvalid = (ncol * CMP_STRIDE + CMP_LEN <= p_len + trow + 1) & (ncol < n_cmp)
    p_c = _masked_softmax2(_dot_nt(q2, ck), cvalid)
    out = gate_rows(0) * jnp.where(lane_half == grow, _dot(p_c.astype(BF16), cv), 0.0)
    psum = []
    for g in range(NSA_GROUPS):
        acc = jnp.zeros((s_new, nc), F32)
        for h in range(NSA_HPG):
            r0 = (g * NSA_HPG + h) * s_new
            acc = acc + p_c[r0:r0 + s_new]
        psum.append(acc)
    n_imp = NSA_GROUPS * s_new
    psum.append(jnp.zeros((LANES - n_imp, nc), F32))
    imp = _dot01(jnp.concatenate(psum, axis=0), ov_ref[...])
    cur = jnp.full((1, LANES), p_len // SLC_BLOCK, jnp.int32)
    sel = _topk_select(imp, cur)[0:n_imp]
    selneg = jnp.where(sel > 0.0, 0.0, NEG).astype(BF16)
    selneg_rows = jnp.concatenate([selneg[g * s_new:(g + 1) * s_new] for g in range(NSA_GROUPS)
                                   for _ in range(NSA_HPG)], axis=0)

    kvt = jnp.concatenate([r[...] for r in page_refs], axis=1).astype(BF16)
    kaug = jnp.concatenate([kvt[0:LANES], e_ref[...]], axis=0)
    s_past = _dot(jnp.concatenate([q2, selneg_rows], axis=1), kaug)
    ns = ns_ref[...].astype(BF16)
    out = out + gate_rows(1) * attend(s_past, _dot_nt(q2, ns[:, 0:LANES]), kvt[LANES:2 * LANES], ns)

    wt = win_ref[...].astype(BF16)
    nw = nw_ref[...].astype(BF16)
    s_b = _dot(q2, wt[0:LANES])
    scol = _iota((rows, w_buf), 1)
    s_b = jnp.where((w_buf + trow - scol < WINDOW) & (p_len - w_buf + scol >= 0), s_b, NEG)
    out = out + gate_rows(2) * attend(s_b, _dot_nt(q2, nw[:, 0:LANES]), wt[LANES:2 * LANES], nw)

    for pair in range(NSA_HEADS // 2):
        parts = []
        for hh in (2 * pair, 2 * pair + 1):
            x = out[hh * s_new:(hh + 1) * s_new]
            if hh % 2 != hh // NSA_HPG:
                x = jnp.concatenate([x[:, HEAD_DIM:], x[:, :HEAD_DIM]], axis=1)
            parts.append(x)
        lh = _iota((s_new, LANES), 1) >> 6
        o_ref[:, pair * LANES:(pair + 1) * LANES] = jnp.where(lh == 0, parts[0], parts[1]).astype(o_ref.dtype)


def _nsa_sample(q, gates, ckv, cache_t, page_table, win_t, new_s, new_w, ov, e_all):
    bd, s_new, _ = q.shape
    n_pages = page_table.shape[1]
    p_len = n_pages * PAGE_SIZE
    assert s_new & (s_new - 1) == 0
    full = lambda a: pl.BlockSpec(a.shape, lambda i, pt: (0, 0))
    seq = lambda a: pl.BlockSpec((None,) + a.shape[1:], lambda i, pt: (i, 0, 0))
    page = lambda k: pl.BlockSpec((None, KV_W, PAGE_SIZE), lambda i, pt, k=k: (pt[i, k], 0, 0))
    return pl.pallas_call(
        functools.partial(_nsa_sample_body, n_pages=n_pages, s_new=s_new, n_cmp=p_len // CMP_STRIDE - 1, p_len=p_len),
        grid_spec=pltpu.PrefetchScalarGridSpec(
            num_scalar_prefetch=1,
            grid=(bd,),
            in_specs=[page(k) for k in range(n_pages)] + [seq(q), seq(gates), seq(ckv), seq(win_t), seq(new_s),
                                                          seq(new_w), full(ov), full(e_all)],
            out_specs=pl.BlockSpec((None, s_new, NQ), lambda i, pt: (i, 0, 0)),
        ),
        out_shape=jax.ShapeDtypeStruct((bd, s_new, NQ), BF16),
        compiler_params=_params("parallel"),
        name="nsa_sample",
    )(page_table, *([cache_t] * n_pages), q, gates, ckv, win_t, new_s, new_w, ov, e_all)


def _gla_body(q_ref, k_ref, v_ref, lg_ref, r_ref, gn_ref, s0_ref, o_ref, sfin_ref, s_sc, *, ns, tb, chunk, sub):
    j = pl.program_id(1)

    @pl.when(j == 0)
    def _():
        s_sc[...] = s0_ref[...]

    nchunk = tb // chunk
    nsub = chunk // sub
    ri = _iota((tb, tb), 0)
    ci = _iota((tb, tb), 1)
    tril = jnp.where((_div_pow2(ri, chunk) == _div_pow2(ci, chunk)) & (ci <= ri), 1.0, 0.0).astype(BF16)
    head_of_lane = _div_pow2(_iota((1, GQK), 1), GLA_DK)
    rows = _iota((chunk, GQK), 0)
    ones_dv = jnp.ones((chunk, GLA_DV), BF16)
    vhead = _div_pow2(_iota((chunk, GV), 1), GLA_DV)
    gn = gn_ref[...]

    units = [(si, c) for si in range(ns) for c in range(nchunk)]
    b_seq = [_dot01_l(tril, lg_ref[si]) for si in range(ns)]
    pre = {}
    for si, c in units:
        c0 = c * chunk
        bc = b_seq[si][c0:c0 + chunk]
        qc = q_ref[si, c0:c0 + chunk, :]
        kc = k_ref[si, c0:c0 + chunk, :]
        b_last = bc[chunk - 1:chunk]
        qdb = qc * jnp.exp(bc)
        kdl = kc * jnp.exp(b_last - bc)
        decay = jnp.exp(_dot01_tn(jnp.where(rows == chunk - 1, bc, 0.0), ones_dv))
        kds = []
        for sb in range(nsub):
            rr = sb * sub
            ref_row = bc[rr - 1:rr] if sb else jnp.zeros((1, GQK), F32)
            vis = rows < rr + sub
            kds.append((ref_row, jnp.where(vis, kc * jnp.exp(jnp.where(vis, ref_row - bc, 0.0)), 0.0).astype(BF16)))
        vc = v_ref[si, c0:c0 + chunk, :].astype(BF16)
        vbd = jnp.concatenate([jnp.where(vhead == h, vc, jnp.zeros_like(vc)) for h in range(GLA_HEADS)], axis=0)
        o_sub = []
        for sb in range(nsub):
            rr = sb * sub
            ref_row, kd = kds[sb]
            qd = qc[rr:rr + sub] * jnp.exp(bc[rr:rr + sub] - ref_row)
            qst = jnp.concatenate([jnp.where(head_of_lane == h, qd, 0.0) for h in range(GLA_HEADS)], axis=0)
            a = _dot_nt(qst.astype(BF16), kd)
            qrow = _iota((GLA_HEADS * sub, chunk), 0) & (sub - 1)
            a = jnp.where(_iota((GLA_HEADS * sub, chunk), 1) <= rr + qrow, a, 0.0)
            a_cat = jnp.concatenate([a[h * sub:(h + 1) * sub] for h in range(GLA_HEADS)],
                                    axis=1).astype(BF16)
            o_sub.append(a_cat)
        o_intra = _dot(o_sub[0] if nsub == 1 else jnp.concatenate(o_sub, axis=0), vbd)
        qdb_st = jnp.concatenate([jnp.where(head_of_lane == h, qdb, 0.0) for h in range(GLA_HEADS)],
                                 axis=0).astype(BF16)
        kv = _dot_tn(kdl.astype(BF16), vc)
        upd = jnp.concatenate([kv[h * GLA_DK:(h + 1) * GLA_DK, h * GLA_DV:(h + 1) * GLA_DV]
                               for h in range(GLA_HEADS)], axis=0)
        pre[si, c] = (decay, upd, o_intra, qdb_st)

    for si in range(ns):
        state = s_sc[si]
        for c in range(nchunk):
            c0 = c * chunk
            decay, upd, o_intra, qdb_st = pre[si, c]
            o_inter = _dot(qdb_st, state.astype(BF16))
            for h in range(GLA_HEADS):
                hs = slice(h * GLA_DV, (h + 1) * GLA_DV)
                o = o_inter[h * chunk:(h + 1) * chunk] + o_intra[:, hs]
                o_ref[si, c0:c0 + chunk, hs] = (_rms(o, gn) * r_ref[si, c0:c0 + chunk, hs].astype(F32)).astype(o_ref.dtype)
            state = decay * state + upd
        s_sc[si] = state

    @pl.when(j == pl.num_programs(1) - 1)
    def _():
        sfin_ref[...] = s_sc[...]


def _gla(q, k, v, lg, r, gn, s0, ns, tb, chunk):
    s, t, _ = q.shape
    sub = min(GLA_SUB, chunk)
    row = lambda w: pl.BlockSpec((ns, tb, w), lambda gi, j: (gi, j, 0))
    st = pl.BlockSpec((ns, GQK, GLA_DV), lambda gi, j: (gi, 0, 0))
    return pl.pallas_call(
        functools.partial(_gla_body, ns=ns, tb=tb, chunk=chunk, sub=sub),
        grid=(s // ns, t // tb),
        in_specs=[row(GQK), row(GQK), row(GV), row(GQK), row(GV), pl.BlockSpec(gn.shape, lambda gi, j: (0, 0)), st],
        out_specs=[row(GV), st],
        out_shape=[jax.ShapeDtypeStruct((s, t, GV), BF16), jax.ShapeDtypeStruct((s, GQK, GLA_DV), F32)],
        scratch_shapes=[pltpu.VMEM((ns, GQK, GLA_DV), F32)],
        compiler_params=_params("parallel", "arbitrary"),
        name="gla",
    )(q, k, v, lg, r, gn, s0)


def _mix_body(x_ref, sc_ref, sh_ref, ga_ref, gpre_ref, gpost_ref, on_ref, og_ref, wm_ref, wn_ref, wgo_ref, wo_ref,
              o_ref):
    x = x_ref[...]
    d = x.shape[1]
    h = _norm_mod(x, gpre_ref[...], sc_ref[...], sh_ref[...]).astype(BF16)
    m = _sigmoid(_dot(h, wm_ref[...]))
    y_a = _dot(on_ref[...], wn_ref[...])
    y_b = _dot(og_ref[...], wgo_ref[...])
    mixin = (m[:, 0:d] * y_a + m[:, d:2 * d] * y_b).astype(BF16)
    mix = _dot(mixin, wo_ref[...])
    o_ref[...] = x + ga_ref[...] * _rms(mix, gpost_ref[...])


def _mix(x, sc, sh, ga, gpre, gpost, o_nsa, o_gla, wm, wn, wgo, wo, tm, rows_per_mod):
    n, d = x.shape
    row = lambda w: pl.BlockSpec((tm, w), lambda i: (i, 0))
    full = lambda a: pl.BlockSpec(a.shape, lambda i: (0, 0))
    ms = lambda a: _mod_spec(a, tm, rows_per_mod)
    return pl.pallas_call(
        _mix_body,
        grid=(n // tm,),
        in_specs=[row(d), ms(sc), ms(sh), ms(ga), full(gpre), full(gpost), row(NQ), row(GV),
                  full(wm), full(wn), full(wgo), full(wo)],
        out_specs=row(d),
        out_shape=jax.ShapeDtypeStruct((n, d), F32),
        compiler_params=_params("parallel"),
        name="mix",
    )(x, sc, sh, ga, gpre, gpost, o_nsa, o_gla, wm, wn, wgo, wo)


def _ffn_body(x_ref, sc_ref, sh_ref, ga_ref, gpre_ref, gpost_ref, wg_ref, wu_ref, wd_ref, o_ref, h_sc, acc_sc):
    j = pl.program_id(1)

    @pl.when(j == 0)
    def _():
        h_sc[...] = _norm_mod(x_ref[...], gpre_ref[...], sc_ref[...], sh_ref[...]).astype(BF16)
        acc_sc[...] = jnp.zeros(acc_sc.shape, F32)

    h = h_sc[...]
    gt = _dot(h, wg_ref[...])
    up = _dot(h, wu_ref[...])
    a = (gt * _sigmoid(gt) * up).astype(BF16)
    acc_sc[...] += _dot(a, wd_ref[...])

    @pl.when(j == pl.num_programs(1) - 1)
    def _():
        o_ref[...] = x_ref[...] + ga_ref[...] * _rms(acc_sc[...], gpost_ref[...])


def _ffn(x, sc, sh, ga, gpre, gpost, wg, wu, wd, tm, rows_per_mod):
    n, d = x.shape
    dff = wg.shape[1]
    tf = dff // 2 if (dff // 2) % LANES == 0 else dff
    row = pl.BlockSpec((tm, d), lambda i, j: (i, 0))
    full = lambda a: pl.BlockSpec(a.shape, lambda i, j: (0, 0))
    ms = lambda a: pl.BlockSpec((None, a.shape[1], a.shape[2]), lambda i, j: ((i * tm) // rows_per_mod, 0, 0))
    return pl.pallas_call(
        _ffn_body,
        grid=(n // tm, dff // tf),
        in_specs=[row, ms(sc), ms(sh), ms(ga), full(gpre), full(gpost),
                  pl.BlockSpec((d, tf), lambda i, j: (0, j)), pl.BlockSpec((d, tf), lambda i, j: (0, j)),
                  pl.BlockSpec((tf, d), lambda i, j: (j, 0))],
        out_specs=row,
        out_shape=jax.ShapeDtypeStruct((n, d), F32),
        scratch_shapes=[pltpu.VMEM((tm, d), BF16), pltpu.VMEM((tm, d), F32)],
        compiler_params=_params("parallel", "arbitrary"),
        name="ffn",
    )(x, sc, sh, ga, gpre, gpost, wg, wu, wd)


def _split_w_in(w_in, d):
    sizes = (NQ, KV_W, KV_W, KV_W, 3 * NSA_HEADS, GQK, GQK, GV, 16, GV, 2 * d)
    out, o = [], 0
    for s in sizes:
        out.append(w_in[:, o:o + s])
        o += s
    return out


def _pad_cols(w, n):
    return jnp.pad(w, ((0, 0), (0, n - w.shape[1])))


def _compress_weights(w1k, b1k, w2k, pek, w1v, b1v, w2v, pev):
    hid = w2k.shape[0]
    npair = CMP_STRIDE // 2

    def expand_w1(half):
        out = jnp.zeros((2, CMP_STRIDE, NSA_GROUPS, HEAD_DIM, NSA_GROUPS, hid), F32)
        for kv, w1 in enumerate((w1k, w1v)):
            w = w1.reshape(CMP_LEN, HEAD_DIM, hid)[half * CMP_STRIDE:(half + 1) * CMP_STRIDE]
            for g in range(NSA_GROUPS):
                out = out.at[kv, :, g, :, g, :].set(w)
        return out.reshape(2, npair, 2 * NSA_GROUPS * HEAD_DIM, NSA_GROUPS * hid)

    w1 = jnp.concatenate([expand_w1(0), expand_w1(1)], axis=-1).astype(BF16)

    def expand_pe(half):
        rows = [jnp.broadcast_to(pe[half * CMP_STRIDE:(half + 1) * CMP_STRIDE, None, :],
                                 (CMP_STRIDE, NSA_GROUPS, HEAD_DIM)) for pe in (pek, pev)]
        return jnp.stack(rows, axis=0).reshape(2, npair, 1, 2 * NSA_GROUPS * HEAD_DIM)

    pe = jnp.concatenate([expand_pe(0), expand_pe(1), jnp.zeros((2, npair, PE_ROWS - 2, 2 * LANES), F32)],
                         axis=2).astype(BF16)

    b1 = jnp.concatenate([jnp.tile(b1k, NSA_GROUPS), jnp.tile(b1v, NSA_GROUPS)]).reshape(1, -1)
    w2 = jnp.zeros((2, NSA_GROUPS, hid, 2, NSA_GROUPS, HEAD_DIM), F32)
    for g in range(NSA_GROUPS):
        w2 = w2.at[0, g, :, 0, g, :].set(w2k)
        w2 = w2.at[1, g, :, 1, g, :].set(w2v)
    w2 = w2.reshape(2 * NSA_GROUPS * hid, KV_W).astype(BF16)
    return pe, w1, b1, w2


def _overlap(n_cmp_pad, n_cmp, nb):
    n = np.arange(n_cmp_pad)[:, None]
    j = np.arange(nb)[None, :] * SLC_BLOCK
    ok = (n * CMP_STRIDE < j + SLC_BLOCK) & (n * CMP_STRIDE + CMP_LEN > j) & (n < n_cmp)
    return ok.astype(BF16)


def _block_onehot(nb, nkeys):
    return (np.arange(nb)[:, None] == (np.arange(nkeys)[None, :] // SLC_BLOCK)).astype(BF16)


def _tile_rows(n, cap):
    t = cap
    while n % t:
        t //= 2
    return t


def _feature_major(a):
    nd = a.ndim
    a = jnp.moveaxis(a, nd - 4, nd - 1)
    return a.reshape(a.shape[:nd - 4] + (KV_W, a.shape[-1]))


def kernel(x_prompt, x_sample, cache_cmp_kv, cache_slc_kv, state_win_kv, state_gla, page_table, c_prompt, c_sample, w_ada, b_ada, g_pre_mix, g_post_mix, w_in, cmp_w1_k, cmp_b1_k, cmp_w2_k, cmp_pe_k, cmp_w1_v, cmp_b1_v, cmp_w2_v, cmp_pe_v, w_gla_a2, b_gla_a, gla_norm_g, w_nsa_o, w_gla_o, w_out, g_pre_ffn, g_post_ffn, w_ff_gate, w_ff_up, w_ff_down):
    depth = w_ada.shape[0]
    bp, l, d = x_prompt.shape
    bd, s_new, _ = x_sample.shape
    n_pages = page_table.shape[1]
    p_len = n_pages * PAGE_SIZE
    w_buf = state_win_kv.shape[2]
    keep = min(WINDOW, l)
    np_, ns_ = bp * l, bd * s_new
    tmp = _tile_rows(np_ // bp, 512)
    tms = _tile_rows(ns_, 512)
    tq = _tile_rows(l, 256)
    tk = 2 * tq

    yp, ys = x_prompt.reshape(np_, d), x_sample.reshape(ns_, d)
    col = [[] for _ in range(8)]
    for li in range(depth):
        r_all = bp + bd
        r_pad = -(-r_all // 8) * 8
        c_all = jnp.pad(jnp.concatenate([c_prompt, c_sample], axis=0), ((0, r_pad - r_all), (0, 0)))
        mod = _ada(c_all, w_ada[li], b_ada[li])
        mods_p = [m.reshape(bp, 1, d) for m in jnp.split(mod[:bp], 6, axis=-1)]
        mods_s = [jnp.repeat(m, s_new, axis=0).reshape(ns_ // tms, tms, d) for m in jnp.split(mod[bp:r_all], 6, axis=-1)]

        (w_q, w_kc, w_ks, w_kw, w_gn, w_qg, w_kg, w_vg, w_ag, w_rg, w_mg) = _split_w_in(w_in[li], d)
        wq = w_q.astype(BF16)
        wkv = jnp.concatenate([w_kc, w_ks, w_kw], axis=1).astype(BF16)
        wgn = _pad_cols(w_gn, LANES).astype(BF16)
        wqkv = jnp.concatenate([w_qg, w_kg, w_vg], axis=1).astype(BF16)
        wa = _pad_cols(w_ag, LANES).astype(BF16)
        wr = w_rg.astype(BF16)
        a2 = jnp.pad(w_gla_a2[li], ((0, LANES - w_gla_a2.shape[1]), (0, 0)))
        ba = b_gla_a[li].reshape(1, GQK)
        cw = _compress_weights(cmp_w1_k[li], cmp_b1_k[li], cmp_w2_k[li], cmp_pe_k[li],
                               cmp_w1_v[li], cmp_b1_v[li], cmp_w2_v[li], cmp_pe_v[li])
        gpre = g_pre_mix[li].reshape(1, d)
        gpost = g_post_mix[li].reshape(1, d)
        gpre2 = g_pre_ffn[li].reshape(1, d)
        gpost2 = g_post_ffn[li].reshape(1, d)
        gn = gla_norm_g[li].reshape(1, GLA_DV)
        wm = w_mg.astype(BF16)
        wn = w_nsa_o[li].astype(BF16)
        wgo = w_gla_o[li].astype(BF16)
        wo = w_out[li].astype(BF16)
        wfg = w_ff_gate[li].astype(BF16)
        wfu = w_ff_up[li].astype(BF16)
        wfd = w_ff_down[li].astype(BF16)

        sh1, sc1, ga1, sh2, sc2, ga2 = mods_p
        q, kvc, kvs, kvw, kvs_b, kvw_b, gates = _pre_nsa(yp, sc1, sh1, gpre, wq, wkv, wgn, tmp, l)
        qg, kg, vg, lg, rg = _pre_gla(yp, sc1, sh1, gpre, wqkv, wa, wr, a2, ba, tmp, l)
        ckv = _cmp_prompt(kvc.reshape(bp, l, KV_W), cw)
        nch = l // CMP_STRIDE
        nb = l // SLC_BLOCK
        o_nsa = _nsa_prompt(q.reshape(bp, l, NQ), gates.reshape(bp, l, LANES), ckv, kvs_b.reshape(bp, l, KV_W),
                            kvw_b.reshape(bp, l, KV_W), _overlap(nch, nch - 1, nb), _block_onehot(nb, l).T, tq, tk)
        chunk = GLA_CHUNK if l % GLA_CHUNK == 0 else l
        tb = 2 * LANES if (l % (2 * LANES) == 0 and LANES % chunk == 0) else chunk
        o_gla, s_fin_p = _gla(qg.reshape(bp, l, GQK), kg.reshape(bp, l, GQK), vg.reshape(bp, l, GV),
                              lg.reshape(bp, l, GQK), rg.reshape(bp, l, GV), gn,
                              jnp.zeros((bp, GQK, GLA_DV), F32), bp, tb, chunk)
        x1 = _mix(yp, sc1, sh1, ga1, gpre, gpost, o_nsa.reshape(np_, NQ), o_gla.reshape(np_, GV), wm, wn, wgo, wo, tmp, l)
        yp = _ffn(x1, sc2, sh2, ga2, gpre2, gpost2, wfg, wfu, wfd, tmp, l)
        col[0].append(kvc.reshape(bp, l, 2, NSA_GROUPS, HEAD_DIM))
        col[2].append(kvs.reshape(bp, l, 2, NSA_GROUPS, HEAD_DIM))
        col[4].append(kvw.reshape(bp, l, KV_W)[:, l - keep:].reshape(bp, keep, 2, NSA_GROUPS, HEAD_DIM))
        col[6].append(s_fin_p.reshape(bp, GLA_HEADS, GLA_DK, GLA_DV).astype(state_gla.dtype))

        sh1, sc1, ga1, sh2, sc2, ga2 = mods_s
        q, kvc, kvs, kvw, _, _, gates = _pre_nsa(ys, sc1, sh1, gpre, wq, wkv, wgn, tms, tms)
        qg, kg, vg, lg, rg = _pre_gla(ys, sc1, sh1, gpre, wqkv, wa, wr, a2, ba, tms, tms)
        ckv = _cmp_sample(_feature_major(cache_cmp_kv[li]), page_table, cw)
        nch = p_len // CMP_STRIDE
        nb = p_len // SLC_BLOCK
        win_t = _feature_major(state_win_kv[li])
        kvw3 = kvw.reshape(bd, s_new, KV_W)
        o_nsa = _nsa_sample(q.reshape(bd, s_new, NQ), gates.reshape(bd, s_new, LANES), ckv,
                            _feature_major(cache_slc_kv[li]), page_table, win_t, kvs.reshape(bd, s_new, KV_W), kvw3,
                            _overlap(nch, nch - 1, nb), _block_onehot(nb, p_len))
        chunk = GLA_CHUNK if s_new % GLA_CHUNK == 0 else s_new
        o_gla, s_fin_s = _gla(qg.reshape(bd, s_new, GQK), kg.reshape(bd, s_new, GQK), vg.reshape(bd, s_new, GV),
                              lg.reshape(bd, s_new, GQK), rg.reshape(bd, s_new, GV), gn,
                              state_gla[li].astype(F32).reshape(bd, GQK, GLA_DV), _tile_rows(bd, 16), chunk, chunk)
        x1 = _mix(ys, sc1, sh1, ga1, gpre, gpost, o_nsa.reshape(ns_, NQ), o_gla.reshape(ns_, GV), wm, wn, wgo, wo, tms, tms)
        ys = _ffn(x1, sc2, sh2, ga2, gpre2, gpost2, wfg, wfu, wfd, tms, tms)
        win_new_t = jnp.concatenate([win_t, kvw3.transpose(0, 2, 1)], axis=2)[:, :, s_new:]
        win_new = jnp.moveaxis(win_new_t.reshape(bd, 2, NSA_GROUPS, HEAD_DIM, w_buf), 4, 1)
        col[1].append(kvc.reshape(bd, s_new, 2, NSA_GROUPS, HEAD_DIM))
        col[3].append(kvs.reshape(bd, s_new, 2, NSA_GROUPS, HEAD_DIM))
        col[5].append(win_new)
        col[7].append(s_fin_s.reshape(bd, GLA_HEADS, GLA_DK, GLA_DV).astype(state_gla.dtype))

    stacked = [c[0][None] if len(c) == 1 else jnp.stack(c) for c in col]
    return (yp.reshape(bp, l, d), ys.reshape(bd, s_new, d), *stacked)
```

```python
import functools
import math

import numpy as np
import jax
import jax.numpy as jnp
from jax import lax
from jax.experimental import pallas as pl
from jax.experimental.pallas import tpu as pltpu

F32 = jnp.float32
BF16 = jnp.bfloat16

NSA_HEADS = 8
NSA_GROUPS = 2
NSA_HPG = NSA_HEADS // NSA_GROUPS
HEAD_DIM = 64
CMP_STRIDE = 16
CMP_LEN = 32
SLC_BLOCK = 64
N_SEL = 16
WINDOW = 512
GLA_HEADS = 4
GLA_DK = 64
GLA_DV = 128
GLA_TAU = 16.0
GLA_CHUNK = 64
GLA_SUB = 16
PE_ROWS = 16
EPS = 1e-6
PAGE_SIZE = 128
NEG = -1e30
LOG2E = math.log2(math.e)
LANES = 128
VMEM_LIMIT = 56 * 1024 * 1024

KV_W = 2 * NSA_GROUPS * HEAD_DIM
NQ = NSA_HEADS * HEAD_DIM
GQK = GLA_HEADS * GLA_DK
GV = GLA_HEADS * GLA_DV


def _dot(a, b):
    return jnp.dot(a, b, preferred_element_type=F32)


def _dot_nt(a, b):
    return lax.dot_general(a, b, (((1,), (1,)), ((), ())), preferred_element_type=F32)


def _dot_f32(a, b):
    return jnp.dot(a, b, preferred_element_type=F32, precision=lax.Precision.HIGHEST)


def _dot01(x, m01):
    hi = x.astype(BF16)
    r1 = x - hi.astype(F32)
    mid = r1.astype(BF16)
    lo = (r1 - mid.astype(F32)).astype(BF16)
    return _dot(hi, m01) + _dot(mid, m01) + _dot(lo, m01)


def _dot01_l(m01, x):
    hi = x.astype(BF16)
    r1 = x - hi.astype(F32)
    mid = r1.astype(BF16)
    lo = (r1 - mid.astype(F32)).astype(BF16)
    return _dot(m01, hi) + _dot(m01, mid) + _dot(m01, lo)


def _dot_tn(a, b):
    return lax.dot_general(a, b, (((0,), (0,)), ((), ())), preferred_element_type=F32)


def _dot01_tn(x, m01):
    hi = x.astype(BF16)
    r1 = x - hi.astype(F32)
    mid = r1.astype(BF16)
    lo = (r1 - mid.astype(F32)).astype(BF16)
    return _dot_tn(hi, m01) + _dot_tn(mid, m01) + _dot_tn(lo, m01)


def _sigmoid(x):
    return 1.0 / (1.0 + jnp.exp(-x))


def _params(*sem):
    return pltpu.CompilerParams(dimension_semantics=sem, vmem_limit_bytes=VMEM_LIMIT)


def _iota(shape, dim):
    return lax.broadcasted_iota(jnp.int32, shape, dim)


def _div_pow2(x, n):
    assert n & (n - 1) == 0, n
    return x >> (n.bit_length() - 1)


def _ada_body(c_ref, w_ref, b_ref, o_ref):
    o_ref[...] = _dot_f32(c_ref[...], w_ref[...]) + b_ref[...]


def _ada(c_all, w_ada, b_ada):
    r, d = c_all.shape
    n = w_ada.shape[1]
    tn = 1536 if n % 1536 == 0 else n
    return pl.pallas_call(
        _ada_body,
        grid=(n // tn,),
        in_specs=[pl.BlockSpec((r, d), lambda j: (0, 0)),
                  pl.BlockSpec((d, tn), lambda j: (0, j)),
                  pl.BlockSpec((1, tn), lambda j: (0, j))],
        out_specs=pl.BlockSpec((r, tn), lambda j: (0, j)),
        out_shape=jax.ShapeDtypeStruct((r, n), F32),
        compiler_params=_params("arbitrary"),
        name="ada",
    )(c_all, w_ada, b_ada.reshape(1, n))


def _norm_mod(x, g, sc, sh):
    ms = jnp.mean(x * x, axis=-1, keepdims=True)
    y = x * lax.rsqrt(ms + EPS) * g
    return y * (1.0 + sc) + sh


def _rms(x, g):
    ms = jnp.mean(x * x, axis=-1, keepdims=True)
    return x * lax.rsqrt(ms + EPS) * g


def _pre_nsa_body(x_ref, sc_ref, sh_ref, g_ref, wq_ref, wkv_ref, wg_ref,
                  q_ref, kvc_ref, kvs_ref, kvw_ref, kvsb_ref, kvwb_ref, gate_ref, *t_refs):
    h = _norm_mod(x_ref[...], g_ref[...], sc_ref[...], sh_ref[...]).astype(BF16)
    q_ref[...] = (_dot(h, wq_ref[...]) * (HEAD_DIM ** -0.5 * LOG2E)).astype(BF16)
    kv = _dot(h, wkv_ref[...])
    kvc = kv[:, 0:KV_W]
    kvs = kv[:, KV_W:2 * KV_W]
    kvw = kv[:, 2 * KV_W:3 * KV_W]
    kvc_ref[...] = kvc
    kvs_ref[...] = kvs
    kvw_ref[...] = kvw
    kvsb_ref[...] = kvs.astype(BF16)
    kvwb_ref[...] = kvw.astype(BF16)
    gate_ref[...] = _sigmoid(_dot(h, wg_ref[...]))
    if t_refs:
        kvt = kv.T
        for j, t_ref in enumerate(t_refs):
            t_ref[...] = kvt[j * KV_W:(j + 1) * KV_W]


def _mod_spec(mod3, tm, rows_per_mod):
    r = mod3.shape[1]
    return pl.BlockSpec((None, r, mod3.shape[2]), lambda i: ((i * tm) // rows_per_mod, 0, 0))


def _pre_nsa(x, sc, sh, g, wq, wkv, wg, tm, rows_per_mod, feature_major=None):
    n, d = x.shape
    row = lambda w: pl.BlockSpec((tm, w), lambda i: (i, 0))
    full = lambda a: pl.BlockSpec(a.shape, lambda i: (0, 0))
    outs = [(NQ, BF16), (KV_W, F32), (KV_W, F32), (KV_W, F32), (KV_W, BF16), (KV_W, BF16), (LANES, F32)]
    out_specs = [row(w) for w, _ in outs]
    out_shape = [jax.ShapeDtypeStruct((n, w), t) for w, t in outs]
    if feature_major is not None:
        nbatch, per = feature_major
        tiles = per // tm
        out_specs += [pl.BlockSpec((None, KV_W, tm), lambda i: (i // tiles, 0, i % tiles))] * 3
        out_shape += [jax.ShapeDtypeStruct((nbatch, KV_W, per), F32)] * 3
    return pl.pallas_call(
        _pre_nsa_body,
        grid=(n // tm,),
        in_specs=[row(d), _mod_spec(sc, tm, rows_per_mod), _mod_spec(sh, tm, rows_per_mod), full(g),
                  full(wq), full(wkv), full(wg)],
        out_specs=out_specs,
        out_shape=out_shape,
        compiler_params=_params("parallel"),
        name="pre_nsa",
    )(x, sc, sh, g, wq, wkv, wg)


def _log_sigmoid(x):
    return jnp.minimum(x, 0.0) - jnp.log(1.0 + jnp.exp(-jnp.abs(x)))


def _pre_gla_body(x_ref, sc_ref, sh_ref, g_ref, wqkv_ref, wa_ref, wr_ref, a2_ref, ba_ref,
                  q_ref, k_ref, v_ref, lg_ref, r_ref):
    h = _norm_mod(x_ref[...], g_ref[...], sc_ref[...], sh_ref[...]).astype(BF16)
    qkv = _dot(h, wqkv_ref[...])
    q_ref[...] = qkv[:, 0:GQK] * GLA_DK ** -0.5
    k_ref[...] = qkv[:, GQK:2 * GQK]
    v_ref[...] = qkv[:, 2 * GQK:2 * GQK + GV]
    a = _dot(h, wa_ref[...])
    lg_ref[...] = _log_sigmoid(_dot_f32(a, a2_ref[...]) + ba_ref[...]) * (1.0 / GLA_TAU)
    r = _dot(h, wr_ref[...])
    r_ref[...] = (r * _sigmoid(r)).astype(BF16)


def _pre_gla(x, sc, sh, g, wqkv, wa, wr, a2, ba, tm, rows_per_mod):
    n, d = x.shape
    row = lambda w: pl.BlockSpec((tm, w), lambda i: (i, 0))
    full = lambda a: pl.BlockSpec(a.shape, lambda i: (0, 0))
    return pl.pallas_call(
        _pre_gla_body,
        grid=(n // tm,),
        in_specs=[row(d), _mod_spec(sc, tm, rows_per_mod), _mod_spec(sh, tm, rows_per_mod), full(g),
                  full(wqkv), full(wa), full(wr), full(a2), full(ba)],
        out_specs=[row(GQK), row(GQK), row(GV), row(GQK), row(GV)],
        out_shape=[jax.ShapeDtypeStruct((n, GQK), F32), jax.ShapeDtypeStruct((n, GQK), F32),
                   jax.ShapeDtypeStruct((n, GV), F32), jax.ShapeDtypeStruct((n, GQK), F32),
                   jax.ShapeDtypeStruct((n, GV), BF16)],
        compiler_params=_params("parallel"),
        name="pre_gla",
    )(x, sc, sh, g, wqkv, wa, wr, a2, ba)


def _gelu_tanh(x):
    return 0.5 * x * (1.0 + jnp.tanh(0.7978845608028654 * (x + 0.044715 * x * x * x)))


def _compress(get_xp, nch, pe_ref, w1_ref, b1_ref, w2_ref):
    first, second = [], []
    for kv in range(2):
        acc = jnp.zeros((nch + PE_ROWS, 2 * LANES), F32)
        for pp in range(CMP_STRIDE // 2):
            lhs = jnp.concatenate([get_xp(kv, 2 * pp), get_xp(kv, 2 * pp + 1)], axis=1).astype(BF16)
            acc = acc + _dot(jnp.concatenate([lhs, pe_ref[kv, pp]], axis=0), w1_ref[kv, pp])
        first.append(acc[0:nch, 0:LANES] + acc[nch:nch + 1, 0:LANES])
        second.append(acc[0:nch, LANES:2 * LANES] + acc[nch + 1:nch + 2, LANES:2 * LANES])
    first = jnp.concatenate(first, axis=1)
    second = jnp.concatenate(second, axis=1)
    if nch % 8 == 0:
        nxt = pltpu.roll(second, nch - 1, 0)
    else:
        nxt = jnp.concatenate([second[1:], second[:1]], axis=0)
    hid = _gelu_tanh(first + nxt + b1_ref[...])
    out = _dot(hid.astype(BF16), w2_ref[...])
    return jnp.where(_iota(out.shape, 0) < nch - 1, out, 0.0)


def _cmp_prompt_body(xk_ref, xv_ref, pe_ref, w1_ref, b1_ref, w2_ref, o_ref, *, nch):
    x_refs = (xk_ref, xv_ref)

    def get_xp(kv, p):
        return x_refs[kv][pl.ds(p, nch, stride=CMP_STRIDE), :]

    o_ref[...] = _compress(get_xp, nch, pe_ref, w1_ref, b1_ref, w2_ref).astype(o_ref.dtype)


def _cmp_prompt(kvc, cw):
    b, l, _ = kvc.shape
    nch = l // CMP_STRIDE
    full = lambda a: pl.BlockSpec(a.shape, lambda i: (0,) * a.ndim)
    return pl.pallas_call(
        functools.partial(_cmp_prompt_body, nch=nch),
        grid=(b,),
        in_specs=[pl.BlockSpec((None, l, LANES), lambda i: (i, 0, 0)),
                  pl.BlockSpec((None, l, LANES), lambda i: (i, 0, 1))] + [full(a) for a in cw],
        out_specs=pl.BlockSpec((None, nch, KV_W), lambda i: (i, 0, 0)),
        out_shape=jax.ShapeDtypeStruct((b, nch, KV_W), BF16),
        compiler_params=_params("parallel"),
        name="cmp_prompt",
    )(kvc, kvc, *cw)


def _cmp_sample_body(*refs, n_pages, nch):
    page_refs = refs[1:1 + n_pages]
    pe_ref, w1_ref, b1_ref, w2_ref, o_ref, ak_sc, av_sc, bk_sc, bv_sc = refs[1 + n_pages:]
    i = pl.program_id(0)
    cpp = PAGE_SIZE // CMP_STRIDE

    def regroup(dst):
        for k in range(n_pages):
            for kv in range(2):
                xt = page_refs[k][kv * LANES:(kv + 1) * LANES, :].T
                for n in range(cpp):
                    for a in range(CMP_STRIDE // 8):
                        r0 = n * CMP_STRIDE + 8 * a
                        dst[kv][k, pl.ds(8 * a * cpp + n, 8, stride=cpp), :] = xt[r0:r0 + 8, :]

    def mlp(src):
        def get_xp(kv, p):
            return src[kv][:, p * cpp:(p + 1) * cpp, :].reshape(nch, LANES)

        o_ref[...] = _compress(get_xp, nch, pe_ref, w1_ref, b1_ref, w2_ref).astype(o_ref.dtype)

    @pl.when(i == 0)
    def _():
        bk_sc[...] = jnp.zeros(bk_sc.shape, F32)
        bv_sc[...] = jnp.zeros(bv_sc.shape, F32)

    @pl.when(i % 2 == 0)
    def _():
        regroup((ak_sc, av_sc))
        mlp((bk_sc, bv_sc))

    @pl.when(i % 2 == 1)
    def _():
        regroup((bk_sc, bv_sc))
        mlp((ak_sc, av_sc))


def _cmp_sample(cache_t, page_table, cw):
    bd, n_pages = page_table.shape
    p_len = n_pages * PAGE_SIZE
    nch = p_len // CMP_STRIDE
    full = lambda a: pl.BlockSpec(a.shape, lambda i, pt: (0,) * a.ndim)
    page = lambda k: pl.BlockSpec((None, KV_W, PAGE_SIZE),
                                  lambda i, pt, k=k: (pt[jnp.minimum(i, bd - 1), k], 0, 0))
    return pl.pallas_call(
        functools.partial(_cmp_sample_body, n_pages=n_pages, nch=nch),
        grid_spec=pltpu.PrefetchScalarGridSpec(
            num_scalar_prefetch=1,
            grid=(bd + 1,),
            in_specs=[page(k) for k in range(n_pages)] + [full(a) for a in cw],
            out_specs=pl.BlockSpec((None, nch, KV_W), lambda i, pt: (jnp.maximum(i - 1, 0), 0, 0)),
            scratch_shapes=[pltpu.VMEM((n_pages, PAGE_SIZE, LANES), F32)] * 4,
        ),
        out_shape=jax.ShapeDtypeStruct((bd, nch, KV_W), BF16),
        compiler_params=_params("arbitrary"),
        name="cmp_sample",
    )(page_table, *([cache_t] * n_pages), *cw)


def _head_queries(q, rows):
    half = _iota((rows, LANES), 1) >> 6
    out = []
    for hh in range(NSA_HEADS):
        g = hh // NSA_HPG
        blk = q[:, (hh // 2) * LANES:(hh // 2 + 1) * LANES].astype(F32)
        if hh % 2 != g:
            blk = pltpu.roll(blk, HEAD_DIM, 1) if rows % 8 == 0 else jnp.roll(blk, HEAD_DIM, 1)
        out.append(jnp.where(half == g, blk, 0.0).astype(BF16))
    return out


def _topk_select(imp, cur):
    impt = imp.T
    nb = impt.shape[0]
    jl = _iota(impt.shape, 0)
    jf = jl.astype(F32)
    forced = (jl == 0) | (jl == cur) | (jl == cur - 1)
    elig = (jl >= 1) & (jl <= cur - 2)
    sel0 = jnp.where(forced, 1.0, 0.0)
    alive0 = jnp.where(elig, 1.0, 0.0)

    def step(_, carry):
        sel, alive = carry
        live = alive > 0.0
        vals = jnp.where(live, impt, -1.0)
        m = jnp.max(vals, axis=0, keepdims=True)
        cand = live & (vals == m)
        idx = jnp.min(jnp.where(cand, jf, float(nb)), axis=0, keepdims=True)
        one = jf == idx
        return jnp.where(one, 1.0, sel), jnp.where(one, 0.0, alive)

    sel, _ = lax.fori_loop(0, N_SEL - 3, step, (sel0, alive0))
    return sel.T


def _masked_softmax2(s, valid):
    s = jnp.where(valid, s, NEG)
    m = jnp.max(s, axis=-1, keepdims=True)
    e = jnp.where(valid, jnp.exp2(s - m), 0.0)
    d = jnp.sum(e, axis=-1, keepdims=True)
    return e / jnp.where(d > 0.0, d, 1.0)


def _nsa_prompt_body(q_ref, gate_ref, ckv_ref, kvs_ref, kvw_ref, ov_ref, et_ref, o_ref, qa_sc, m_sc, acc_sc, out_sc,
                     *, tq, tk, n_cmp):
    i = pl.program_id(1)
    t0 = i * tq
    nc = ckv_ref.shape[0]
    nb = ov_ref.shape[1]
    nh = NSA_HEADS
    qh = _head_queries(q_ref[...], tq)
    for hh in range(nh):
        qa_sc[hh * tq:(hh + 1) * tq, 0:LANES] = qh[hh]
    half = _iota((tq, LANES), 1) >> 6
    trow = t0 + _iota((1, tq, 1), 1)
    gates = gate_ref[...]

    def gate(branch, hh):
        c = branch * nh + hh
        return gates[:, c:c + 1]

    ck = ckv_ref[:, 0:LANES]
    cv = ckv_ref[:, LANES:2 * LANES]
    s = _dot_nt(qa_sc[:, 0:LANES], ck).reshape(nh, tq, nc)
    ncol = _iota((1, tq, nc), 2)
    cvalid = (ncol * CMP_STRIDE + CMP_LEN <= trow + 1) & (ncol < n_cmp)
    p = _masked_softmax2(s, cvalid)
    for hh in range(nh):
        out_sc[hh] = gate(0, hh) * _dot(p[hh].astype(BF16), cv)
    ov = ov_ref[...]
    imp = []
    for g in range(NSA_GROUPS):
        psum = p[g * NSA_HPG]
        for h in range(1, NSA_HPG):
            psum = psum + p[g * NSA_HPG + h]
        imp.append(_dot01(psum, ov))
    imp = jnp.concatenate(imp, axis=0)
    cur = (t0 + (_iota((1, NSA_GROUPS * tq), 1) & (tq - 1))) >> 6
    sel = _topk_select(imp, cur)
    selneg = jnp.where(sel > 0.0, 0.0, NEG).astype(BF16)
    for hh in range(nh):
        g = hh // NSA_HPG
        qa_sc[hh * tq:(hh + 1) * tq, LANES:LANES + nb] = selneg[g * tq:(g + 1) * tq]

    def load_v(ref, start, width):
        v128 = ref[pl.ds(start, width), LANES:2 * LANES]
        lane_half = _iota(v128.shape, 1) >> 6
        return [jnp.where(lane_half == g, v128, jnp.ones_like(v128)) for g in range(NSA_GROUPS)]

    def online(s3, vaug, first):
        width = s3.shape[-1]
        tmax = jnp.max(s3, axis=-1, keepdims=True)
        if first:
            m_new = jnp.broadcast_to(tmax, (nh, tq, LANES))
        else:
            m_old = m_sc[...]
            m_new = jnp.maximum(m_old, tmax)
            alpha = jnp.exp2(m_old - m_new)
        pexp = jnp.concatenate([jnp.exp2((s3[:, :, c * LANES:(c + 1) * LANES] - m_new).astype(BF16))
                                for c in range(width // LANES)], axis=-1)
        for hh in range(nh):
            pv = _dot(pexp[hh], vaug[hh // NSA_HPG])
            acc_sc[hh] = pv if first else alpha[hh] * acc_sc[hh] + pv
        m_sc[...] = m_new

    def finish(branch):
        for hh in range(nh):
            acc = acc_sc[hh]
            den = pltpu.roll(acc, HEAD_DIM, 1)
            out_sc[hh] = out_sc[hh] + gate(branch, hh) * (acc / den)

    def slc_scores(start):
        kaug = jnp.concatenate([kvs_ref[pl.ds(start, tk), 0:LANES], et_ref[pl.ds(start, tk), :]], axis=1)
        return _dot_nt(qa_sc[...], kaug).reshape(nh, tq, tk)

    n_full = i // (tk // tq)
    last = pl.multiple_of(n_full * tk, tk)
    s3 = slc_scores(last)
    s3 = jnp.where(last + _iota((1, tq, tk), 2) <= trow, s3, NEG)
    online(s3, load_v(kvs_ref, last, tk), True)

    def slc_step(j, carry):
        start = pl.multiple_of(j * tk, tk)
        online(slc_scores(start), load_v(kvs_ref, start, tk), False)
        return carry

    lax.fori_loop(0, n_full, slc_step, 0)
    finish(1)

    d0 = pl.multiple_of(t0, tq)
    s3 = _dot_nt(qa_sc[:, 0:LANES], kvw_ref[pl.ds(d0, tq), 0:LANES]).reshape(nh, tq, tq)
    s3 = jnp.where(_iota((1, tq, tq), 2) <= _iota((1, tq, tq), 1), s3, NEG)
    online(s3, load_v(kvw_ref, d0, tq), True)

    @pl.when(i >= 1)
    def _():
        start = pl.multiple_of(jnp.maximum(t0 - WINDOW, 0), tq)
        s3 = _dot_nt(qa_sc[:, 0:LANES], kvw_ref[pl.ds(start, WINDOW), 0:LANES]).reshape(nh, tq, WINDOW)
        kpos = start + _iota((1, tq, WINDOW), 2)
        s3 = jnp.where((kpos < t0) & (trow - kpos < WINDOW), s3, NEG)
        online(s3, load_v(kvw_ref, start, WINDOW), False)

    finish(2)

    for pair in range(nh // 2):
        parts = []
        for hh in (2 * pair, 2 * pair + 1):
            x = out_sc[hh]
            if hh % 2 != hh // NSA_HPG:
                x = pltpu.roll(x, HEAD_DIM, 1)
            parts.append(x)
        o_ref[:, pair * LANES:(pair + 1) * LANES] = jnp.where(half == 0, parts[0], parts[1]).astype(o_ref.dtype)


def _nsa_prompt(q, gates, ckv, kvs_b, kvw_b, ov, et, tq, tk):
    b, l, _ = q.shape
    nb = l // SLC_BLOCK
    assert l % tk == 0 and tk % tq == 0 and WINDOW % tq == 0 and l >= WINDOW
    blk = lambda w: pl.BlockSpec((None, tq, w), lambda bi, i: (bi, i, 0))
    seq = lambda a: pl.BlockSpec((None,) + a.shape[1:], lambda bi, i: (bi, 0, 0))
    full = lambda a: pl.BlockSpec(a.shape, lambda bi, i: (0, 0))
    hs = (NSA_HEADS, tq, LANES)
    return pl.pallas_call(
        functools.partial(_nsa_prompt_body, tq=tq, tk=tk, n_cmp=l // CMP_STRIDE - 1),
        grid=(b, l // tq),
        in_specs=[blk(NQ), blk(LANES), seq(ckv), seq(kvs_b), seq(kvw_b), full(ov), full(et)],
        out_specs=blk(NQ),
        out_shape=jax.ShapeDtypeStruct((b, l, NQ), BF16),
        scratch_shapes=[pltpu.VMEM((NSA_HEADS * tq, LANES + nb), BF16), pltpu.VMEM(hs, F32), pltpu.VMEM(hs, F32),
                        pltpu.VMEM(hs, F32)],
        compiler_params=_params("parallel", "arbitrary"),
        name="nsa_prompt",
    )(q, gates, ckv, kvs_b, kvw_b, ov, et)


def _nsa_sample_body(*refs, n_pages, s_new, n_cmp, p_len):
    page_refs = refs[1:1 + n_pages]
    (q_ref, gate_ref, ckv_ref, win_ref, ns_ref, nw_ref, ov_ref, e_ref, o_ref) = refs[1 + n_pages:]
    rows = NSA_HEADS * s_new
    nc = ckv_ref.shape[0]
    nb = ov_ref.shape[1]
    w_buf = win_ref.shape[1]
    q2 = jnp.concatenate(_head_queries(q_ref[...], s_new), axis=0)
    gates = gate_ref[...]
    trow = _iota((rows, 1), 0) & (s_new - 1)
    grow = _div_pow2(_iota((rows, 1), 0), s_new * NSA_HPG)
    lane_half = _iota((rows, LANES), 1) >> 6

    def gate_rows(branch):
        cols = [gates[:, branch * NSA_HEADS + hh:branch * NSA_HEADS + hh + 1] for hh in range(NSA_HEADS)]
        return jnp.concatenate(cols, axis=0)

    def pick(acc, den):
        return jnp.where(lane_half == grow, acc, 0.0) / den

    def attend(s_old, s_nw, vt_old, new_rows):
        s_nw = jnp.where(_iota((rows, s_new), 1) <= trow, s_nw, NEG)
        m = jnp.maximum(jnp.max(s_old, axis=-1, keepdims=True), jnp.max(s_nw, axis=-1, keepdims=True))
        e1 = jnp.exp2(s_old - m)
        e2 = jnp.exp2(s_nw - m)
        den = jnp.sum(e1, axis=-1, keepdims=True) + jnp.sum(e2, axis=-1, keepdims=True)
        acc = _dot_nt(vt_old, e1.astype(BF16)).T + _dot(e2.astype(BF16), new_rows[:, LANES:2 * LANES])
        return pick(acc, den)

    ck = ckv_ref[:, 0:LANES]
    cv = ckv_ref[:, LANES:2 * LANES]
    ncol = _iota((rows, nc), 1)
    cvalid = (ncol * CMP_STRIDE + CMP_LEN <= p_len + trow + 1) & (ncol < n_cmp)
    p_c = _masked_softmax2(_dot_nt(q2, ck), cvalid)
    out = gate_rows(0) * jnp.where(lane_half == grow, _dot(p_c.astype(BF16), cv), 0.0)
    psum = []
    for g in range(NSA_GROUPS):
        acc = jnp.zeros((s_new, nc), F32)
        for h in range(NSA_HPG):
            r0 = (g * NSA_HPG + h) * s_new
            acc = acc + p_c[r0:r0 + s_new]
        psum.append(acc)
    n_imp = NSA_GROUPS * s_new
    psum.append(jnp.zeros((LANES - n_imp, nc), F32))
    imp = _dot01(jnp.concatenate(psum, axis=0), ov_ref[...])
    cur = jnp.full((1, LANES), p_len // SLC_BLOCK, jnp.int32)
    sel = _topk_select(imp, cur)[0:n_imp]
    selneg = jnp.where(sel > 0.0, 0.0, NEG).astype(BF16)
    selneg_rows = jnp.concatenate([selneg[g * s_new:(g + 1) * s_new] for g in range(NSA_GROUPS)
                                   for _ in range(NSA_HPG)], axis=0)

    kvt = jnp.concatenate([r[...] for r in page_refs], axis=1).astype(BF16)
    kaug = jnp.concatenate([kvt[0:LANES], e_ref[...]], axis=0)
    s_past = _dot(jnp.concatenate([q2, selneg_rows], axis=1), kaug)
    ns = ns_ref[...].astype(BF16)
    out = out + gate_rows(1) * attend(s_past, _dot_nt(q2, ns[:, 0:LANES]), kvt[LANES:2 * LANES], ns)

    wt = win_ref[...].astype(BF16)
    nw = nw_ref[...].astype(BF16)
    s_b = _dot(q2, wt[0:LANES])
    scol = _iota((rows, w_buf), 1)
    s_b = jnp.where((w_buf + trow - scol < WINDOW) & (p_len - w_buf + scol >= 0), s_b, NEG)
    out = out + gate_rows(2) * attend(s_b, _dot_nt(q2, nw[:, 0:LANES]), wt[LANES:2 * LANES], nw)

    for pair in range(NSA_HEADS // 2):
        parts = []
        for hh in (2 * pair, 2 * pair + 1):
            x = out[hh * s_new:(hh + 1) * s_new]
            if hh % 2 != hh // NSA_HPG:
                x = jnp.concatenate([x[:, HEAD_DIM:], x[:, :HEAD_DIM]], axis=1)
            parts.append(x)
        lh = _iota((s_new, LANES), 1) >> 6
        o_ref[:, pair * LANES:(pair + 1) * LANES] = jnp.where(lh == 0, parts[0], parts[1]).astype(o_ref.dtype)


def _nsa_sample(q, gates, ckv, cache_t, page_table, win_t, new_s, new_w, ov, e_all):
    bd, s_new, _ = q.shape
    n_pages = page_table.shape[1]
    p_len = n_pages * PAGE_SIZE
    assert s_new & (s_new - 1) == 0
    full = lambda a: pl.BlockSpec(a.shape, lambda i, pt: (0, 0))
    seq = lambda a: pl.BlockSpec((None,) + a.shape[1:], lambda i, pt: (i, 0, 0))
    page = lambda k: pl.BlockSpec((None, KV_W, PAGE_SIZE), lambda i, pt, k=k: (pt[i, k], 0, 0))
    return pl.pallas_call(
        functools.partial(_nsa_sample_body, n_pages=n_pages, s_new=s_new, n_cmp=p_len // CMP_STRIDE - 1, p_len=p_len),
        grid_spec=pltpu.PrefetchScalarGridSpec(
            num_scalar_prefetch=1,
            grid=(bd,),
            in_specs=[page(k) for k in range(n_pages)] + [seq(q), seq(gates), seq(ckv), seq(win_t), seq(new_s),
                                                          seq(new_w), full(ov), full(e_all)],
            out_specs=pl.BlockSpec((None, s_new, NQ), lambda i, pt: (i, 0, 0)),
        ),
        out_shape=jax.ShapeDtypeStruct((bd, s_new, NQ), BF16),
        compiler_params=_params("parallel"),
        name="nsa_sample",
    )(page_table, *([cache_t] * n_pages), q, gates, ckv, win_t, new_s, new_w, ov, e_all)


def _gla_body(q_ref, k_ref, v_ref, lg_ref, r_ref, gn_ref, s0_ref, o_ref, sfin_ref, s_sc, *, ns, tb, chunk, sub):
    j = pl.program_id(1)

    @pl.when(j == 0)
    def _():
        s_sc[...] = s0_ref[...]

    nchunk = tb // chunk
    nsub = chunk // sub
    ri = _iota((tb, tb), 0)
    ci = _iota((tb, tb), 1)
    tril = jnp.where((_div_pow2(ri, chunk) == _div_pow2(ci, chunk)) & (ci <= ri), 1.0, 0.0).astype(BF16)
    head_of_lane = _div_pow2(_iota((1, GQK), 1), GLA_DK)
    rows = _iota((chunk, GQK), 0)
    ones_dv = jnp.ones((chunk, GLA_DV), BF16)
    vhead = _div_pow2(_iota((chunk, GV), 1), GLA_DV)
    gn = gn_ref[...]

    units = [(si, c) for si in range(ns) for c in range(nchunk)]
    b_seq = [_dot01_l(tril, lg_ref[si]) for si in range(ns)]
    pre = {}
    for si, c in units:
        c0 = c * chunk
        bc = b_seq[si][c0:c0 + chunk]
        qc = q_ref[si, c0:c0 + chunk, :]
        kc = k_ref[si, c0:c0 + chunk, :]
        b_last = bc[chunk - 1:chunk]
        qdb = qc * jnp.exp(bc)
        kdl = kc * jnp.exp(b_last - bc)
        decay = jnp.exp(_dot01_tn(jnp.where(rows == chunk - 1, bc, 0.0), ones_dv))
        kds = []
        for sb in range(nsub):
            rr = sb * sub
            ref_row = bc[rr - 1:rr] if sb else jnp.zeros((1, GQK), F32)
            vis = rows < rr + sub
            kds.append((ref_row, jnp.where(vis, kc * jnp.exp(jnp.where(vis, ref_row - bc, 0.0)), 0.0).astype(BF16)))
        vc = v_ref[si, c0:c0 + chunk, :].astype(BF16)
        vbd = jnp.concatenate([jnp.where(vhead == h, vc, jnp.zeros_like(vc)) for h in range(GLA_HEADS)], axis=0)
        o_sub = []
        for sb in range(nsub):
            rr = sb * sub
            ref_row, kd = kds[sb]
            qd = qc[rr:rr + sub] * jnp.exp(bc[rr:rr + sub] - ref_row)
            qst = jnp.concatenate([jnp.where(head_of_lane == h, qd, 0.0) for h in range(GLA_HEADS)], axis=0)
            a = _dot_nt(qst.astype(BF16), kd)
            qrow = _iota((GLA_HEADS * sub, chunk), 0) & (sub - 1)
            a = jnp.where(_iota((GLA_HEADS * sub, chunk), 1) <= rr + qrow, a, 0.0)
            a_cat = jnp.concatenate([a[h * sub:(h + 1) * sub] for h in range(GLA_HEADS)],
                                    axis=1).astype(BF16)
            o_sub.append(a_cat)
        o_intra = _dot(o_sub[0] if nsub == 1 else jnp.concatenate(o_sub, axis=0), vbd)
        qdb_st = jnp.concatenate([jnp.where(head_of_lane == h, qdb, 0.0) for h in range(GLA_HEADS)],
                                 axis=0).astype(BF16)
        kv = _dot_tn(kdl.astype(BF16), vc)
        upd = jnp.concatenate([kv[h * GLA_DK:(h + 1) * GLA_DK, h * GLA_DV:(h + 1) * GLA_DV]
                               for h in range(GLA_HEADS)], axis=0)
        pre[si, c] = (decay, upd, o_intra, qdb_st)

    for si in range(ns):
        state = s_sc[si]
        for c in range(nchunk):
            c0 = c * chunk
            decay, upd, o_intra, qdb_st = pre[si, c]
            o_inter = _dot(qdb_st, state.astype(BF16))
            for h in range(GLA_HEADS):
                hs = slice(h * GLA_DV, (h + 1) * GLA_DV)
                o = o_inter[h * chunk:(h + 1) * chunk] + o_intra[:, hs]
                o_ref[si, c0:c0 + chunk, hs] = (_rms(o, gn) * r_ref[si, c0:c0 + chunk, hs].astype(F32)).astype(o_ref.dtype)
            state = decay * state + upd
        s_sc[si] = state

    @pl.when(j == pl.num_programs(1) - 1)
    def _():
        sfin_ref[...] = s_sc[...]


def _gla(q, k, v, lg, r, gn, s0, ns, tb, chunk):
    s, t, _ = q.shape
    sub = min(GLA_SUB, chunk)
    row = lambda w: pl.BlockSpec((ns, tb, w), lambda gi, j: (gi, j, 0))
    st = pl.BlockSpec((ns, GQK, GLA_DV), lambda gi, j: (gi, 0, 0))
    return pl.pallas_call(
        functools.partial(_gla_body, ns=ns, tb=tb, chunk=chunk, sub=sub),
        grid=(s // ns, t // tb),
        in_specs=[row(GQK), row(GQK), row(GV), row(GQK), row(GV), pl.BlockSpec(gn.shape, lambda gi, j: (0, 0)), st],
        out_specs=[row(GV), st],
        out_shape=[jax.ShapeDtypeStruct((s, t, GV), BF16), jax.ShapeDtypeStruct((s, GQK, GLA_DV), F32)],
        scratch_shapes=[pltpu.VMEM((ns, GQK, GLA_DV), F32)],
        compiler_params=_params("parallel", "arbitrary"),
        name="gla",
    )(q, k, v, lg, r, gn, s0)


def _mix_body(x_ref, sc_ref, sh_ref, ga_ref, gpre_ref, gpost_ref, on_ref, og_ref, wm_ref, wn_ref, wgo_ref, wo_ref,
              o_ref):
    x = x_ref[...]
    d = x.shape[1]
    h = _norm_mod(x, gpre_ref[...], sc_ref[...], sh_ref[...]).astype(BF16)
    m = _sigmoid(_dot(h, wm_ref[...]))
    y_a = _dot(on_ref[...], wn_ref[...])
    y_b = _dot(og_ref[...], wgo_ref[...])
    mixin = (m[:, 0:d] * y_a + m[:, d:2 * d] * y_b).astype(BF16)
    mix = _dot(mixin, wo_ref[...])
    o_ref[...] = x + ga_ref[...] * _rms(mix, gpost_ref[...])


def _mix(x, sc, sh, ga, gpre, gpost, o_nsa, o_gla, wm, wn, wgo, wo, tm, rows_per_mod):
    n, d = x.shape
    row = lambda w: pl.BlockSpec((tm, w), lambda i: (i, 0))
    full = lambda a: pl.BlockSpec(a.shape, lambda i: (0, 0))
    ms = lambda a: _mod_spec(a, tm, rows_per_mod)
    return pl.pallas_call(
        _mix_body,
        grid=(n // tm,),
        in_specs=[row(d), ms(sc), ms(sh), ms(ga), full(gpre), full(gpost), row(NQ), row(GV),
                  full(wm), full(wn), full(wgo), full(wo)],
        out_specs=row(d),
        out_shape=jax.ShapeDtypeStruct((n, d), F32),
        compiler_params=_params("parallel"),
        name="mix",
    )(x, sc, sh, ga, gpre, gpost, o_nsa, o_gla, wm, wn, wgo, wo)


def _ffn_body(x_ref, sc_ref, sh_ref, ga_ref, gpre_ref, gpost_ref, wg_ref, wu_ref, wd_ref, o_ref, h_sc, acc_sc):
    j = pl.program_id(1)

    @pl.when(j == 0)
    def _():
        h_sc[...] = _norm_mod(x_ref[...], gpre_ref[...], sc_ref[...], sh_ref[...]).astype(BF16)
        acc_sc[...] = jnp.zeros(acc_sc.shape, F32)

    h = h_sc[...]
    gt = _dot(h, wg_ref[...])
    up = _dot(h, wu_ref[...])
    a = (gt * _sigmoid(gt) * up).astype(BF16)
    acc_sc[...] += _dot(a, wd_ref[...])

    @pl.when(j == pl.num_programs(1) - 1)
    def _():
        o_ref[...] = x_ref[...] + ga_ref[...] * _rms(acc_sc[...], gpost_ref[...])


def _ffn(x, sc, sh, ga, gpre, gpost, wg, wu, wd, tm, rows_per_mod):
    n, d = x.shape
    dff = wg.shape[1]
    tf = dff // 2 if (dff // 2) % LANES == 0 else dff
    row = pl.BlockSpec((tm, d), lambda i, j: (i, 0))
    full = lambda a: pl.BlockSpec(a.shape, lambda i, j: (0, 0))
    ms = lambda a: pl.BlockSpec((None, a.shape[1], a.shape[2]), lambda i, j: ((i * tm) // rows_per_mod, 0, 0))
    return pl.pallas_call(
        _ffn_body,
        grid=(n // tm, dff // tf),
        in_specs=[row, ms(sc), ms(sh), ms(ga), full(gpre), full(gpost),
                  pl.BlockSpec((d, tf), lambda i, j: (0, j)), pl.BlockSpec((d, tf), lambda i, j: (0, j)),
                  pl.BlockSpec((tf, d), lambda i, j: (j, 0))],
        out_specs=row,
        out_shape=jax.ShapeDtypeStruct((n, d), F32),
        scratch_shapes=[pltpu.VMEM((tm, d), BF16), pltpu.VMEM((tm, d), F32)],
        compiler_params=_params("parallel", "arbitrary"),
        name="ffn",
    )(x, sc, sh, ga, gpre, gpost, wg, wu, wd)


def _split_w_in(w_in, d):
    sizes = (NQ, KV_W, KV_W, KV_W, 3 * NSA_HEADS, GQK, GQK, GV, 16, GV, 2 * d)
    out, o = [], 0
    for s in sizes:
        out.append(w_in[:, o:o + s])
        o += s
    return out


def _pad_cols(w, n):
    return jnp.pad(w, ((0, 0), (0, n - w.shape[1])))


def _compress_weights(w1k, b1k, w2k, pek, w1v, b1v, w2v, pev):
    hid = w2k.shape[0]
    npair = CMP_STRIDE // 2
    eye = np.eye(NSA_GROUPS, dtype=np.float32)

    def group_diag(w):
        out = jnp.einsum('gh,...ab->...gahb', eye, w)
        return out.reshape(w.shape[:-2] + (NSA_GROUPS * w.shape[-2], NSA_GROUPS * w.shape[-1]))

    def expand_w1(half):
        per_kv = [group_diag(w1.reshape(CMP_LEN, HEAD_DIM, hid)[half * CMP_STRIDE:(half + 1) * CMP_STRIDE])
                  for w1 in (w1k, w1v)]
        return jnp.stack(per_kv).reshape(2, npair, 2 * NSA_GROUPS * HEAD_DIM, NSA_GROUPS * hid)

    w1 = jnp.concatenate([expand_w1(0), expand_w1(1)], axis=-1).astype(BF16)

    def expand_pe(half):
        per_kv = [jnp.tile(pe[half * CMP_STRIDE:(half + 1) * CMP_STRIDE], (1, NSA_GROUPS)) for pe in (pek, pev)]
        return jnp.stack(per_kv).reshape(2, npair, 2 * NSA_GROUPS * HEAD_DIM)

    pe = jnp.pad(jnp.stack([expand_pe(0), expand_pe(1)], axis=2),
                 ((0, 0), (0, 0), (0, PE_ROWS - 2), (0, 0))).astype(BF16)

    b1 = jnp.concatenate([jnp.tile(b1k, NSA_GROUPS), jnp.tile(b1v, NSA_GROUPS)]).reshape(1, -1)
    w2d = [group_diag(w2k), group_diag(w2v)]
    zero = jnp.zeros_like(w2d[0])
    w2 = jnp.concatenate([jnp.concatenate([w2d[0], zero], axis=1),
                          jnp.concatenate([zero, w2d[1]], axis=1)], axis=0).astype(BF16)
    return pe, w1, b1, w2


def _overlap(n_cmp_pad, n_cmp, nb):
    n = np.arange(n_cmp_pad)[:, None]
    j = np.arange(nb)[None, :] * SLC_BLOCK
    ok = (n * CMP_STRIDE < j + SLC_BLOCK) & (n * CMP_STRIDE + CMP_LEN > j) & (n < n_cmp)
    return ok.astype(BF16)


def _block_onehot(nb, nkeys):
    return (np.arange(nb)[:, None] == (np.arange(nkeys)[None, :] // SLC_BLOCK)).astype(BF16)


def _tile_rows(n, cap):
    t = cap
    while n % t:
        t //= 2
    return t


def _feature_major(a):
    nd = a.ndim
    a = jnp.moveaxis(a, nd - 4, nd - 1)
    return a.reshape(a.shape[:nd - 4] + (KV_W, a.shape[-1]))


def _row_major(a_t):
    b, _, r = a_t.shape
    return jnp.moveaxis(a_t.reshape(b, 2, NSA_GROUPS, HEAD_DIM, r), 4, 1)


def kernel(x_prompt, x_sample, cache_cmp_kv, cache_slc_kv, state_win_kv, state_gla, page_table, c_prompt, c_sample, w_ada, b_ada, g_pre_mix, g_post_mix, w_in, cmp_w1_k, cmp_b1_k, cmp_w2_k, cmp_pe_k, cmp_w1_v, cmp_b1_v, cmp_w2_v, cmp_pe_v, w_gla_a2, b_gla_a, gla_norm_g, w_nsa_o, w_gla_o, w_out, g_pre_ffn, g_post_ffn, w_ff_gate, w_ff_up, w_ff_down):
    depth = w_ada.shape[0]
    bp, l, d = x_prompt.shape
    bd, s_new, _ = x_sample.shape
    n_pages = page_table.shape[1]
    p_len = n_pages * PAGE_SIZE
    w_buf = state_win_kv.shape[2]
    keep = min(WINDOW, l)
    np_, ns_ = bp * l, bd * s_new
    tmp = _tile_rows(np_ // bp, 512)
    tms = _tile_rows(ns_, 512)
    tq = _tile_rows(l, 256)
    tk = 2 * tq

    yp, ys = x_prompt.reshape(np_, d), x_sample.reshape(ns_, d)
    col = [[] for _ in range(8)]
    for li in range(depth):
        r_all = bp + bd
        r_pad = -(-r_all // 8) * 8
        c_all = jnp.pad(jnp.concatenate([c_prompt, c_sample], axis=0), ((0, r_pad - r_all), (0, 0)))
        mod = _ada(c_all, w_ada[li], b_ada[li])
        mods_p = [m.reshape(bp, 1, d) for m in jnp.split(mod[:bp], 6, axis=-1)]
        mods_s = [jnp.repeat(m, s_new, axis=0).reshape(ns_ // tms, tms, d) for m in jnp.split(mod[bp:r_all], 6, axis=-1)]

        (w_q, w_kc, w_ks, w_kw, w_gn, w_qg, w_kg, w_vg, w_ag, w_rg, w_mg) = _split_w_in(w_in[li], d)
        wq = w_q.astype(BF16)
        wkv = jnp.concatenate([w_kc, w_ks, w_kw], axis=1).astype(BF16)
        wgn = _pad_cols(w_gn, LANES).astype(BF16)
        wqkv = jnp.concatenate([w_qg, w_kg, w_vg], axis=1).astype(BF16)
        wa = _pad_cols(w_ag, LANES).astype(BF16)
        wr = w_rg.astype(BF16)
        a2 = jnp.pad(w_gla_a2[li], ((0, LANES - w_gla_a2.shape[1]), (0, 0)))
        ba = b_gla_a[li].reshape(1, GQK)
        cw = _compress_weights(cmp_w1_k[li], cmp_b1_k[li], cmp_w2_k[li], cmp_pe_k[li],
                               cmp_w1_v[li], cmp_b1_v[li], cmp_w2_v[li], cmp_pe_v[li])
        gpre = g_pre_mix[li].reshape(1, d)
        gpost = g_post_mix[li].reshape(1, d)
        gpre2 = g_pre_ffn[li].reshape(1, d)
        gpost2 = g_post_ffn[li].reshape(1, d)
        gn = gla_norm_g[li].reshape(1, GLA_DV)
        wm = w_mg.astype(BF16)
        wn = w_nsa_o[li].astype(BF16)
        wgo = w_gla_o[li].astype(BF16)
        wo = w_out[li].astype(BF16)
        wfg = w_ff_gate[li].astype(BF16)
        wfu = w_ff_up[li].astype(BF16)
        wfd = w_ff_down[li].astype(BF16)

        sh1, sc1, ga1, sh2, sc2, ga2 = mods_p
        (q, kvc, _, _, kvs_b, kvw_b, gates, kvc_t, kvs_t, kvw_t) = _pre_nsa(yp, sc1, sh1, gpre, wq, wkv, wgn, tmp, l,
                                                                            feature_major=(bp, l))
        qg, kg, vg, lg, rg = _pre_gla(yp, sc1, sh1, gpre, wqkv, wa, wr, a2, ba, tmp, l)
        ckv = _cmp_prompt(kvc.reshape(bp, l, KV_W), cw)
        nch = l // CMP_STRIDE
        nb = l // SLC_BLOCK
        o_nsa = _nsa_prompt(q.reshape(bp, l, NQ), gates.reshape(bp, l, LANES), ckv, kvs_b.reshape(bp, l, KV_W),
                            kvw_b.reshape(bp, l, KV_W), _overlap(nch, nch - 1, nb), _block_onehot(nb, l).T, tq, tk)
        chunk = GLA_CHUNK if l % GLA_CHUNK == 0 else l
        tb = 2 * LANES if (l % (2 * LANES) == 0 and LANES % chunk == 0) else chunk
        o_gla, s_fin_p = _gla(qg.reshape(bp, l, GQK), kg.reshape(bp, l, GQK), vg.reshape(bp, l, GV),
                              lg.reshape(bp, l, GQK), rg.reshape(bp, l, GV), gn,
                              jnp.zeros((bp, GQK, GLA_DV), F32), bp, tb, chunk)
        x1 = _mix(yp, sc1, sh1, ga1, gpre, gpost, o_nsa.reshape(np_, NQ), o_gla.reshape(np_, GV), wm, wn, wgo, wo, tmp, l)
        yp = _ffn(x1, sc2, sh2, ga2, gpre2, gpost2, wfg, wfu, wfd, tmp, l)
        col[0].append(_row_major(kvc_t))
        col[2].append(_row_major(kvs_t))
        col[4].append(_row_major(kvw_t[:, :, l - keep:]))
        col[6].append(s_fin_p.reshape(bp, GLA_HEADS, GLA_DK, GLA_DV).astype(state_gla.dtype))

        sh1, sc1, ga1, sh2, sc2, ga2 = mods_s
        q, kvc, kvs, kvw, _, _, gates = _pre_nsa(ys, sc1, sh1, gpre, wq, wkv, wgn, tms, tms)
        qg, kg, vg, lg, rg = _pre_gla(ys, sc1, sh1, gpre, wqkv, wa, wr, a2, ba, tms, tms)
        ckv = _cmp_sample(_feature_major(cache_cmp_kv[li]), page_table, cw)
        nch = p_len // CMP_STRIDE
        nb = p_len // SLC_BLOCK
        win_t = _feature_major(state_win_kv[li])
        kvw3 = kvw.reshape(bd, s_new, KV_W)
        o_nsa = _nsa_sample(q.reshape(bd, s_new, NQ), gates.reshape(bd, s_new, LANES), ckv,
                            _feature_major(cache_slc_kv[li]), page_table, win_t, kvs.reshape(bd, s_new, KV_W), kvw3,
                            _overlap(nch, nch - 1, nb), _block_onehot(nb, p_len))
        chunk = GLA_CHUNK if s_new % GLA_CHUNK == 0 else s_new
        o_gla, s_fin_s = _gla(qg.reshape(bd, s_new, GQK), kg.reshape(bd, s_new, GQK), vg.reshape(bd, s_new, GV),
                              lg.reshape(bd, s_new, GQK), rg.reshape(bd, s_new, GV), gn,
                              state_gla[li].astype(F32).reshape(bd, GQK, GLA_DV), _tile_rows(bd, 16), chunk, chunk)
        x1 = _mix(ys, sc1, sh1, ga1, gpre, gpost, o_nsa.reshape(ns_, NQ), o_gla.reshape(ns_, GV), wm, wn, wgo, wo, tms, tms)
        ys = _ffn(x1, sc2, sh2, ga2, gpre2, gpost2, wfg, wfu, wfd, tms, tms)
        win_new_t = jnp.concatenate([win_t, kvw3.transpose(0, 2, 1)], axis=2)[:, :, s_new:]
        win_new = jnp.moveaxis(win_new_t.reshape(bd, 2, NSA_GROUPS, HEAD_DIM, w_buf), 4, 1)
        col[1].append(kvc.reshape(bd, s_new, 2, NSA_GROUPS, HEAD_DIM))
        col[3].append(kvs.reshape(bd, s_new, 2, NSA_GROUPS, HEAD_DIM))
        col[5].append(win_new)
        col[7].append(s_fin_s.reshape(bd, GLA_HEADS, GLA_DK, GLA_DV).astype(state_gla.dtype))

    stacked = [c[0][None] if len(c) == 1 else jnp.stack(c) for c in col]
    return (yp.reshape(bp, l, d), ys.reshape(bd, s_new, d), *stacked)
```

```python
import functools
import math

import numpy as np
import jax
import jax.numpy as jnp
from jax import lax
from jax.experimental import pallas as pl
from jax.experimental.pallas import tpu as pltpu

F32 = jnp.float32
BF16 = jnp.bfloat16

NSA_HEADS = 8
NSA_GROUPS = 2
NSA_HPG = NSA_HEADS // NSA_GROUPS
HEAD_DIM = 64
CMP_STRIDE = 16
CMP_LEN = 32
SLC_BLOCK = 64
N_SEL = 16
WINDOW = 512
GLA_HEADS = 4
GLA_DK = 64
GLA_DV = 128
GLA_TAU = 16.0
GLA_CHUNK = 64
GLA_SUB = 16
PE_ROWS = 16
SAMPLE_SEQS_PER_STEP = 2
EPS = 1e-6
PAGE_SIZE = 128
NEG = -1e30
LOG2E = math.log2(math.e)
LANES = 128
VMEM_LIMIT = 56 * 1024 * 1024

KV_W = 2 * NSA_GROUPS * HEAD_DIM
NQ = NSA_HEADS * HEAD_DIM
GQK = GLA_HEADS * GLA_DK
GV = GLA_HEADS * GLA_DV


def _dot(a, b):
    return jnp.dot(a, b, preferred_element_type=F32)


def _dot_nt(a, b):
    return lax.dot_general(a, b, (((1,), (1,)), ((), ())), preferred_element_type=F32)


def _dot_f32(a, b):
    return jnp.dot(a, b, preferred_element_type=F32, precision=lax.Precision.HIGHEST)


def _dot01(x, m01):
    hi = x.astype(BF16)
    r1 = x - hi.astype(F32)
    mid = r1.astype(BF16)
    lo = (r1 - mid.astype(F32)).astype(BF16)
    return _dot(hi, m01) + _dot(mid, m01) + _dot(lo, m01)


def _dot01_l(m01, x):
    hi = x.astype(BF16)
    r1 = x - hi.astype(F32)
    mid = r1.astype(BF16)
    lo = (r1 - mid.astype(F32)).astype(BF16)
    return _dot(m01, hi) + _dot(m01, mid) + _dot(m01, lo)


def _dot_tn(a, b):
    return lax.dot_general(a, b, (((0,), (0,)), ((), ())), preferred_element_type=F32)


def _dot01_tn(x, m01):
    hi = x.astype(BF16)
    r1 = x - hi.astype(F32)
    mid = r1.astype(BF16)
    lo = (r1 - mid.astype(F32)).astype(BF16)
    return _dot_tn(hi, m01) + _dot_tn(mid, m01) + _dot_tn(lo, m01)


def _sigmoid(x):
    return 1.0 / (1.0 + jnp.exp(-x))


def _params(*sem):
    return pltpu.CompilerParams(dimension_semantics=sem, vmem_limit_bytes=VMEM_LIMIT)


def _iota(shape, dim):
    return lax.broadcasted_iota(jnp.int32, shape, dim)


def _div_pow2(x, n):
    assert n & (n - 1) == 0, n
    return x >> (n.bit_length() - 1)


def _ada_body(c_ref, w_ref, b_ref, o_ref):
    o_ref[...] = _dot_f32(c_ref[...], w_ref[...]) + b_ref[...]


def _ada(c_all, w_ada, b_ada):
    r, d = c_all.shape
    n = w_ada.shape[1]
    tn = 1536 if n % 1536 == 0 else n
    return pl.pallas_call(
        _ada_body,
        grid=(n // tn,),
        in_specs=[pl.BlockSpec((r, d), lambda j: (0, 0)),
                  pl.BlockSpec((d, tn), lambda j: (0, j)),
                  pl.BlockSpec((1, tn), lambda j: (0, j))],
        out_specs=pl.BlockSpec((r, tn), lambda j: (0, j)),
        out_shape=jax.ShapeDtypeStruct((r, n), F32),
        compiler_params=_params("arbitrary"),
        name="ada",
    )(c_all, w_ada, b_ada.reshape(1, n))


def _norm_mod(x, g, sc, sh):
    ms = jnp.mean(x * x, axis=-1, keepdims=True)
    y = x * lax.rsqrt(ms + EPS) * g
    return y * (1.0 + sc) + sh


def _rms(x, g):
    ms = jnp.mean(x * x, axis=-1, keepdims=True)
    return x * lax.rsqrt(ms + EPS) * g


def _pre_nsa_body(x_ref, sc_ref, sh_ref, g_ref, wq_ref, wkv_ref, wg_ref,
                  q_ref, kvc_ref, kvs_ref, kvw_ref, kvsb_ref, kvwb_ref, gate_ref, *t_refs):
    h = _norm_mod(x_ref[...], g_ref[...], sc_ref[...], sh_ref[...]).astype(BF16)
    q_ref[...] = (_dot(h, wq_ref[...]) * (HEAD_DIM ** -0.5 * LOG2E)).astype(BF16)
    kv = _dot(h, wkv_ref[...])
    kvc = kv[:, 0:KV_W]
    kvs = kv[:, KV_W:2 * KV_W]
    kvw = kv[:, 2 * KV_W:3 * KV_W]
    kvc_ref[...] = kvc
    kvs_ref[...] = kvs
    kvw_ref[...] = kvw
    kvsb_ref[...] = kvs.astype(BF16)
    kvwb_ref[...] = kvw.astype(BF16)
    gate_ref[...] = _sigmoid(_dot(h, wg_ref[...]))
    if t_refs:
        kvt = kv.T
        for j, t_ref in enumerate(t_refs):
            t_ref[...] = kvt[j * KV_W:(j + 1) * KV_W]


def _mod_spec(mod3, tm, rows_per_mod):
    r = mod3.shape[1]
    return pl.BlockSpec((None, r, mod3.shape[2]), lambda i: ((i * tm) // rows_per_mod, 0, 0))


def _pre_nsa(x, sc, sh, g, wq, wkv, wg, tm, rows_per_mod, feature_major=None):
    n, d = x.shape
    row = lambda w: pl.BlockSpec((tm, w), lambda i: (i, 0))
    full = lambda a: pl.BlockSpec(a.shape, lambda i: (0, 0))
    outs = [(NQ, BF16), (KV_W, F32), (KV_W, F32), (KV_W, F32), (KV_W, BF16), (KV_W, BF16), (LANES, F32)]
    out_specs = [row(w) for w, _ in outs]
    out_shape = [jax.ShapeDtypeStruct((n, w), t) for w, t in outs]
    if feature_major is not None:
        nbatch, per = feature_major
        tiles = per // tm
        out_specs += [pl.BlockSpec((None, KV_W, tm), lambda i: (i // tiles, 0, i % tiles))] * 3
        out_shape += [jax.ShapeDtypeStruct((nbatch, KV_W, per), F32)] * 3
    return pl.pallas_call(
        _pre_nsa_body,
        grid=(n // tm,),
        in_specs=[row(d), _mod_spec(sc, tm, rows_per_mod), _mod_spec(sh, tm, rows_per_mod), full(g),
                  full(wq), full(wkv), full(wg)],
        out_specs=out_specs,
        out_shape=out_shape,
        compiler_params=_params("parallel"),
        name="pre_nsa",
    )(x, sc, sh, g, wq, wkv, wg)


def _log_sigmoid(x):
    return jnp.minimum(x, 0.0) - jnp.log(1.0 + jnp.exp(-jnp.abs(x)))


def _pre_gla_body(x_ref, sc_ref, sh_ref, g_ref, wqkv_ref, wa_ref, wr_ref, a2_ref, ba_ref,
                  q_ref, k_ref, v_ref, lg_ref, r_ref):
    h = _norm_mod(x_ref[...], g_ref[...], sc_ref[...], sh_ref[...]).astype(BF16)
    qkv = _dot(h, wqkv_ref[...])
    q_ref[...] = qkv[:, 0:GQK] * GLA_DK ** -0.5
    k_ref[...] = qkv[:, GQK:2 * GQK]
    v_ref[...] = qkv[:, 2 * GQK:2 * GQK + GV]
    a = _dot(h, wa_ref[...])
    lg_ref[...] = _log_sigmoid(_dot_f32(a, a2_ref[...]) + ba_ref[...]) * (1.0 / GLA_TAU)
    r = _dot(h, wr_ref[...])
    r_ref[...] = (r * _sigmoid(r)).astype(BF16)


def _pre_gla(x, sc, sh, g, wqkv, wa, wr, a2, ba, tm, rows_per_mod):
    n, d = x.shape
    row = lambda w: pl.BlockSpec((tm, w), lambda i: (i, 0))
    full = lambda a: pl.BlockSpec(a.shape, lambda i: (0, 0))
    return pl.pallas_call(
        _pre_gla_body,
        grid=(n // tm,),
        in_specs=[row(d), _mod_spec(sc, tm, rows_per_mod), _mod_spec(sh, tm, rows_per_mod), full(g),
                  full(wqkv), full(wa), full(wr), full(a2), full(ba)],
        out_specs=[row(GQK), row(GQK), row(GV), row(GQK), row(GV)],
        out_shape=[jax.ShapeDtypeStruct((n, GQK), F32), jax.ShapeDtypeStruct((n, GQK), F32),
                   jax.ShapeDtypeStruct((n, GV), F32), jax.ShapeDtypeStruct((n, GQK), F32),
                   jax.ShapeDtypeStruct((n, GV), BF16)],
        compiler_params=_params("parallel"),
        name="pre_gla",
    )(x, sc, sh, g, wqkv, wa, wr, a2, ba)


def _gelu_tanh(x):
    return 0.5 * x * (1.0 + jnp.tanh(0.7978845608028654 * (x + 0.044715 * x * x * x)))


def _compress(get_xp, nch, pe_ref, w1_ref, b1_ref, w2_ref):
    first, second = [], []
    for kv in range(2):
        acc = jnp.zeros((nch + PE_ROWS, 2 * LANES), F32)
        for pp in range(CMP_STRIDE // 2):
            lhs = jnp.concatenate([get_xp(kv, 2 * pp), get_xp(kv, 2 * pp + 1)], axis=1).astype(BF16)
            acc = acc + _dot(jnp.concatenate([lhs, pe_ref[kv, pp]], axis=0), w1_ref[kv, pp])
        first.append(acc[0:nch, 0:LANES] + acc[nch:nch + 1, 0:LANES])
        second.append(acc[0:nch, LANES:2 * LANES] + acc[nch + 1:nch + 2, LANES:2 * LANES])
    first = jnp.concatenate(first, axis=1)
    second = jnp.concatenate(second, axis=1)
    if nch % 8 == 0:
        nxt = pltpu.roll(second, nch - 1, 0)
    else:
        nxt = jnp.concatenate([second[1:], second[:1]], axis=0)
    hid = _gelu_tanh(first + nxt + b1_ref[...])
    out = _dot(hid.astype(BF16), w2_ref[...])
    return jnp.where(_iota(out.shape, 0) < nch - 1, out, 0.0)


def _cmp_prompt_body(xk_ref, xv_ref, pe_ref, w1_ref, b1_ref, w2_ref, o_ref, *, nch):
    x_refs = (xk_ref, xv_ref)

    def get_xp(kv, p):
        return x_refs[kv][pl.ds(p, nch, stride=CMP_STRIDE), :]

    o_ref[...] = _compress(get_xp, nch, pe_ref, w1_ref, b1_ref, w2_ref).astype(o_ref.dtype)


def _cmp_prompt(kvc, cw):
    b, l, _ = kvc.shape
    nch = l // CMP_STRIDE
    full = lambda a: pl.BlockSpec(a.shape, lambda i: (0,) * a.ndim)
    return pl.pallas_call(
        functools.partial(_cmp_prompt_body, nch=nch),
        grid=(b,),
        in_specs=[pl.BlockSpec((None, l, LANES), lambda i: (i, 0, 0)),
                  pl.BlockSpec((None, l, LANES), lambda i: (i, 0, 1))] + [full(a) for a in cw],
        out_specs=pl.BlockSpec((None, nch, KV_W), lambda i: (i, 0, 0)),
        out_shape=jax.ShapeDtypeStruct((b, nch, KV_W), BF16),
        compiler_params=_params("parallel"),
        name="cmp_prompt",
    )(kvc, kvc, *cw)


def _cmp_sample_body(*refs, n_pages, nch):
    page_refs = refs[1:1 + n_pages]
    pe_ref, w1_ref, b1_ref, w2_ref, o_ref, ak_sc, av_sc, bk_sc, bv_sc = refs[1 + n_pages:]
    i = pl.program_id(0)
    cpp = PAGE_SIZE // CMP_STRIDE

    def regroup(dst):
        for k in range(n_pages):
            for kv in range(2):
                xt = page_refs[k][kv * LANES:(kv + 1) * LANES, :].T
                for n in range(cpp):
                    for a in range(CMP_STRIDE // 8):
                        r0 = n * CMP_STRIDE + 8 * a
                        dst[kv][k, pl.ds(8 * a * cpp + n, 8, stride=cpp), :] = xt[r0:r0 + 8, :]

    def mlp(src):
        def get_xp(kv, p):
            return src[kv][:, p * cpp:(p + 1) * cpp, :].reshape(nch, LANES)

        o_ref[...] = _compress(get_xp, nch, pe_ref, w1_ref, b1_ref, w2_ref).astype(o_ref.dtype)

    @pl.when(i == 0)
    def _():
        bk_sc[...] = jnp.zeros(bk_sc.shape, F32)
        bv_sc[...] = jnp.zeros(bv_sc.shape, F32)

    @pl.when(i % 2 == 0)
    def _():
        regroup((ak_sc, av_sc))
        mlp((bk_sc, bv_sc))

    @pl.when(i % 2 == 1)
    def _():
        regroup((bk_sc, bv_sc))
        mlp((ak_sc, av_sc))


def _cmp_sample(cache_t, page_table, cw):
    bd, n_pages = page_table.shape
    p_len = n_pages * PAGE_SIZE
    nch = p_len // CMP_STRIDE
    full = lambda a: pl.BlockSpec(a.shape, lambda i, pt: (0,) * a.ndim)
    page = lambda k: pl.BlockSpec((None, KV_W, PAGE_SIZE),
                                  lambda i, pt, k=k: (pt[jnp.minimum(i, bd - 1), k], 0, 0))
    return pl.pallas_call(
        functools.partial(_cmp_sample_body, n_pages=n_pages, nch=nch),
        grid_spec=pltpu.PrefetchScalarGridSpec(
            num_scalar_prefetch=1,
            grid=(bd + 1,),
            in_specs=[page(k) for k in range(n_pages)] + [full(a) for a in cw],
            out_specs=pl.BlockSpec((None, nch, KV_W), lambda i, pt: (jnp.maximum(i - 1, 0), 0, 0)),
            scratch_shapes=[pltpu.VMEM((n_pages, PAGE_SIZE, LANES), F32)] * 4,
        ),
        out_shape=jax.ShapeDtypeStruct((bd, nch, KV_W), BF16),
        compiler_params=_params("arbitrary"),
        name="cmp_sample",
    )(page_table, *([cache_t] * n_pages), *cw)


def _head_queries(q, rows):
    half = _iota((rows, LANES), 1) >> 6
    out = []
    for hh in range(NSA_HEADS):
        g = hh // NSA_HPG
        blk = q[:, (hh // 2) * LANES:(hh // 2 + 1) * LANES].astype(F32)
        if hh % 2 != g:
            blk = pltpu.roll(blk, HEAD_DIM, 1) if rows % 8 == 0 else jnp.roll(blk, HEAD_DIM, 1)
        out.append(jnp.where(half == g, blk, 0.0).astype(BF16))
    return out


def _topk_select(imp, cur):
    impt = imp.T
    nb = impt.shape[0]
    jl = _iota(impt.shape, 0)
    jf = jl.astype(F32)
    forced = (jl == 0) | (jl == cur) | (jl == cur - 1)
    elig = (jl >= 1) & (jl <= cur - 2)
    sel0 = jnp.where(forced, 1.0, 0.0)
    alive0 = jnp.where(elig, 1.0, 0.0)

    def step(_, carry):
        sel, alive = carry
        live = alive > 0.0
        vals = jnp.where(live, impt, -1.0)
        m = jnp.max(vals, axis=0, keepdims=True)
        cand = live & (vals == m)
        idx = jnp.min(jnp.where(cand, jf, float(nb)), axis=0, keepdims=True)
        one = jf == idx
        return jnp.where(one, 1.0, sel), jnp.where(one, 0.0, alive)

    sel, _ = lax.fori_loop(0, N_SEL - 3, step, (sel0, alive0))
    return sel.T


def _masked_softmax2(s, valid):
    s = jnp.where(valid, s, NEG)
    m = jnp.max(s, axis=-1, keepdims=True)
    e = jnp.where(valid, jnp.exp2(s - m), 0.0)
    d = jnp.sum(e, axis=-1, keepdims=True)
    return e * (1.0 / jnp.where(d > 0.0, d, 1.0))


def _nsa_prompt_body(q_ref, gate_ref, ckv_ref, kvs_ref, kvw_ref, ov_ref, et_ref, o_ref, qa_sc, m_sc, acc_sc, out_sc,
                     *, tq, tk, n_cmp):
    i = pl.program_id(1)
    t0 = i * tq
    nc = ckv_ref.shape[0]
    nb = ov_ref.shape[1]
    nh = NSA_HEADS
    qh = _head_queries(q_ref[...], tq)
    for hh in range(nh):
        qa_sc[hh * tq:(hh + 1) * tq, 0:LANES] = qh[hh]
    half = _iota((tq, LANES), 1) >> 6
    trow = t0 + _iota((1, tq, 1), 1)
    gates = gate_ref[...]

    def gate(branch, hh):
        c = branch * nh + hh
        return gates[:, c:c + 1]

    ck = ckv_ref[:, 0:LANES]
    cv = ckv_ref[:, LANES:2 * LANES]
    s = _dot_nt(qa_sc[:, 0:LANES], ck).reshape(nh, tq, nc)
    ncol = _iota((1, tq, nc), 2)
    cvalid = (ncol * CMP_STRIDE + CMP_LEN <= trow + 1) & (ncol < n_cmp)
    p = _masked_softmax2(s, cvalid)
    for hh in range(nh):
        out_sc[hh] = gate(0, hh) * _dot(p[hh].astype(BF16), cv)
    ov = ov_ref[...]
    imp = []
    for g in range(NSA_GROUPS):
        psum = p[g * NSA_HPG]
        for h in range(1, NSA_HPG):
            psum = psum + p[g * NSA_HPG + h]
        imp.append(_dot01(psum, ov))
    imp = jnp.concatenate(imp, axis=0)
    cur = (t0 + (_iota((1, NSA_GROUPS * tq), 1) & (tq - 1))) >> 6
    sel = _topk_select(imp, cur)
    selneg = jnp.where(sel > 0.0, 0.0, NEG).astype(BF16)
    for hh in range(nh):
        g = hh // NSA_HPG
        qa_sc[hh * tq:(hh + 1) * tq, LANES:LANES + nb] = selneg[g * tq:(g + 1) * tq]

    def load_v(ref, start, width):
        v128 = ref[pl.ds(start, width), LANES:2 * LANES]
        lane_half = _iota(v128.shape, 1) >> 6
        return [jnp.where(lane_half == g, v128, jnp.ones_like(v128)) for g in range(NSA_GROUPS)]

    def online(s3, vaug, first):
        width = s3.shape[-1]
        tmax = jnp.max(s3, axis=-1, keepdims=True)
        if first:
            m_new = jnp.broadcast_to(tmax, (nh, tq, LANES))
        else:
            m_old = m_sc[...]
            m_new = jnp.maximum(m_old, tmax)
            alpha = jnp.exp2(m_old - m_new)
        pexp = jnp.concatenate([jnp.exp2((s3[:, :, c * LANES:(c + 1) * LANES] - m_new).astype(BF16))
                                for c in range(width // LANES)], axis=-1)
        for hh in range(nh):
            pv = _dot(pexp[hh], vaug[hh // NSA_HPG])
            acc_sc[hh] = pv if first else alpha[hh] * acc_sc[hh] + pv
        m_sc[...] = m_new

    def finish(branch):
        for hh in range(nh):
            acc = acc_sc[hh]
            den = pltpu.roll(acc, HEAD_DIM, 1)
            out_sc[hh] = out_sc[hh] + gate(branch, hh) * (acc / den)

    def slc_scores(start):
        kaug = jnp.concatenate([kvs_ref[pl.ds(start, tk), 0:LANES], et_ref[pl.ds(start, tk), :]], axis=1)
        return _dot_nt(qa_sc[...], kaug).reshape(nh, tq, tk)

    n_full = i // (tk // tq)
    last = pl.multiple_of(n_full * tk, tk)
    s3 = slc_scores(last)
    s3 = jnp.where(last + _iota((1, tq, tk), 2) <= trow, s3, NEG)
    online(s3, load_v(kvs_ref, last, tk), True)

    def slc_step(j, carry):
        start = pl.multiple_of(j * tk, tk)
        online(slc_scores(start), load_v(kvs_ref, start, tk), False)
        return carry

    lax.fori_loop(0, n_full, slc_step, 0)
    finish(1)

    d0 = pl.multiple_of(t0, tq)
    s3 = _dot_nt(qa_sc[:, 0:LANES], kvw_ref[pl.ds(d0, tq), 0:LANES]).reshape(nh, tq, tq)
    s3 = jnp.where(_iota((1, tq, tq), 2) <= _iota((1, tq, tq), 1), s3, NEG)
    online(s3, load_v(kvw_ref, d0, tq), True)

    @pl.when(i >= 1)
    def _():
        start = pl.multiple_of(jnp.maximum(t0 - WINDOW, 0), tq)
        s3 = _dot_nt(qa_sc[:, 0:LANES], kvw_ref[pl.ds(start, WINDOW), 0:LANES]).reshape(nh, tq, WINDOW)
        kpos = start + _iota((1, tq, WINDOW), 2)
        s3 = jnp.where((kpos < t0) & (trow - kpos < WINDOW), s3, NEG)
        online(s3, load_v(kvw_ref, start, WINDOW), False)

    finish(2)

    for pair in range(nh // 2):
        parts = []
        for hh in (2 * pair, 2 * pair + 1):
            x = out_sc[hh]
            if hh % 2 != hh // NSA_HPG:
                x = pltpu.roll(x, HEAD_DIM, 1)
            parts.append(x)
        o_ref[:, pair * LANES:(pair + 1) * LANES] = jnp.where(half == 0, parts[0], parts[1]).astype(o_ref.dtype)


def _nsa_prompt(q, gates, ckv, kvs_b, kvw_b, ov, et, tq, tk):
    b, l, _ = q.shape
    nb = l // SLC_BLOCK
    assert l % tk == 0 and tk % tq == 0 and WINDOW % tq == 0 and l >= WINDOW
    blk = lambda w: pl.BlockSpec((None, tq, w), lambda bi, i: (bi, i, 0))
    seq = lambda a: pl.BlockSpec((None,) + a.shape[1:], lambda bi, i: (bi, 0, 0))
    full = lambda a: pl.BlockSpec(a.shape, lambda bi, i: (0, 0))
    hs = (NSA_HEADS, tq, LANES)
    return pl.pallas_call(
        functools.partial(_nsa_prompt_body, tq=tq, tk=tk, n_cmp=l // CMP_STRIDE - 1),
        grid=(b, l // tq),
        in_specs=[blk(NQ), blk(LANES), seq(ckv), seq(kvs_b), seq(kvw_b), full(ov), full(et)],
        out_specs=blk(NQ),
        out_shape=jax.ShapeDtypeStruct((b, l, NQ), BF16),
        scratch_shapes=[pltpu.VMEM((NSA_HEADS * tq, LANES + nb), BF16), pltpu.VMEM(hs, F32), pltpu.VMEM(hs, F32),
                        pltpu.VMEM(hs, F32)],
        compiler_params=_params("parallel", "arbitrary"),
        name="nsa_prompt",
    )(q, gates, ckv, kvs_b, kvw_b, ov, et)


def _nsa_sample_body(*refs, n_pages, nseq, s_new, n_cmp, p_len):
    page_refs = refs[1:1 + nseq * n_pages]
    (q_ref, gate_ref, ckv_ref, win_ref, ns_ref, nw_ref, ov_ref, e_ref, o_ref) = refs[1 + nseq * n_pages:]
    rows1 = NSA_HEADS * s_new
    rows = nseq * rows1
    nc = ckv_ref.shape[1]
    nb = ov_ref.shape[1]
    w_buf = win_ref.shape[2]
    seqs = range(nseq)
    of = lambda x, s: x[s * rows1:(s + 1) * rows1]
    cat = lambda xs: xs[0] if len(xs) == 1 else jnp.concatenate(xs, axis=0)
    q2 = [jnp.concatenate(_head_queries(q_ref[s], s_new), axis=0) for s in seqs]
    ridx = _iota((rows, 1), 0)
    trow = ridx & (s_new - 1)
    grow = _div_pow2(ridx, s_new * NSA_HPG) & (NSA_GROUPS - 1)
    lane_half = _iota((rows, LANES), 1) >> 6

    def gate_rows(branch):
        cols = [gate_ref[s][:, branch * NSA_HEADS + hh:branch * NSA_HEADS + hh + 1]
                for s in seqs for hh in range(NSA_HEADS)]
        return jnp.concatenate(cols, axis=0)

    def pick(acc, den):
        return jnp.where(lane_half == grow, acc, 0.0) / den

    def attend(s_old, s_nw, vt_old, new_rows):
        s_old = cat(s_old)
        s_nw = jnp.where(_iota((rows, s_new), 1) <= trow, cat(s_nw), NEG)
        m = jnp.maximum(jnp.max(s_old, axis=-1, keepdims=True), jnp.max(s_nw, axis=-1, keepdims=True))
        e1 = jnp.exp2(s_old - m)
        e2 = jnp.exp2(s_nw - m)
        den = jnp.sum(e1, axis=-1, keepdims=True) + jnp.sum(e2, axis=-1, keepdims=True)
        e1 = e1.astype(BF16)
        e2 = e2.astype(BF16)
        acc = cat([_dot_nt(vt_old[s], of(e1, s)).T + _dot(of(e2, s), new_rows[s][:, LANES:2 * LANES]) for s in seqs])
        return pick(acc, den)

    ncol = _iota((rows, nc), 1)
    cvalid = (ncol * CMP_STRIDE + CMP_LEN <= p_len + trow + 1) & (ncol < n_cmp)
    p_c = _masked_softmax2(cat([_dot_nt(q2[s], ckv_ref[s, :, 0:LANES]) for s in seqs]), cvalid)
    o_c = cat([_dot(of(p_c, s).astype(BF16), ckv_ref[s, :, LANES:2 * LANES]) for s in seqs])
    out = gate_rows(0) * jnp.where(lane_half == grow, o_c, 0.0)
    psum = []
    for s in seqs:
        for g in range(NSA_GROUPS):
            acc = jnp.zeros((s_new, nc), F32)
            for h in range(NSA_HPG):
                r0 = s * rows1 + (g * NSA_HPG + h) * s_new
                acc = acc + p_c[r0:r0 + s_new]
            psum.append(acc)
    n_imp = nseq * NSA_GROUPS * s_new
    psum.append(jnp.zeros((LANES - n_imp, nc), F32))
    imp = _dot01(jnp.concatenate(psum, axis=0), ov_ref[...])
    cur = jnp.full((1, LANES), p_len // SLC_BLOCK, jnp.int32)
    sel = _topk_select(imp, cur)[0:n_imp]
    selneg = jnp.where(sel > 0.0, 0.0, NEG).astype(BF16)
    selneg_rows = [jnp.concatenate([selneg[(s * NSA_GROUPS + g) * s_new:(s * NSA_GROUPS + g + 1) * s_new]
                                    for g in range(NSA_GROUPS) for _ in range(NSA_HPG)], axis=0)
                   for s in seqs]

    e_hot = e_ref[...]
    s_past, s_nw, vts, news = [], [], [], []
    for s in seqs:
        kvt = jnp.concatenate([r[...] for r in page_refs[s * n_pages:(s + 1) * n_pages]],
                              axis=1).astype(BF16)
        kaug = jnp.concatenate([kvt[0:LANES], e_hot], axis=0)
        s_past.append(_dot(jnp.concatenate([q2[s], selneg_rows[s]], axis=1), kaug))
        ns = ns_ref[s].astype(BF16)
        s_nw.append(_dot_nt(q2[s], ns[:, 0:LANES]))
        vts.append(kvt[LANES:2 * LANES])
        news.append(ns)
    out = out + gate_rows(1) * attend(s_past, s_nw, vts, news)

    scol = _iota((rows1, w_buf), 1)
    tr1 = _iota((rows1, 1), 0) & (s_new - 1)
    wmask = (w_buf + tr1 - scol < WINDOW) & (p_len - w_buf + scol >= 0)
    s_b, s_nw, vts, news = [], [], [], []
    for s in seqs:
        wt = win_ref[s].astype(BF16)
        nw = nw_ref[s].astype(BF16)
        s_b.append(jnp.where(wmask, _dot(q2[s], wt[0:LANES]), NEG))
        s_nw.append(_dot_nt(q2[s], nw[:, 0:LANES]))
        vts.append(wt[LANES:2 * LANES])
        news.append(nw)
    out = out + gate_rows(2) * attend(s_b, s_nw, vts, news)

    lh = _iota((s_new, LANES), 1) >> 6
    for s in seqs:
        for pair in range(NSA_HEADS // 2):
            parts = []
            for hh in (2 * pair, 2 * pair + 1):
                x = out[s * rows1 + hh * s_new:s * rows1 + (hh + 1) * s_new]
                if hh % 2 != hh // NSA_HPG:
                    x = jnp.concatenate([x[:, HEAD_DIM:], x[:, :HEAD_DIM]], axis=1)
                parts.append(x)
            o_ref[s, :, pair * LANES:(pair + 1) * LANES] = jnp.where(lh == 0, parts[0], parts[1]).astype(o_ref.dtype)


def _nsa_sample(q, gates, ckv, cache_t, page_table, win_t, new_s, new_w, ov, e_all):
    bd, s_new, _ = q.shape
    n_pages = page_table.shape[1]
    p_len = n_pages * PAGE_SIZE
    assert s_new & (s_new - 1) == 0
    nseq = _tile_rows(bd, SAMPLE_SEQS_PER_STEP)
    full = lambda a: pl.BlockSpec(a.shape, lambda i, pt: (0, 0))
    seq = lambda a: pl.BlockSpec((nseq,) + a.shape[1:], lambda i, pt: (i, 0, 0))
    page = lambda s, k: pl.BlockSpec((None, KV_W, PAGE_SIZE), lambda i, pt, s=s, k=k: (pt[i * nseq + s, k], 0, 0))
    return pl.pallas_call(
        functools.partial(_nsa_sample_body, n_pages=n_pages, nseq=nseq, s_new=s_new, n_cmp=p_len // CMP_STRIDE - 1,
                          p_len=p_len),
        grid_spec=pltpu.PrefetchScalarGridSpec(
            num_scalar_prefetch=1,
            grid=(bd // nseq,),
            in_specs=[page(s, k) for s in range(nseq) for k in range(n_pages)]
            + [seq(q), seq(gates), seq(ckv), seq(win_t), seq(new_s), seq(new_w), full(ov), full(e_all)],
            out_specs=pl.BlockSpec((nseq, s_new, NQ), lambda i, pt: (i, 0, 0)),
        ),
        out_shape=jax.ShapeDtypeStruct((bd, s_new, NQ), BF16),
        compiler_params=_params("parallel"),
        name="nsa_sample",
    )(page_table, *([cache_t] * (nseq * n_pages)), q, gates, ckv, win_t, new_s, new_w, ov, e_all)


def _gla_body(q_ref, k_ref, v_ref, lg_ref, r_ref, gn_ref, s0_ref, o_ref, sfin_ref, s_sc, *, ns, tb, chunk, sub):
    j = pl.program_id(1)

    @pl.when(j == 0)
    def _():
        s_sc[...] = s0_ref[...]

    nchunk = tb // chunk
    nsub = chunk // sub
    ri = _iota((tb, tb), 0)
    ci = _iota((tb, tb), 1)
    tril = jnp.where((_div_pow2(ri, chunk) == _div_pow2(ci, chunk)) & (ci <= ri), 1.0, 0.0).astype(BF16)
    head_of_lane = _div_pow2(_iota((1, GQK), 1), GLA_DK)
    rows = _iota((chunk, GQK), 0)
    ones_dv = jnp.ones((chunk, GLA_DV), BF16)
    vhead = _div_pow2(_iota((chunk, GV), 1), GLA_DV)
    gn = gn_ref[...]

    units = [(si, c) for si in range(ns) for c in range(nchunk)]
    b_seq = [_dot01_l(tril, lg_ref[si]) for si in range(ns)]
    pre = {}
    for si, c in units:
        c0 = c * chunk
        bc = b_seq[si][c0:c0 + chunk]
        qc = q_ref[si, c0:c0 + chunk, :]
        kc = k_ref[si, c0:c0 + chunk, :]
        b_last = bc[chunk - 1:chunk]
        qdb = qc * jnp.exp(bc)
        kdl = kc * jnp.exp(b_last - bc)
        decay = jnp.exp(_dot01_tn(jnp.where(rows == chunk - 1, bc, 0.0), ones_dv))
        kds = []
        for sb in range(nsub):
            rr = sb * sub
            ref_row = bc[rr - 1:rr] if sb else jnp.zeros((1, GQK), F32)
            vis = rows < rr + sub
            kds.append((ref_row, jnp.where(vis, kc * jnp.exp(jnp.where(vis, ref_row - bc, 0.0)), 0.0).astype(BF16)))
        vc = v_ref[si, c0:c0 + chunk, :].astype(BF16)
        vbd = jnp.concatenate([jnp.where(vhead == h, vc, jnp.zeros_like(vc)) for h in range(GLA_HEADS)], axis=0)
        o_sub = []
        for sb in range(nsub):
            rr = sb * sub
            ref_row, kd = kds[sb]
            qd = qc[rr:rr + sub] * jnp.exp(bc[rr:rr + sub] - ref_row)
            qst = jnp.concatenate([jnp.where(head_of_lane == h, qd, 0.0) for h in range(GLA_HEADS)], axis=0)
            a = _dot_nt(qst.astype(BF16), kd)
            qrow = _iota((GLA_HEADS * sub, chunk), 0) & (sub - 1)
            a = jnp.where(_iota((GLA_HEADS * sub, chunk), 1) <= rr + qrow, a, 0.0)
            a_cat = jnp.concatenate([a[h * sub:(h + 1) * sub] for h in range(GLA_HEADS)],
                                    axis=1).astype(BF16)
            o_sub.append(a_cat)
        o_intra = _dot(o_sub[0] if nsub == 1 else jnp.concatenate(o_sub, axis=0), vbd)
        qdb_st = jnp.concatenate([jnp.where(head_of_lane == h, qdb, 0.0) for h in range(GLA_HEADS)],
                                 axis=0).astype(BF16)
        kv = _dot_tn(kdl.astype(BF16), vc)
        upd = jnp.concatenate([kv[h * GLA_DK:(h + 1) * GLA_DK, h * GLA_DV:(h + 1) * GLA_DV]
                               for h in range(GLA_HEADS)], axis=0)
        pre[si, c] = (decay, upd, o_intra, qdb_st)

    for si in range(ns):
        state = s_sc[si]
        for c in range(nchunk):
            c0 = c * chunk
            decay, upd, o_intra, qdb_st = pre[si, c]
            o_inter = _dot(qdb_st, state.astype(BF16))
            for h in range(GLA_HEADS):
                hs = slice(h * GLA_DV, (h + 1) * GLA_DV)
                o = o_inter[h * chunk:(h + 1) * chunk] + o_intra[:, hs]
                o_ref[si, c0:c0 + chunk, hs] = (_rms(o, gn) * r_ref[si, c0:c0 + chunk, hs].astype(F32)).astype(o_ref.dtype)
            state = decay * state + upd
        s_sc[si] = state

    @pl.when(j == pl.num_programs(1) - 1)
    def _():
        sfin_ref[...] = s_sc[...]


def _gla(q, k, v, lg, r, gn, s0, ns, tb, chunk):
    s, t, _ = q.shape
    sub = min(GLA_SUB, chunk)
    row = lambda w: pl.BlockSpec((ns, tb, w), lambda gi, j: (gi, j, 0))
    st = pl.BlockSpec((ns, GQK, GLA_DV), lambda gi, j: (gi, 0, 0))
    return pl.pallas_call(
        functools.partial(_gla_body, ns=ns, tb=tb, chunk=chunk, sub=sub),
        grid=(s // ns, t // tb),
        in_specs=[row(GQK), row(GQK), row(GV), row(GQK), row(GV), pl.BlockSpec(gn.shape, lambda gi, j: (0, 0)), st],
        out_specs=[row(GV), st],
        out_shape=[jax.ShapeDtypeStruct((s, t, GV), BF16), jax.ShapeDtypeStruct((s, GQK, GLA_DV), F32)],
        scratch_shapes=[pltpu.VMEM((ns, GQK, GLA_DV), F32)],
        compiler_params=_params("parallel", "arbitrary"),
        name="gla",
    )(q, k, v, lg, r, gn, s0)


def _mix_body(x_ref, sc_ref, sh_ref, ga_ref, gpre_ref, gpost_ref, on_ref, og_ref, wm_ref, wn_ref, wgo_ref, wo_ref,
              o_ref):
    x = x_ref[...]
    d = x.shape[1]
    h = _norm_mod(x, gpre_ref[...], sc_ref[...], sh_ref[...]).astype(BF16)
    m = _sigmoid(_dot(h, wm_ref[...]))
    y_a = _dot(on_ref[...], wn_ref[...])
    y_b = _dot(og_ref[...], wgo_ref[...])
    mixin = (m[:, 0:d] * y_a + m[:, d:2 * d] * y_b).astype(BF16)
    mix = _dot(mixin, wo_ref[...])
    o_ref[...] = x + ga_ref[...] * _rms(mix, gpost_ref[...])


def _mix(x, sc, sh, ga, gpre, gpost, o_nsa, o_gla, wm, wn, wgo, wo, tm, rows_per_mod):
    n, d = x.shape
    row = lambda w: pl.BlockSpec((tm, w), lambda i: (i, 0))
    full = lambda a: pl.BlockSpec(a.shape, lambda i: (0, 0))
    ms = lambda a: _mod_spec(a, tm, rows_per_mod)
    return pl.pallas_call(
        _mix_body,
        grid=(n // tm,),
        in_specs=[row(d), ms(sc), ms(sh), ms(ga), full(gpre), full(gpost), row(NQ), row(GV),
                  full(wm), full(wn), full(wgo), full(wo)],
        out_specs=row(d),
        out_shape=jax.ShapeDtypeStruct((n, d), F32),
        compiler_params=_params("parallel"),
        name="mix",
    )(x, sc, sh, ga, gpre, gpost, o_nsa, o_gla, wm, wn, wgo, wo)


def _ffn_body(x_ref, sc_ref, sh_ref, ga_ref, gpre_ref, gpost_ref, wg_ref, wu_ref, wd_ref, o_ref, h_sc, acc_sc):
    j = pl.program_id(1)

    @pl.when(j == 0)
    def _():
        h_sc[...] = _norm_mod(x_ref[...], gpre_ref[...], sc_ref[...], sh_ref[...]).astype(BF16)
        acc_sc[...] = jnp.zeros(acc_sc.shape, F32)

    h = h_sc[...]
    gt = _dot(h, wg_ref[...])
    up = _dot(h, wu_ref[...])
    a = (gt * _sigmoid(gt) * up).astype(BF16)
    acc_sc[...] += _dot(a, wd_ref[...])

    @pl.when(j == pl.num_programs(1) - 1)
    def _():
        o_ref[...] = x_ref[...] + ga_ref[...] * _rms(acc_sc[...], gpost_ref[...])


def _ffn(x, sc, sh, ga, gpre, gpost, wg, wu, wd, tm, rows_per_mod):
    n, d = x.shape
    dff = wg.shape[1]
    tf = dff // 2 if (dff // 2) % LANES == 0 else dff
    row = pl.BlockSpec((tm, d), lambda i, j: (i, 0))
    full = lambda a: pl.BlockSpec(a.shape, lambda i, j: (0, 0))
    ms = lambda a: pl.BlockSpec((None, a.shape[1], a.shape[2]), lambda i, j: ((i * tm) // rows_per_mod, 0, 0))
    return pl.pallas_call(
        _ffn_body,
        grid=(n // tm, dff // tf),
        in_specs=[row, ms(sc), ms(sh), ms(ga), full(gpre), full(gpost),
                  pl.BlockSpec((d, tf), lambda i, j: (0, j)), pl.BlockSpec((d, tf), lambda i, j: (0, j)),
                  pl.BlockSpec((tf, d), lambda i, j: (j, 0))],
        out_specs=row,
        out_shape=jax.ShapeDtypeStruct((n, d), F32),
        scratch_shapes=[pltpu.VMEM((tm, d), BF16), pltpu.VMEM((tm, d), F32)],
        compiler_params=_params("parallel", "arbitrary"),
        name="ffn",
    )(x, sc, sh, ga, gpre, gpost, wg, wu, wd)


def _split_w_in(w_in, d):
    sizes = (NQ, KV_W, KV_W, KV_W, 3 * NSA_HEADS, GQK, GQK, GV, 16, GV, 2 * d)
    out, o = [], 0
    for s in sizes:
        out.append(w_in[:, o:o + s])
        o += s
    return out


def _pad_cols(w, n):
    return jnp.pad(w, ((0, 0), (0, n - w.shape[1])))


def _compress_weights(w1k, b1k, w2k, pek, w1v, b1v, w2v, pev):
    hid = w2k.shape[0]
    npair = CMP_STRIDE // 2
    eye = np.eye(NSA_GROUPS, dtype=np.float32)

    def group_diag(w):
        out = jnp.einsum('gh,...ab->...gahb', eye, w)
        return out.reshape(w.shape[:-2] + (NSA_GROUPS * w.shape[-2], NSA_GROUPS * w.shape[-1]))

    def expand_w1(half):
        per_kv = [group_diag(w1.reshape(CMP_LEN, HEAD_DIM, hid)[half * CMP_STRIDE:(half + 1) * CMP_STRIDE])
                  for w1 in (w1k, w1v)]
        return jnp.stack(per_kv).reshape(2, npair, 2 * NSA_GROUPS * HEAD_DIM, NSA_GROUPS * hid)

    w1 = jnp.concatenate([expand_w1(0), expand_w1(1)], axis=-1).astype(BF16)

    def expand_pe(half):
        per_kv = [jnp.tile(pe[half * CMP_STRIDE:(half + 1) * CMP_STRIDE], (1, NSA_GROUPS)) for pe in (pek, pev)]
        return jnp.stack(per_kv).reshape(2, npair, 2 * NSA_GROUPS * HEAD_DIM)

    pe = jnp.pad(jnp.stack([expand_pe(0), expand_pe(1)], axis=2),
                 ((0, 0), (0, 0), (0, PE_ROWS - 2), (0, 0))).astype(BF16)

    b1 = jnp.concatenate([jnp.tile(b1k, NSA_GROUPS), jnp.tile(b1v, NSA_GROUPS)]).reshape(1, -1)
    w2d = [group_diag(w2k), group_diag(w2v)]
    zero = jnp.zeros_like(w2d[0])
    w2 = jnp.concatenate([jnp.concatenate([w2d[0], zero], axis=1),
                          jnp.concatenate([zero, w2d[1]], axis=1)], axis=0).astype(BF16)
    return pe, w1, b1, w2


def _overlap(n_cmp_pad, n_cmp, nb):
    n = np.arange(n_cmp_pad)[:, None]
    j = np.arange(nb)[None, :] * SLC_BLOCK
    ok = (n * CMP_STRIDE < j + SLC_BLOCK) & (n * CMP_STRIDE + CMP_LEN > j) & (n < n_cmp)
    return ok.astype(BF16)


def _block_onehot(nb, nkeys):
    return (np.arange(nb)[:, None] == (np.arange(nkeys)[None, :] // SLC_BLOCK)).astype(BF16)


def _tile_rows(n, cap):
    t = cap
    while n % t:
        t //= 2
    return t


def _feature_major(a):
    nd = a.ndim
    a = jnp.moveaxis(a, nd - 4, nd - 1)
    return a.reshape(a.shape[:nd - 4] + (KV_W, a.shape[-1]))


def _row_major(a_t):
    b, _, r = a_t.shape
    return jnp.moveaxis(a_t.reshape(b, 2, NSA_GROUPS, HEAD_DIM, r), 4, 1)


def kernel(x_prompt, x_sample, cache_cmp_kv, cache_slc_kv, state_win_kv, state_gla, page_table, c_prompt, c_sample, w_ada, b_ada, g_pre_mix, g_post_mix, w_in, cmp_w1_k, cmp_b1_k, cmp_w2_k, cmp_pe_k, cmp_w1_v, cmp_b1_v, cmp_w2_v, cmp_pe_v, w_gla_a2, b_gla_a, gla_norm_g, w_nsa_o, w_gla_o, w_out, g_pre_ffn, g_post_ffn, w_ff_gate, w_ff_up, w_ff_down):
    depth = w_ada.shape[0]
    bp, l, d = x_prompt.shape
    bd, s_new, _ = x_sample.shape
    n_pages = page_table.shape[1]
    p_len = n_pages * PAGE_SIZE
    w_buf = state_win_kv.shape[2]
    keep = min(WINDOW, l)
    np_, ns_ = bp * l, bd * s_new
    tmp = _tile_rows(np_ // bp, 512)
    tms = _tile_rows(ns_, 512)
    tq = _tile_rows(l, 256)
    tk = 2 * tq

    yp, ys = x_prompt.reshape(np_, d), x_sample.reshape(ns_, d)
    col = [[] for _ in range(8)]
    for li in range(depth):
        r_all = bp + bd
        r_pad = -(-r_all // 8) * 8
        c_all = jnp.pad(jnp.concatenate([c_prompt, c_sample], axis=0), ((0, r_pad - r_all), (0, 0)))
        mod = _ada(c_all, w_ada[li], b_ada[li])
        mods_p = [m.reshape(bp, 1, d) for m in jnp.split(mod[:bp], 6, axis=-1)]
        mods_s = [jnp.repeat(m, s_new, axis=0).reshape(ns_ // tms, tms, d) for m in jnp.split(mod[bp:r_all], 6, axis=-1)]

        (w_q, w_kc, w_ks, w_kw, w_gn, w_qg, w_kg, w_vg, w_ag, w_rg, w_mg) = _split_w_in(w_in[li], d)
        wq = w_q.astype(BF16)
        wkv = jnp.concatenate([w_kc, w_ks, w_kw], axis=1).astype(BF16)
        wgn = _pad_cols(w_gn, LANES).astype(BF16)
        wqkv = jnp.concatenate([w_qg, w_kg, w_vg], axis=1).astype(BF16)
        wa = _pad_cols(w_ag, LANES).astype(BF16)
        wr = w_rg.astype(BF16)
        a2 = jnp.pad(w_gla_a2[li], ((0, LANES - w_gla_a2.shape[1]), (0, 0)))
        ba = b_gla_a[li].reshape(1, GQK)
        cw = _compress_weights(cmp_w1_k[li], cmp_b1_k[li], cmp_w2_k[li], cmp_pe_k[li],
                               cmp_w1_v[li], cmp_b1_v[li], cmp_w2_v[li], cmp_pe_v[li])
        gpre = g_pre_mix[li].reshape(1, d)
        gpost = g_post_mix[li].reshape(1, d)
        gpre2 = g_pre_ffn[li].reshape(1, d)
        gpost2 = g_post_ffn[li].reshape(1, d)
        gn = gla_norm_g[li].reshape(1, GLA_DV)
        wm = w_mg.astype(BF16)
        wn = w_nsa_o[li].astype(BF16)
        wgo = w_gla_o[li].astype(BF16)
        wo = w_out[li].astype(BF16)
        wfg = w_ff_gate[li].astype(BF16)
        wfu = w_ff_up[li].astype(BF16)
        wfd = w_ff_down[li].astype(BF16)

        sh1, sc1, ga1, sh2, sc2, ga2 = mods_p
        (q, kvc, _, _, kvs_b, kvw_b, gates, kvc_t, kvs_t, kvw_t) = _pre_nsa(yp, sc1, sh1, gpre, wq, wkv, wgn, tmp, l,
                                                                            feature_major=(bp, l))
        qg, kg, vg, lg, rg = _pre_gla(yp, sc1, sh1, gpre, wqkv, wa, wr, a2, ba, tmp, l)
        ckv = _cmp_prompt(kvc.reshape(bp, l, KV_W), cw)
        nch = l // CMP_STRIDE
        nb = l // SLC_BLOCK
        o_nsa = _nsa_prompt(q.reshape(bp, l, NQ), gates.reshape(bp, l, LANES), ckv, kvs_b.reshape(bp, l, KV_W),
                            kvw_b.reshape(bp, l, KV_W), _overlap(nch, nch - 1, nb), _block_onehot(nb, l).T, tq, tk)
        chunk = GLA_CHUNK if l % GLA_CHUNK == 0 else l
        tb = 2 * LANES if (l % (2 * LANES) == 0 and LANES % chunk == 0) else chunk
        o_gla, s_fin_p = _gla(qg.reshape(bp, l, GQK), kg.reshape(bp, l, GQK), vg.reshape(bp, l, GV),
                              lg.reshape(bp, l, GQK), rg.reshape(bp, l, GV), gn,
                              jnp.zeros((bp, GQK, GLA_DV), F32), bp, tb, chunk)
        x1 = _mix(yp, sc1, sh1, ga1, gpre, gpost, o_nsa.reshape(np_, NQ), o_gla.reshape(np_, GV), wm, wn, wgo, wo, tmp, l)
        yp = _ffn(x1, sc2, sh2, ga2, gpre2, gpost2, wfg, wfu, wfd, tmp, l)
        col[0].append(_row_major(kvc_t))
        col[2].append(_row_major(kvs_t))
        col[4].append(_row_major(kvw_t[:, :, l - keep:]))
        col[6].append(s_fin_p.reshape(bp, GLA_HEADS, GLA_DK, GLA_DV).astype(state_gla.dtype))

        sh1, sc1, ga1, sh2, sc2, ga2 = mods_s
        q, kvc, kvs, kvw, _, _, gates = _pre_nsa(ys, sc1, sh1, gpre, wq, wkv, wgn, tms, tms)
        qg, kg, vg, lg, rg = _pre_gla(ys, sc1, sh1, gpre, wqkv, wa, wr, a2, ba, tms, tms)
        ckv = _cmp_sample(_feature_major(cache_cmp_kv[li]), page_table, cw)
        nch = p_len // CMP_STRIDE
        nb = p_len // SLC_BLOCK
        win_t = _feature_major(state_win_kv[li])
        kvw3 = kvw.reshape(bd, s_new, KV_W)
        o_nsa = _nsa_sample(q.reshape(bd, s_new, NQ), gates.reshape(bd, s_new, LANES), ckv,
                            _feature_major(cache_slc_kv[li]), page_table, win_t, kvs.reshape(bd, s_new, KV_W), kvw3,
                            _overlap(nch, nch - 1, nb), _block_onehot(nb, p_len))
        chunk = GLA_CHUNK if s_new % GLA_CHUNK == 0 else s_new
        o_gla, s_fin_s = _gla(qg.reshape(bd, s_new, GQK), kg.reshape(bd, s_new, GQK), vg.reshape(bd, s_new, GV),
                              lg.reshape(bd, s_new, GQK), rg.reshape(bd, s_new, GV), gn,
                              state_gla[li].astype(F32).reshape(bd, GQK, GLA_DV), _tile_rows(bd, 16), chunk, chunk)
        x1 = _mix(ys, sc1, sh1, ga1, gpre, gpost, o_nsa.reshape(ns_, NQ), o_gla.reshape(ns_, GV), wm, wn, wgo, wo, tms, tms)
        ys = _ffn(x1, sc2, sh2, ga2, gpre2, gpost2, wfg, wfu, wfd, tms, tms)
        win_new_t = jnp.concatenate([win_t, kvw3.transpose(0, 2, 1)], axis=2)[:, :, s_new:]
        win_new = jnp.moveaxis(win_new_t.reshape(bd, 2, NSA_GROUPS, HEAD_DIM, w_buf), 4, 1)
        col[1].append(kvc.reshape(bd, s_new, 2, NSA_GROUPS, HEAD_DIM))
        col[3].append(kvs.reshape(bd, s_new, 2, NSA_GROUPS, HEAD_DIM))
        col[5].append(win_new)
        col[7].append(s_fin_s.reshape(bd, GLA_HEADS, GLA_DK, GLA_DV).astype(state_gla.dtype))

    stacked = [c[0][None] if len(c) == 1 else jnp.stack(c) for c in col]
    return (yp.reshape(bp, l, d), ys.reshape(bd, s_new, d), *stacked)
```

```python
import functools
import math

import numpy as np
import jax
import jax.numpy as jnp
from jax import lax
from jax.experimental import pallas as pl
from jax.experimental.pallas import tpu as pltpu

F32 = jnp.float32
BF16 = jnp.bfloat16

NSA_HEADS = 8
NSA_GROUPS = 2
NSA_HPG = NSA_HEADS // NSA_GROUPS
HEAD_DIM = 64
CMP_STRIDE = 16
CMP_LEN = 32
SLC_BLOCK = 64
N_SEL = 16
WINDOW = 512
GLA_HEADS = 4
GLA_DK = 64
GLA_DV = 128
GLA_TAU = 16.0
GLA_CHUNK = 64
GLA_SUB = 16
PE_ROWS = 16
SAMPLE_SEQS_PER_STEP = 2
EPS = 1e-6
PAGE_SIZE = 128
NEG = -1e30
LOG2E = math.log2(math.e)
LANES = 128
VMEM_LIMIT = 56 * 1024 * 1024

KV_W = 2 * NSA_GROUPS * HEAD_DIM
NQ = NSA_HEADS * HEAD_DIM
GQK = GLA_HEADS * GLA_DK
GV = GLA_HEADS * GLA_DV


def _dot(a, b):
    return jnp.dot(a, b, preferred_element_type=F32)


def _dot_nt(a, b):
    return lax.dot_general(a, b, (((1,), (1,)), ((), ())), preferred_element_type=F32)


def _dot_f32(a, b):
    return jnp.dot(a, b, preferred_element_type=F32, precision=lax.Precision.HIGHEST)


def _dot01(x, m01):
    hi = x.astype(BF16)
    r1 = x - hi.astype(F32)
    mid = r1.astype(BF16)
    lo = (r1 - mid.astype(F32)).astype(BF16)
    return _dot(hi, m01) + _dot(mid, m01) + _dot(lo, m01)


def _dot01_l(m01, x):
    hi = x.astype(BF16)
    r1 = x - hi.astype(F32)
    mid = r1.astype(BF16)
    lo = (r1 - mid.astype(F32)).astype(BF16)
    return _dot(m01, hi) + _dot(m01, mid) + _dot(m01, lo)


def _dot_tn(a, b):
    return lax.dot_general(a, b, (((0,), (0,)), ((), ())), preferred_element_type=F32)


def _dot01_tn(x, m01):
    hi = x.astype(BF16)
    r1 = x - hi.astype(F32)
    mid = r1.astype(BF16)
    lo = (r1 - mid.astype(F32)).astype(BF16)
    return _dot_tn(hi, m01) + _dot_tn(mid, m01) + _dot_tn(lo, m01)


def _sigmoid(x):
    return 1.0 / (1.0 + jnp.exp(-x))


def _params(*sem):
    return pltpu.CompilerParams(dimension_semantics=sem, vmem_limit_bytes=VMEM_LIMIT)


def _iota(shape, dim):
    return lax.broadcasted_iota(jnp.int32, shape, dim)


def _div_pow2(x, n):
    assert n & (n - 1) == 0, n
    return x >> (n.bit_length() - 1)


def _ada_body(c_ref, w_ref, b_ref, o_ref):
    o_ref[...] = _dot_f32(c_ref[...], w_ref[...]) + b_ref[...]


def _ada(c_all, w_ada, b_ada):
    r, d = c_all.shape
    n = w_ada.shape[1]
    tn = 1536 if n % 1536 == 0 else n
    return pl.pallas_call(
        _ada_body,
        grid=(n // tn,),
        in_specs=[pl.BlockSpec((r, d), lambda j: (0, 0)),
                  pl.BlockSpec((d, tn), lambda j: (0, j)),
                  pl.BlockSpec((1, tn), lambda j: (0, j))],
        out_specs=pl.BlockSpec((r, tn), lambda j: (0, j)),
        out_shape=jax.ShapeDtypeStruct((r, n), F32),
        compiler_params=_params("arbitrary"),
        name="ada",
    )(c_all, w_ada, b_ada.reshape(1, n))


def _norm_mod(x, g, sc, sh):
    ms = jnp.mean(x * x, axis=-1, keepdims=True)
    y = x * lax.rsqrt(ms + EPS) * g
    return y * (1.0 + sc) + sh


def _rms(x, g):
    ms = jnp.mean(x * x, axis=-1, keepdims=True)
    return x * lax.rsqrt(ms + EPS) * g


def _pre_nsa_body(x_ref, sc_ref, sh_ref, g_ref, wq_ref, wkv_ref, wg_ref,
                  q_ref, kvc_ref, kvs_ref, kvw_ref, kvsb_ref, kvwb_ref, gate_ref, *t_refs):
    h = _norm_mod(x_ref[...], g_ref[...], sc_ref[...], sh_ref[...]).astype(BF16)
    q_ref[...] = (_dot(h, wq_ref[...]) * (HEAD_DIM ** -0.5 * LOG2E)).astype(BF16)
    kv = _dot(h, wkv_ref[...])
    kvc = kv[:, 0:KV_W]
    kvs = kv[:, KV_W:2 * KV_W]
    kvw = kv[:, 2 * KV_W:3 * KV_W]
    kvc_ref[...] = kvc
    kvs_ref[...] = kvs
    kvw_ref[...] = kvw
    kvsb_ref[...] = kvs.astype(BF16)
    kvwb_ref[...] = kvw.astype(BF16)
    gate_ref[...] = _sigmoid(_dot(h, wg_ref[...]))
    if t_refs:
        kvt = kv.T
        for j, t_ref in enumerate(t_refs):
            t_ref[...] = kvt[j * KV_W:(j + 1) * KV_W]


def _mod_spec(mod3, tm, rows_per_mod):
    r = mod3.shape[1]
    return pl.BlockSpec((None, r, mod3.shape[2]), lambda i: ((i * tm) // rows_per_mod, 0, 0))


def _pre_nsa(x, sc, sh, g, wq, wkv, wg, tm, rows_per_mod, feature_major=None):
    n, d = x.shape
    row = lambda w: pl.BlockSpec((tm, w), lambda i: (i, 0))
    full = lambda a: pl.BlockSpec(a.shape, lambda i: (0, 0))
    outs = [(NQ, BF16), (KV_W, F32), (KV_W, F32), (KV_W, F32), (KV_W, BF16), (KV_W, BF16), (LANES, F32)]
    out_specs = [row(w) for w, _ in outs]
    out_shape = [jax.ShapeDtypeStruct((n, w), t) for w, t in outs]
    if feature_major is not None:
        nbatch, per = feature_major
        tiles = per // tm
        out_specs += [pl.BlockSpec((None, KV_W, tm), lambda i: (i // tiles, 0, i % tiles))] * 3
        out_shape += [jax.ShapeDtypeStruct((nbatch, KV_W, per), F32)] * 3
    return pl.pallas_call(
        _pre_nsa_body,
        grid=(n // tm,),
        in_specs=[row(d), _mod_spec(sc, tm, rows_per_mod), _mod_spec(sh, tm, rows_per_mod), full(g),
                  full(wq), full(wkv), full(wg)],
        out_specs=out_specs,
        out_shape=out_shape,
        compiler_params=_params("parallel"),
        name="pre_nsa",
    )(x, sc, sh, g, wq, wkv, wg)


def _log_sigmoid(x):
    return jnp.minimum(x, 0.0) - jnp.log(1.0 + jnp.exp(-jnp.abs(x)))


def _pre_gla_body(x_ref, sc_ref, sh_ref, g_ref, wqkv_ref, wa_ref, wr_ref, a2_ref, ba_ref,
                  q_ref, k_ref, v_ref, lg_ref, r_ref):
    h = _norm_mod(x_ref[...], g_ref[...], sc_ref[...], sh_ref[...]).astype(BF16)
    qkv = _dot(h, wqkv_ref[...])
    q_ref[...] = qkv[:, 0:GQK] * GLA_DK ** -0.5
    k_ref[...] = qkv[:, GQK:2 * GQK]
    v_ref[...] = qkv[:, 2 * GQK:2 * GQK + GV]
    a = _dot(h, wa_ref[...])
    lg_ref[...] = _log_sigmoid(_dot_f32(a, a2_ref[...]) + ba_ref[...]) * (1.0 / GLA_TAU)
    r = _dot(h, wr_ref[...])
    r_ref[...] = (r * _sigmoid(r)).astype(BF16)


def _pre_gla(x, sc, sh, g, wqkv, wa, wr, a2, ba, tm, rows_per_mod):
    n, d = x.shape
    row = lambda w: pl.BlockSpec((tm, w), lambda i: (i, 0))
    full = lambda a: pl.BlockSpec(a.shape, lambda i: (0, 0))
    return pl.pallas_call(
        _pre_gla_body,
        grid=(n // tm,),
        in_specs=[row(d), _mod_spec(sc, tm, rows_per_mod), _mod_spec(sh, tm, rows_per_mod), full(g),
                  full(wqkv), full(wa), full(wr), full(a2), full(ba)],
        out_specs=[row(GQK), row(GQK), row(GV), row(GQK), row(GV)],
        out_shape=[jax.ShapeDtypeStruct((n, GQK), F32), jax.ShapeDtypeStruct((n, GQK), F32),
                   jax.ShapeDtypeStruct((n, GV), F32), jax.ShapeDtypeStruct((n, GQK), F32),
                   jax.ShapeDtypeStruct((n, GV), BF16)],
        compiler_params=_params("parallel"),
        name="pre_gla",
    )(x, sc, sh, g, wqkv, wa, wr, a2, ba)


def _gelu_tanh(x):
    return 0.5 * x * (1.0 + jnp.tanh(0.7978845608028654 * (x + 0.044715 * x * x * x)))


def _compress(get_xp, nch, pe_ref, w1_ref, b1_ref, w2_ref):
    first, second = [], []
    for kv in range(2):
        acc = jnp.zeros((nch + PE_ROWS, 2 * LANES), F32)
        for pp in range(CMP_STRIDE // 2):
            lhs = jnp.concatenate([get_xp(kv, 2 * pp), get_xp(kv, 2 * pp + 1)], axis=1).astype(BF16)
            acc = acc + _dot(jnp.concatenate([lhs, pe_ref[kv, pp]], axis=0), w1_ref[kv, pp])
        first.append(acc[0:nch, 0:LANES] + acc[nch:nch + 1, 0:LANES])
        second.append(acc[0:nch, LANES:2 * LANES] + acc[nch + 1:nch + 2, LANES:2 * LANES])
    first = jnp.concatenate(first, axis=1)
    second = jnp.concatenate(second, axis=1)
    if nch % 8 == 0:
        nxt = pltpu.roll(second, nch - 1, 0)
    else:
        nxt = jnp.concatenate([second[1:], second[:1]], axis=0)
    hid = _gelu_tanh(first + nxt + b1_ref[...])
    out = _dot(hid.astype(BF16), w2_ref[...])
    return jnp.where(_iota(out.shape, 0) < nch - 1, out, 0.0)


def _cmp_prompt_body(xk_ref, xv_ref, pe_ref, w1_ref, b1_ref, w2_ref, o_ref, *, nch):
    x_refs = (xk_ref, xv_ref)

    def get_xp(kv, p):
        return x_refs[kv][pl.ds(p, nch, stride=CMP_STRIDE), :]

    o_ref[...] = _compress(get_xp, nch, pe_ref, w1_ref, b1_ref, w2_ref).astype(o_ref.dtype)


def _cmp_prompt(kvc, cw):
    b, l, _ = kvc.shape
    nch = l // CMP_STRIDE
    full = lambda a: pl.BlockSpec(a.shape, lambda i: (0,) * a.ndim)
    return pl.pallas_call(
        functools.partial(_cmp_prompt_body, nch=nch),
        grid=(b,),
        in_specs=[pl.BlockSpec((None, l, LANES), lambda i: (i, 0, 0)),
                  pl.BlockSpec((None, l, LANES), lambda i: (i, 0, 1))] + [full(a) for a in cw],
        out_specs=pl.BlockSpec((None, nch, KV_W), lambda i: (i, 0, 0)),
        out_shape=jax.ShapeDtypeStruct((b, nch, KV_W), BF16),
        compiler_params=_params("parallel"),
        name="cmp_prompt",
    )(kvc, kvc, *cw)


def _cmp_sample_body(pt_ref, cache_ref, pe_ref, w1_ref, b1_ref, w2_ref, o_ref,
                     page_sc, ak_sc, av_sc, bk_sc, bv_sc, sem, *, n_seq, n_pages, nch):
    i = pl.program_id(0)
    cpp = PAGE_SIZE // CMP_STRIDE
    bufs = ((ak_sc, av_sc), (bk_sc, bv_sc))

    def page_copy(seq, slot, k):
        return pltpu.make_async_copy(cache_ref.at[pt_ref[seq, k]], page_sc.at[slot, k], sem.at[slot])

    def fetch(seq, slot):
        for k in range(n_pages):
            page_copy(seq, slot, k).start()

    def wait(slot):
        for k in range(n_pages):
            pltpu.make_async_copy(cache_ref.at[0], page_sc.at[slot, k], sem.at[slot]).wait()

    def regroup(slot):
        for k in range(n_pages):
            for kv in range(2):
                xt = page_sc[slot, k, kv * LANES:(kv + 1) * LANES, :].T
                for n in range(cpp):
                    for a in range(CMP_STRIDE // 8):
                        r0 = n * CMP_STRIDE + 8 * a
                        bufs[slot][kv][k, pl.ds(8 * a * cpp + n, 8, stride=cpp), :] = xt[r0:r0 + 8, :]

    def mlp(src):
        def get_xp(kv, p):
            return src[kv][:, p * cpp:(p + 1) * cpp, :].reshape(nch, LANES)

        o_ref[...] = _compress(get_xp, nch, pe_ref, w1_ref, b1_ref, w2_ref).astype(o_ref.dtype)

    @pl.when(i == 0)
    def _():
        fetch(0, 0)
        if n_seq == 1:
            page_sc[1] = jnp.zeros(page_sc.shape[1:], F32)
        bk_sc[...] = jnp.zeros(bk_sc.shape, F32)
        bv_sc[...] = jnp.zeros(bv_sc.shape, F32)

    for slot in range(2):
        @pl.when(i % 2 == slot)
        def _():
            @pl.when(i + 1 < n_seq)
            def _():
                fetch(i + 1, 1 - slot)

            @pl.when(i < n_seq)
            def _():
                wait(slot)

            regroup(slot)
            mlp(bufs[1 - slot])


def _cmp_sample(cache_t, page_table, cw):
    bd, n_pages = page_table.shape
    p_len = n_pages * PAGE_SIZE
    nch = p_len // CMP_STRIDE
    full = lambda a: pl.BlockSpec(a.shape, lambda i, pt: (0,) * a.ndim)
    return pl.pallas_call(
        functools.partial(_cmp_sample_body, n_seq=bd, n_pages=n_pages, nch=nch),
        grid_spec=pltpu.PrefetchScalarGridSpec(
            num_scalar_prefetch=1,
            grid=(bd + 1,),
            in_specs=[pl.BlockSpec(memory_space=pl.ANY)] + [full(a) for a in cw],
            out_specs=pl.BlockSpec((None, nch, KV_W), lambda i, pt: (jnp.maximum(i - 1, 0), 0, 0)),
            scratch_shapes=[pltpu.VMEM((2, n_pages, KV_W, PAGE_SIZE), F32)]
            + [pltpu.VMEM((n_pages, PAGE_SIZE, LANES), F32)] * 4 + [pltpu.SemaphoreType.DMA((2,))],
        ),
        out_shape=jax.ShapeDtypeStruct((bd, nch, KV_W), BF16),
        compiler_params=_params("arbitrary"),
        name="cmp_sample",
    )(page_table, cache_t, *cw)


def _head_queries(q, rows):
    half = _iota((rows, LANES), 1) >> 6
    out = []
    for hh in range(NSA_HEADS):
        g = hh // NSA_HPG
        blk = q[:, (hh // 2) * LANES:(hh // 2 + 1) * LANES].astype(F32)
        if hh % 2 != g:
            blk = pltpu.roll(blk, HEAD_DIM, 1) if rows % 8 == 0 else jnp.roll(blk, HEAD_DIM, 1)
        out.append(jnp.where(half == g, blk, 0.0).astype(BF16))
    return out


def _topk_select(imp, cur):
    impt = imp.T
    nb = impt.shape[0]
    jl = _iota(impt.shape, 0)
    jf = jl.astype(F32)
    forced = (jl == 0) | (jl == cur) | (jl == cur - 1)
    elig = (jl >= 1) & (jl <= cur - 2)
    sel0 = jnp.where(forced, 1.0, 0.0)
    alive0 = jnp.where(elig, 1.0, 0.0)

    def step(_, carry):
        sel, alive = carry
        live = alive > 0.0
        vals = jnp.where(live, impt, -1.0)
        m = jnp.max(vals, axis=0, keepdims=True)
        cand = live & (vals == m)
        idx = jnp.min(jnp.where(cand, jf, float(nb)), axis=0, keepdims=True)
        one = jf == idx
        return jnp.where(one, 1.0, sel), jnp.where(one, 0.0, alive)

    sel, _ = lax.fori_loop(0, N_SEL - 3, step, (sel0, alive0))
    return sel.T


def _masked_softmax2(s, valid):
    s = jnp.where(valid, s, NEG)
    m = jnp.max(s, axis=-1, keepdims=True)
    e = jnp.where(valid, jnp.exp2(s - m), 0.0)
    d = jnp.sum(e, axis=-1, keepdims=True)
    return e * (1.0 / jnp.where(d > 0.0, d, 1.0))


def _nsa_prompt_body(q_ref, gate_ref, ckv_ref, kvs_ref, kvw_ref, ov_ref, et_ref, o_ref, qa_sc, m_sc, acc_sc, out_sc,
                     *, tq, tk, n_cmp):
    i = pl.program_id(1)
    t0 = i * tq
    nc = ckv_ref.shape[0]
    nb = ov_ref.shape[1]
    nh = NSA_HEADS
    qh = _head_queries(q_ref[...], tq)
    for hh in range(nh):
        qa_sc[hh * tq:(hh + 1) * tq, 0:LANES] = qh[hh]
    half = _iota((tq, LANES), 1) >> 6
    trow = t0 + _iota((1, tq, 1), 1)
    gates = gate_ref[...]

    def gate(branch, hh):
        c = branch * nh + hh
        return gates[:, c:c + 1]

    ck = ckv_ref[:, 0:LANES]
    cv = ckv_ref[:, LANES:2 * LANES]
    s = _dot_nt(qa_sc[:, 0:LANES], ck).reshape(nh, tq, nc)
    ncol = _iota((1, tq, nc), 2)
    cvalid = (ncol * CMP_STRIDE + CMP_LEN <= trow + 1) & (ncol < n_cmp)
    p = _masked_softmax2(s, cvalid)
    for hh in range(nh):
        out_sc[hh] = gate(0, hh) * _dot(p[hh].astype(BF16), cv)
    ov = ov_ref[...]
    imp = []
    for g in range(NSA_GROUPS):
        psum = p[g * NSA_HPG]
        for h in range(1, NSA_HPG):
            psum = psum + p[g * NSA_HPG + h]
        imp.append(_dot01(psum, ov))
    imp = jnp.concatenate(imp, axis=0)
    cur = (t0 + (_iota((1, NSA_GROUPS * tq), 1) & (tq - 1))) >> 6
    sel = _topk_select(imp, cur)
    selneg = jnp.where(sel > 0.0, 0.0, NEG).astype(BF16)
    for hh in range(nh):
        g = hh // NSA_HPG
        qa_sc[hh * tq:(hh + 1) * tq, LANES:LANES + nb] = selneg[g * tq:(g + 1) * tq]

    def load_v(ref, start, width):
        v128 = ref[pl.ds(start, width), LANES:2 * LANES]
        lane_half = _iota(v128.shape, 1) >> 6
        return [jnp.where(lane_half == g, v128, jnp.ones_like(v128)) for g in range(NSA_GROUPS)]

    def online(s3, vaug, first):
        width = s3.shape[-1]
        tmax = jnp.max(s3, axis=-1, keepdims=True)
        if first:
            m_new = jnp.broadcast_to(tmax, (nh, tq, LANES))
        else:
            m_old = m_sc[...]
            m_new = jnp.maximum(m_old, tmax)
            alpha = jnp.exp2(m_old - m_new)
        pexp = jnp.concatenate([jnp.exp2((s3[:, :, c * LANES:(c + 1) * LANES] - m_new).astype(BF16))
                                for c in range(width // LANES)], axis=-1)
        for hh in range(nh):
            pv = _dot(pexp[hh], vaug[hh // NSA_HPG])
            acc_sc[hh] = pv if first else alpha[hh] * acc_sc[hh] + pv
        m_sc[...] = m_new

    def finish(branch):
        for hh in range(nh):
            acc = acc_sc[hh]
            den = pltpu.roll(acc, HEAD_DIM, 1)
            out_sc[hh] = out_sc[hh] + gate(branch, hh) * (acc / den)

    def slc_scores(start):
        kaug = jnp.concatenate([kvs_ref[pl.ds(start, tk), 0:LANES], et_ref[pl.ds(start, tk), :]], axis=1)
        return _dot_nt(qa_sc[...], kaug).reshape(nh, tq, tk)

    n_full = i // (tk // tq)
    last = pl.multiple_of(n_full * tk, tk)
    s3 = slc_scores(last)
    s3 = jnp.where(last + _iota((1, tq, tk), 2) <= trow, s3, NEG)
    online(s3, load_v(kvs_ref, last, tk), True)

    def slc_step(j, carry):
        start = pl.multiple_of(j * tk, tk)
        online(slc_scores(start), load_v(kvs_ref, start, tk), False)
        return carry

    lax.fori_loop(0, n_full, slc_step, 0)
    finish(1)

    d0 = pl.multiple_of(t0, tq)
    s3 = _dot_nt(qa_sc[:, 0:LANES], kvw_ref[pl.ds(d0, tq), 0:LANES]).reshape(nh, tq, tq)
    s3 = jnp.where(_iota((1, tq, tq), 2) <= _iota((1, tq, tq), 1), s3, NEG)
    online(s3, load_v(kvw_ref, d0, tq), True)

    @pl.when(i >= 1)
    def _():
        start = pl.multiple_of(jnp.maximum(t0 - WINDOW, 0), tq)
        s3 = _dot_nt(qa_sc[:, 0:LANES], kvw_ref[pl.ds(start, WINDOW), 0:LANES]).reshape(nh, tq, WINDOW)
        kpos = start + _iota((1, tq, WINDOW), 2)
        s3 = jnp.where((kpos < t0) & (trow - kpos < WINDOW), s3, NEG)
        online(s3, load_v(kvw_ref, start, WINDOW), False)

    finish(2)

    for pair in range(nh // 2):
        parts = []
        for hh in (2 * pair, 2 * pair + 1):
            x = out_sc[hh]
            if hh % 2 != hh // NSA_HPG:
                x = pltpu.roll(x, HEAD_DIM, 1)
            parts.append(x)
        o_ref[:, pair * LANES:(pair + 1) * LANES] = jnp.where(half == 0, parts[0], parts[1]).astype(o_ref.dtype)


def _nsa_prompt(q, gates, ckv, kvs_b, kvw_b, ov, et, tq, tk):
    b, l, _ = q.shape
    nb = l // SLC_BLOCK
    assert l % tk == 0 and tk % tq == 0 and WINDOW % tq == 0 and l >= WINDOW
    blk = lambda w: pl.BlockSpec((None, tq, w), lambda bi, i: (bi, i, 0))
    seq = lambda a: pl.BlockSpec((None,) + a.shape[1:], lambda bi, i: (bi, 0, 0))
    full = lambda a: pl.BlockSpec(a.shape, lambda bi, i: (0, 0))
    hs = (NSA_HEADS, tq, LANES)
    return pl.pallas_call(
        functools.partial(_nsa_prompt_body, tq=tq, tk=tk, n_cmp=l // CMP_STRIDE - 1),
        grid=(b, l // tq),
        in_specs=[blk(NQ), blk(LANES), seq(ckv), seq(kvs_b), seq(kvw_b), full(ov), full(et)],
        out_specs=blk(NQ),
        out_shape=jax.ShapeDtypeStruct((b, l, NQ), BF16),
        scratch_shapes=[pltpu.VMEM((NSA_HEADS * tq, LANES + nb), BF16), pltpu.VMEM(hs, F32), pltpu.VMEM(hs, F32),
                        pltpu.VMEM(hs, F32)],
        compiler_params=_params("parallel", "arbitrary"),
        name="nsa_prompt",
    )(q, gates, ckv, kvs_b, kvw_b, ov, et)


def _nsa_sample_body(*refs, n_pages, nseq, s_new, n_cmp, p_len):
    page_refs = refs[1:1 + nseq * n_pages]
    (q_ref, gate_ref, ckv_ref, win_ref, ns_ref, nw_ref, ov_ref, e_ref, o_ref) = refs[1 + nseq * n_pages:]
    rows1 = NSA_HEADS * s_new
    rows = nseq * rows1
    nc = ckv_ref.shape[1]
    nb = ov_ref.shape[1]
    w_buf = win_ref.shape[2]
    seqs = range(nseq)
    of = lambda x, s: x[s * rows1:(s + 1) * rows1]
    cat = lambda xs: xs[0] if len(xs) == 1 else jnp.concatenate(xs, axis=0)
    q2 = [jnp.concatenate(_head_queries(q_ref[s], s_new), axis=0) for s in seqs]
    ridx = _iota((rows, 1), 0)
    trow = ridx & (s_new - 1)
    grow = _div_pow2(ridx, s_new * NSA_HPG) & (NSA_GROUPS - 1)
    lane_half = _iota((rows, LANES), 1) >> 6

    def gate_rows(branch):
        cols = [gate_ref[s][:, branch * NSA_HEADS + hh:branch * NSA_HEADS + hh + 1]
                for s in seqs for hh in range(NSA_HEADS)]
        return jnp.concatenate(cols, axis=0)

    def pick(acc, den):
        return jnp.where(lane_half == grow, acc, 0.0) / den

    def attend(s_old, s_nw, vt_old, new_rows):
        s_old = cat(s_old)
        s_nw = jnp.where(_iota((rows, s_new), 1) <= trow, cat(s_nw), NEG)
        m = jnp.maximum(jnp.max(s_old, axis=-1, keepdims=True), jnp.max(s_nw, axis=-1, keepdims=True))
        e1 = jnp.exp2(s_old - m)
        e2 = jnp.exp2(s_nw - m)
        den = jnp.sum(e1, axis=-1, keepdims=True) + jnp.sum(e2, axis=-1, keepdims=True)
        e1 = e1.astype(BF16)
        e2 = e2.astype(BF16)
        acc = cat([_dot_nt(vt_old[s], of(e1, s)).T + _dot(of(e2, s), new_rows[s][:, LANES:2 * LANES]) for s in seqs])
        return pick(acc, den)

    ncol = _iota((rows, nc), 1)
    cvalid = (ncol * CMP_STRIDE + CMP_LEN <= p_len + trow + 1) & (ncol < n_cmp)
    p_c = _masked_softmax2(cat([_dot_nt(q2[s], ckv_ref[s, :, 0:LANES]) for s in seqs]), cvalid)
    o_c = cat([_dot(of(p_c, s).astype(BF16), ckv_ref[s, :, LANES:2 * LANES]) for s in seqs])
    out = gate_rows(0) * jnp.where(lane_half == grow, o_c, 0.0)
    psum = []
    for s in seqs:
        for g in range(NSA_GROUPS):
            acc = jnp.zeros((s_new, nc), F32)
            for h in range(NSA_HPG):
                r0 = s * rows1 + (g * NSA_HPG + h) * s_new
                acc = acc + p_c[r0:r0 + s_new]
            psum.append(acc)
    n_imp = nseq * NSA_GROUPS * s_new
    psum.append(jnp.zeros((LANES - n_imp, nc), F32))
    imp = _dot01(jnp.concatenate(psum, axis=0), ov_ref[...])
    cur = jnp.full((1, LANES), p_len // SLC_BLOCK, jnp.int32)
    sel = _topk_select(imp, cur)[0:n_imp]
    selneg = jnp.where(sel > 0.0, 0.0, NEG).astype(BF16)
    selneg_rows = [jnp.concatenate([selneg[(s * NSA_GROUPS + g) * s_new:(s * NSA_GROUPS + g + 1) * s_new]
                                    for g in range(NSA_GROUPS) for _ in range(NSA_HPG)], axis=0)
                   for s in seqs]

    e_hot = e_ref[...]
    s_past, s_nw, vts, news = [], [], [], []
    for s in seqs:
        kvt = jnp.concatenate([r[...] for r in page_refs[s * n_pages:(s + 1) * n_pages]],
                              axis=1).astype(BF16)
        kaug = jnp.concatenate([kvt[0:LANES], e_hot], axis=0)
        s_past.append(_dot(jnp.concatenate([q2[s], selneg_rows[s]], axis=1), kaug))
        ns = ns_ref[s].astype(BF16)
        s_nw.append(_dot_nt(q2[s], ns[:, 0:LANES]))
        vts.append(kvt[LANES:2 * LANES])
        news.append(ns)
    out = out + gate_rows(1) * attend(s_past, s_nw, vts, news)

    scol = _iota((rows1, w_buf), 1)
    tr1 = _iota((rows1, 1), 0) & (s_new - 1)
    wmask = (w_buf + tr1 - scol < WINDOW) & (p_len - w_buf + scol >= 0)
    s_b, s_nw, vts, news = [], [], [], []
    for s in seqs:
        wt = win_ref[s].astype(BF16)
        nw = nw_ref[s].astype(BF16)
        s_b.append(jnp.where(wmask, _dot(q2[s], wt[0:LANES]), NEG))
        s_nw.append(_dot_nt(q2[s], nw[:, 0:LANES]))
        vts.append(wt[LANES:2 * LANES])
        news.append(nw)
    out = out + gate_rows(2) * attend(s_b, s_nw, vts, news)

    lh = _iota((s_new, LANES), 1) >> 6
    for s in seqs:
        for pair in range(NSA_HEADS // 2):
            parts = []
            for hh in (2 * pair, 2 * pair + 1):
                x = out[s * rows1 + hh * s_new:s * rows1 + (hh + 1) * s_new]
                if hh % 2 != hh // NSA_HPG:
                    x = jnp.concatenate([x[:, HEAD_DIM:], x[:, :HEAD_DIM]], axis=1)
                parts.append(x)
            o_ref[s, :, pair * LANES:(pair + 1) * LANES] = jnp.where(lh == 0, parts[0], parts[1]).astype(o_ref.dtype)


def _nsa_sample(q, gates, ckv, cache_t, page_table, win_t, new_s, new_w, ov, e_all):
    bd, s_new, _ = q.shape
    n_pages = page_table.shape[1]
    p_len = n_pages * PAGE_SIZE
    assert s_new & (s_new - 1) == 0
    nseq = _tile_rows(bd, SAMPLE_SEQS_PER_STEP)
    full = lambda a: pl.BlockSpec(a.shape, lambda i, pt: (0, 0))
    seq = lambda a: pl.BlockSpec((nseq,) + a.shape[1:], lambda i, pt: (i, 0, 0))
    page = lambda s, k: pl.BlockSpec((None, KV_W, PAGE_SIZE), lambda i, pt, s=s, k=k: (pt[i * nseq + s, k], 0, 0))
    return pl.pallas_call(
        functools.partial(_nsa_sample_body, n_pages=n_pages, nseq=nseq, s_new=s_new, n_cmp=p_len // CMP_STRIDE - 1,
                          p_len=p_len),
        grid_spec=pltpu.PrefetchScalarGridSpec(
            num_scalar_prefetch=1,
            grid=(bd // nseq,),
            in_specs=[page(s, k) for s in range(nseq) for k in range(n_pages)]
            + [seq(q), seq(gates), seq(ckv), seq(win_t), seq(new_s), seq(new_w), full(ov), full(e_all)],
            out_specs=pl.BlockSpec((nseq, s_new, NQ), lambda i, pt: (i, 0, 0)),
        ),
        out_shape=jax.ShapeDtypeStruct((bd, s_new, NQ), BF16),
        compiler_params=_params("parallel"),
        name="nsa_sample",
    )(page_table, *([cache_t] * (nseq * n_pages)), q, gates, ckv, win_t, new_s, new_w, ov, e_all)


def _gla_body(q_ref, k_ref, v_ref, lg_ref, r_ref, gn_ref, s0_ref, o_ref, sfin_ref, s_sc, *, ns, tb, chunk, sub):
    j = pl.program_id(1)

    @pl.when(j == 0)
    def _():
        s_sc[...] = s0_ref[...]

    nchunk = tb // chunk
    nsub = chunk // sub
    ri = _iota((tb, tb), 0)
    ci = _iota((tb, tb), 1)
    tril = jnp.where((_div_pow2(ri, chunk) == _div_pow2(ci, chunk)) & (ci <= ri), 1.0, 0.0).astype(BF16)
    head_of_lane = _div_pow2(_iota((1, GQK), 1), GLA_DK)
    rows = _iota((chunk, GQK), 0)
    ones_dv = jnp.ones((chunk, GLA_DV), BF16)
    vhead = _div_pow2(_iota((chunk, GV), 1), GLA_DV)
    gn = gn_ref[...]

    units = [(si, c) for si in range(ns) for c in range(nchunk)]
    b_seq = [_dot01_l(tril, lg_ref[si]) for si in range(ns)]
    pre = {}
    for si, c in units:
        c0 = c * chunk
        bc = b_seq[si][c0:c0 + chunk]
        qc = q_ref[si, c0:c0 + chunk, :]
        kc = k_ref[si, c0:c0 + chunk, :]
        b_last = bc[chunk - 1:chunk]
        qdb = qc * jnp.exp(bc)
        kdl = kc * jnp.exp(b_last - bc)
        decay = jnp.exp(_dot01_tn(jnp.where(rows == chunk - 1, bc, 0.0), ones_dv))
        kds = []
        for sb in range(nsub):
            rr = sb * sub
            ref_row = bc[rr - 1:rr] if sb else jnp.zeros((1, GQK), F32)
            vis = rows < rr + sub
            kds.append((ref_row, jnp.where(vis, kc * jnp.exp(jnp.where(vis, ref_row - bc, 0.0)), 0.0).astype(BF16)))
        vc = v_ref[si, c0:c0 + chunk, :].astype(BF16)
        vbd = jnp.concatenate([jnp.where(vhead == h, vc, jnp.zeros_like(vc)) for h in range(GLA_HEADS)], axis=0)
        o_sub = []
        for sb in range(nsub):
            rr = sb * sub
            ref_row, kd = kds[sb]
            qd = qc[rr:rr + sub] * jnp.exp(bc[rr:rr + sub] - ref_row)
            qst = jnp.concatenate([jnp.where(head_of_lane == h, qd, 0.0) for h in range(GLA_HEADS)], axis=0)
            a = _dot_nt(qst.astype(BF16), kd)
            qrow = _iota((GLA_HEADS * sub, chunk), 0) & (sub - 1)
            a = jnp.where(_iota((GLA_HEADS * sub, chunk), 1) <= rr + qrow, a, 0.0)
            a_cat = jnp.concatenate([a[h * sub:(h + 1) * sub] for h in range(GLA_HEADS)],
                                    axis=1).astype(BF16)
            o_sub.append(a_cat)
        o_intra = _dot(o_sub[0] if nsub == 1 else jnp.concatenate(o_sub, axis=0), vbd)
        qdb_st = jnp.concatenate([jnp.where(head_of_lane == h, qdb, 0.0) for h in range(GLA_HEADS)],
                                 axis=0).astype(BF16)
        kv = _dot_tn(kdl.astype(BF16), vc)
        upd = jnp.concatenate([kv[h * GLA_DK:(h + 1) * GLA_DK, h * GLA_DV:(h + 1) * GLA_DV]
                               for h in range(GLA_HEADS)], axis=0)
        pre[si, c] = (decay, upd, o_intra, qdb_st)

    for si in range(ns):
        state = s_sc[si]
        for c in range(nchunk):
            c0 = c * chunk
            decay, upd, o_intra, qdb_st = pre[si, c]
            o_inter = _dot(qdb_st, state.astype(BF16))
            for h in range(GLA_HEADS):
                hs = slice(h * GLA_DV, (h + 1) * GLA_DV)
                o = o_inter[h * chunk:(h + 1) * chunk] + o_intra[:, hs]
                o_ref[si, c0:c0 + chunk, hs] = (_rms(o, gn) * r_ref[si, c0:c0 + chunk, hs].astype(F32)).astype(o_ref.dtype)
            state = decay * state + upd
        s_sc[si] = state

    @pl.when(j == pl.num_programs(1) - 1)
    def _():
        sfin_ref[...] = s_sc[...]


def _gla(q, k, v, lg, r, gn, s0, ns, tb, chunk):
    s, t, _ = q.shape
    sub = min(GLA_SUB, chunk)
    row = lambda w: pl.BlockSpec((ns, tb, w), lambda gi, j: (gi, j, 0))
    st = pl.BlockSpec((ns, GQK, GLA_DV), lambda gi, j: (gi, 0, 0))
    return pl.pallas_call(
        functools.partial(_gla_body, ns=ns, tb=tb, chunk=chunk, sub=sub),
        grid=(s // ns, t // tb),
        in_specs=[row(GQK), row(GQK), row(GV), row(GQK), row(GV), pl.BlockSpec(gn.shape, lambda gi, j: (0, 0)), st],
        out_specs=[row(GV), st],
        out_shape=[jax.ShapeDtypeStruct((s, t, GV), BF16), jax.ShapeDtypeStruct((s, GQK, GLA_DV), F32)],
        scratch_shapes=[pltpu.VMEM((ns, GQK, GLA_DV), F32)],
        compiler_params=_params("parallel", "arbitrary"),
        name="gla",
    )(q, k, v, lg, r, gn, s0)


def _mix_body(x_ref, sc_ref, sh_ref, ga_ref, gpre_ref, gpost_ref, on_ref, og_ref, wm_ref, wn_ref, wgo_ref, wo_ref,
              o_ref):
    x = x_ref[...]
    d = x.shape[1]
    h = _norm_mod(x, gpre_ref[...], sc_ref[...], sh_ref[...]).astype(BF16)
    m = _sigmoid(_dot(h, wm_ref[...]))
    y_a = _dot(on_ref[...], wn_ref[...])
    y_b = _dot(og_ref[...], wgo_ref[...])
    mixin = (m[:, 0:d] * y_a + m[:, d:2 * d] * y_b).astype(BF16)
    mix = _dot(mixin, wo_ref[...])
    o_ref[...] = x + ga_ref[...] * _rms(mix, gpost_ref[...])


def _mix(x, sc, sh, ga, gpre, gpost, o_nsa, o_gla, wm, wn, wgo, wo, tm, rows_per_mod):
    n, d = x.shape
    row = lambda w: pl.BlockSpec((tm, w), lambda i: (i, 0))
    full = lambda a: pl.BlockSpec(a.shape, lambda i: (0, 0))
    ms = lambda a: _mod_spec(a, tm, rows_per_mod)
    return pl.pallas_call(
        _mix_body,
        grid=(n // tm,),
        in_specs=[row(d), ms(sc), ms(sh), ms(ga), full(gpre), full(gpost), row(NQ), row(GV),
                  full(wm), full(wn), full(wgo), full(wo)],
        out_specs=row(d),
        out_shape=jax.ShapeDtypeStruct((n, d), F32),
        compiler_params=_params("parallel"),
        name="mix",
    )(x, sc, sh, ga, gpre, gpost, o_nsa, o_gla, wm, wn, wgo, wo)


def _ffn_body(x_ref, sc_ref, sh_ref, ga_ref, gpre_ref, gpost_ref, wg_ref, wu_ref, wd_ref, o_ref, h_sc, acc_sc):
    j = pl.program_id(1)

    @pl.when(j == 0)
    def _():
        h_sc[...] = _norm_mod(x_ref[...], gpre_ref[...], sc_ref[...], sh_ref[...]).astype(BF16)
        acc_sc[...] = jnp.zeros(acc_sc.shape, F32)

    h = h_sc[...]
    gt = _dot(h, wg_ref[...])
    up = _dot(h, wu_ref[...])
    a = (gt * _sigmoid(gt) * up).astype(BF16)
    acc_sc[...] += _dot(a, wd_ref[...])

    @pl.when(j == pl.num_programs(1) - 1)
    def _():
        o_ref[...] = x_ref[...] + ga_ref[...] * _rms(acc_sc[...], gpost_ref[...])


def _ffn(x, sc, sh, ga, gpre, gpost, wg, wu, wd, tm, rows_per_mod):
    n, d = x.shape
    dff = wg.shape[1]
    tf = dff // 2 if (dff // 2) % LANES == 0 else dff
    row = pl.BlockSpec((tm, d), lambda i, j: (i, 0))
    full = lambda a: pl.BlockSpec(a.shape, lambda i, j: (0, 0))
    ms = lambda a: pl.BlockSpec((None, a.shape[1], a.shape[2]), lambda i, j: ((i * tm) // rows_per_mod, 0, 0))
    return pl.pallas_call(
        _ffn_body,
        grid=(n // tm, dff // tf),
        in_specs=[row, ms(sc), ms(sh), ms(ga), full(gpre), full(gpost),
                  pl.BlockSpec((d, tf), lambda i, j: (0, j)), pl.BlockSpec((d, tf), lambda i, j: (0, j)),
                  pl.BlockSpec((tf, d), lambda i, j: (j, 0))],
        out_specs=row,
        out_shape=jax.ShapeDtypeStruct((n, d), F32),
        scratch_shapes=[pltpu.VMEM((tm, d), BF16), pltpu.VMEM((tm, d), F32)],
        compiler_params=_params("parallel", "arbitrary"),
        name="ffn",
    )(x, sc, sh, ga, gpre, gpost, wg, wu, wd)


def _split_w_in(w_in, d):
    sizes = (NQ, KV_W, KV_W, KV_W, 3 * NSA_HEADS, GQK, GQK, GV, 16, GV, 2 * d)
    out, o = [], 0
    for s in sizes:
        out.append(w_in[:, o:o + s])
        o += s
    return out


def _pad_cols(w, n):
    return jnp.pad(w, ((0, 0), (0, n - w.shape[1])))


def _compress_weights(w1k, b1k, w2k, pek, w1v, b1v, w2v, pev):
    hid = w2k.shape[0]
    npair = CMP_STRIDE // 2
    eye = np.eye(NSA_GROUPS, dtype=np.float32)

    def group_diag(w):
        out = jnp.einsum('gh,...ab->...gahb', eye, w)
        return out.reshape(w.shape[:-2] + (NSA_GROUPS * w.shape[-2], NSA_GROUPS * w.shape[-1]))

    def expand_w1(half):
        per_kv = [group_diag(w1.reshape(CMP_LEN, HEAD_DIM, hid)[half * CMP_STRIDE:(half + 1) * CMP_STRIDE])
                  for w1 in (w1k, w1v)]
        return jnp.stack(per_kv).reshape(2, npair, 2 * NSA_GROUPS * HEAD_DIM, NSA_GROUPS * hid)

    w1 = jnp.concatenate([expand_w1(0), expand_w1(1)], axis=-1).astype(BF16)

    def expand_pe(half):
        per_kv = [jnp.tile(pe[half * CMP_STRIDE:(half + 1) * CMP_STRIDE], (1, NSA_GROUPS)) for pe in (pek, pev)]
        return jnp.stack(per_kv).reshape(2, npair, 2 * NSA_GROUPS * HEAD_DIM)

    pe = jnp.pad(jnp.stack([expand_pe(0), expand_pe(1)], axis=2),
                 ((0, 0), (0, 0), (0, PE_ROWS - 2), (0, 0))).astype(BF16)

    b1 = jnp.concatenate([jnp.tile(b1k, NSA_GROUPS), jnp.tile(b1v, NSA_GROUPS)]).reshape(1, -1)
    w2d = [group_diag(w2k), group_diag(w2v)]
    zero = jnp.zeros_like(w2d[0])
    w2 = jnp.concatenate([jnp.concatenate([w2d[0], zero], axis=1),
                          jnp.concatenate([zero, w2d[1]], axis=1)], axis=0).astype(BF16)
    return pe, w1, b1, w2


def _overlap(n_cmp_pad, n_cmp, nb):
    n = np.arange(n_cmp_pad)[:, None]
    j = np.arange(nb)[None, :] * SLC_BLOCK
    ok = (n * CMP_STRIDE < j + SLC_BLOCK) & (n * CMP_STRIDE + CMP_LEN > j) & (n < n_cmp)
    return ok.astype(BF16)


def _block_onehot(nb, nkeys):
    return (np.arange(nb)[:, None] == (np.arange(nkeys)[None, :] // SLC_BLOCK)).astype(BF16)


def _tile_rows(n, cap):
    t = cap
    while n % t:
        t //= 2
    return t


def _feature_major(a):
    nd = a.ndim
    a = jnp.moveaxis(a, nd - 4, nd - 1)
    return a.reshape(a.shape[:nd - 4] + (KV_W, a.shape[-1]))


def _row_major(a_t):
    b, _, r = a_t.shape
    return jnp.moveaxis(a_t.reshape(b, 2, NSA_GROUPS, HEAD_DIM, r), 4, 1)


def kernel(x_prompt, x_sample, cache_cmp_kv, cache_slc_kv, state_win_kv, state_gla, page_table, c_prompt, c_sample, w_ada, b_ada, g_pre_mix, g_post_mix, w_in, cmp_w1_k, cmp_b1_k, cmp_w2_k, cmp_pe_k, cmp_w1_v, cmp_b1_v, cmp_w2_v, cmp_pe_v, w_gla_a2, b_gla_a, gla_norm_g, w_nsa_o, w_gla_o, w_out, g_pre_ffn, g_post_ffn, w_ff_gate, w_ff_up, w_ff_down):
    depth = w_ada.shape[0]
    bp, l, d = x_prompt.shape
    bd, s_new, _ = x_sample.shape
    n_pages = page_table.shape[1]
    p_len = n_pages * PAGE_SIZE
    w_buf = state_win_kv.shape[2]
    keep = min(WINDOW, l)
    np_, ns_ = bp * l, bd * s_new
    tmp = _tile_rows(np_ // bp, 512)
    tms = _tile_rows(ns_, 512)
    tq = _tile_rows(l, 256)
    tk = 2 * tq

    yp, ys = x_prompt.reshape(np_, d), x_sample.reshape(ns_, d)
    col = [[] for _ in range(8)]
    for li in range(depth):
        r_all = bp + bd
        r_pad = -(-r_all // 8) * 8
        c_all = jnp.pad(jnp.concatenate([c_prompt, c_sample], axis=0), ((0, r_pad - r_all), (0, 0)))
        mod = _ada(c_all, w_ada[li], b_ada[li])
        mods_p = [m.reshape(bp, 1, d) for m in jnp.split(mod[:bp], 6, axis=-1)]
        mods_s = [jnp.repeat(m, s_new, axis=0).reshape(ns_ // tms, tms, d) for m in jnp.split(mod[bp:r_all], 6, axis=-1)]

        (w_q, w_kc, w_ks, w_kw, w_gn, w_qg, w_kg, w_vg, w_ag, w_rg, w_mg) = _split_w_in(w_in[li], d)
        wq = w_q.astype(BF16)
        wkv = jnp.concatenate([w_kc, w_ks, w_kw], axis=1).astype(BF16)
        wgn = _pad_cols(w_gn, LANES).astype(BF16)
        wqkv = jnp.concatenate([w_qg, w_kg, w_vg], axis=1).astype(BF16)
        wa = _pad_cols(w_ag, LANES).astype(BF16)
        wr = w_rg.astype(BF16)
        a2 = jnp.pad(w_gla_a2[li], ((0, LANES - w_gla_a2.shape[1]), (0, 0)))
        ba = b_gla_a[li].reshape(1, GQK)
        cw = _compress_weights(cmp_w1_k[li], cmp_b1_k[li], cmp_w2_k[li], cmp_pe_k[li],
                               cmp_w1_v[li], cmp_b1_v[li], cmp_w2_v[li], cmp_pe_v[li])
        gpre = g_pre_mix[li].reshape(1, d)
        gpost = g_post_mix[li].reshape(1, d)
        gpre2 = g_pre_ffn[li].reshape(1, d)
        gpost2 = g_post_ffn[li].reshape(1, d)
        gn = gla_norm_g[li].reshape(1, GLA_DV)
        wm = w_mg.astype(BF16)
        wn = w_nsa_o[li].astype(BF16)
        wgo = w_gla_o[li].astype(BF16)
        wo = w_out[li].astype(BF16)
        wfg = w_ff_gate[li].astype(BF16)
        wfu = w_ff_up[li].astype(BF16)
        wfd = w_ff_down[li].astype(BF16)

        sh1, sc1, ga1, sh2, sc2, ga2 = mods_p
        (q, kvc, _, _, kvs_b, kvw_b, gates, kvc_t, kvs_t, kvw_t) = _pre_nsa(yp, sc1, sh1, gpre, wq, wkv, wgn, tmp, l,
                                                                            feature_major=(bp, l))
        qg, kg, vg, lg, rg = _pre_gla(yp, sc1, sh1, gpre, wqkv, wa, wr, a2, ba, tmp, l)
        ckv = _cmp_prompt(kvc.reshape(bp, l, KV_W), cw)
        nch = l // CMP_STRIDE
        nb = l // SLC_BLOCK
        o_nsa = _nsa_prompt(q.reshape(bp, l, NQ), gates.reshape(bp, l, LANES), ckv, kvs_b.reshape(bp, l, KV_W),
                            kvw_b.reshape(bp, l, KV_W), _overlap(nch, nch - 1, nb), _block_onehot(nb, l).T, tq, tk)
        chunk = GLA_CHUNK if l % GLA_CHUNK == 0 else l
        tb = 2 * LANES if (l % (2 * LANES) == 0 and LANES % chunk == 0) else chunk
        o_gla, s_fin_p = _gla(qg.reshape(bp, l, GQK), kg.reshape(bp, l, GQK), vg.reshape(bp, l, GV),
                              lg.reshape(bp, l, GQK), rg.reshape(bp, l, GV), gn,
                              jnp.zeros((bp, GQK, GLA_DV), F32), bp, tb, chunk)
        x1 = _mix(yp, sc1, sh1, ga1, gpre, gpost, o_nsa.reshape(np_, NQ), o_gla.reshape(np_, GV), wm, wn, wgo, wo, tmp, l)
        yp = _ffn(x1, sc2, sh2, ga2, gpre2, gpost2, wfg, wfu, wfd, tmp, l)
        col[0].append(_row_major(kvc_t))
        col[2].append(_row_major(kvs_t))
        col[4].append(_row_major(kvw_t[:, :, l - keep:]))
        col[6].append(s_fin_p.reshape(bp, GLA_HEADS, GLA_DK, GLA_DV).astype(state_gla.dtype))

        sh1, sc1, ga1, sh2, sc2, ga2 = mods_s
        q, kvc, kvs, kvw, _, _, gates = _pre_nsa(ys, sc1, sh1, gpre, wq, wkv, wgn, tms, tms)
        qg, kg, vg, lg, rg = _pre_gla(ys, sc1, sh1, gpre, wqkv, wa, wr, a2, ba, tms, tms)
        ckv = _cmp_sample(_feature_major(cache_cmp_kv[li]), page_table, cw)
        nch = p_len // CMP_STRIDE
        nb = p_len // SLC_BLOCK
        win_t = _feature_major(state_win_kv[li])
        kvw3 = kvw.reshape(bd, s_new, KV_W)
        o_nsa = _nsa_sample(q.reshape(bd, s_new, NQ), gates.reshape(bd, s_new, LANES), ckv,
                            _feature_major(cache_slc_kv[li]), page_table, win_t, kvs.reshape(bd, s_new, KV_W), kvw3,
                            _overlap(nch, nch - 1, nb), _block_onehot(nb, p_len))
        chunk = GLA_CHUNK if s_new % GLA_CHUNK == 0 else s_new
        o_gla, s_fin_s = _gla(qg.reshape(bd, s_new, GQK), kg.reshape(bd, s_new, GQK), vg.reshape(bd, s_new, GV),
                              lg.reshape(bd, s_new, GQK), rg.reshape(bd, s_new, GV), gn,
                              state_gla[li].astype(F32).reshape(bd, GQK, GLA_DV), _tile_rows(bd, 16), chunk, chunk)
        x1 = _mix(ys, sc1, sh1, ga1, gpre, gpost, o_nsa.reshape(ns_, NQ), o_gla.reshape(ns_, GV), wm, wn, wgo, wo, tms, tms)
        ys = _ffn(x1, sc2, sh2, ga2, gpre2, gpost2, wfg, wfu, wfd, tms, tms)
        win_new_t = jnp.concatenate([win_t, kvw3.transpose(0, 2, 1)], axis=2)[:, :, s_new:]
        win_new = jnp.moveaxis(win_new_t.reshape(bd, 2, NSA_GROUPS, HEAD_DIM, w_buf), 4, 1)
        col[1].append(kvc.reshape(bd, s_new, 2, NSA_GROUPS, HEAD_DIM))
        col[3].append(kvs.reshape(bd, s_new, 2, NSA_GROUPS, HEAD_DIM))
        col[5].append(win_new)
        col[7].append(s_fin_s.reshape(bd, GLA_HEADS, GLA_DK, GLA_DV).astype(state_gla.dtype))

    stacked = [c[0][None] if len(c) == 1 else jnp.stack(c) for c in col]
    return (yp.reshape(bp, l, d), ys.reshape(bd, s_new, d), *stacked)
```

```python
import functools
import math

import numpy as np
import jax
import jax.numpy as jnp
from jax import lax
from jax.experimental import pallas as pl
from jax.experimental.pallas import tpu as pltpu

F32 = jnp.float32
BF16 = jnp.bfloat16

NSA_HEADS = 8
NSA_GROUPS = 2
NSA_HPG = NSA_HEADS // NSA_GROUPS
HEAD_DIM = 64
CMP_STRIDE = 16
CMP_LEN = 32
SLC_BLOCK = 64
N_SEL = 16
WINDOW = 512
GLA_HEADS = 4
GLA_DK = 64
GLA_DV = 128
GLA_TAU = 16.0
GLA_CHUNK = 64
GLA_SUB = 16
PE_ROWS = 16
SAMPLE_SEQS_PER_STEP = 2
EPS = 1e-6
PAGE_SIZE = 128
NEG = -1e30
LOG2E = math.log2(math.e)
LANES = 128
VMEM_LIMIT = 56 * 1024 * 1024

KV_W = 2 * NSA_GROUPS * HEAD_DIM
NQ = NSA_HEADS * HEAD_DIM
GQK = GLA_HEADS * GLA_DK
GV = GLA_HEADS * GLA_DV


def _dot(a, b):
    return jnp.dot(a, b, preferred_element_type=F32)


def _dot_nt(a, b):
    return lax.dot_general(a, b, (((1,), (1,)), ((), ())), preferred_element_type=F32)


def _dot_f32(a, b):
    return jnp.dot(a, b, preferred_element_type=F32, precision=lax.Precision.HIGHEST)


def _dot01(x, m01):
    hi = x.astype(BF16)
    r1 = x - hi.astype(F32)
    mid = r1.astype(BF16)
    lo = (r1 - mid.astype(F32)).astype(BF16)
    return _dot(hi, m01) + _dot(mid, m01) + _dot(lo, m01)


def _dot01_l(m01, x):
    hi = x.astype(BF16)
    r1 = x - hi.astype(F32)
    mid = r1.astype(BF16)
    lo = (r1 - mid.astype(F32)).astype(BF16)
    return _dot(m01, hi) + _dot(m01, mid) + _dot(m01, lo)


def _dot_tn(a, b):
    return lax.dot_general(a, b, (((0,), (0,)), ((), ())), preferred_element_type=F32)


def _dot01_tn(x, m01):
    hi = x.astype(BF16)
    r1 = x - hi.astype(F32)
    mid = r1.astype(BF16)
    lo = (r1 - mid.astype(F32)).astype(BF16)
    return _dot_tn(hi, m01) + _dot_tn(mid, m01) + _dot_tn(lo, m01)


def _sigmoid(x):
    return 1.0 / (1.0 + jnp.exp(-x))


def _params(*sem):
    return pltpu.CompilerParams(dimension_semantics=sem, vmem_limit_bytes=VMEM_LIMIT)


def _iota(shape, dim):
    return lax.broadcasted_iota(jnp.int32, shape, dim)


def _div_pow2(x, n):
    assert n & (n - 1) == 0, n
    return x >> (n.bit_length() - 1)


def _ada_body(c_ref, w_ref, b_ref, o_ref):
    o_ref[...] = _dot_f32(c_ref[...], w_ref[...]) + b_ref[...]


def _ada(c_all, w_ada, b_ada):
    r, d = c_all.shape
    n = w_ada.shape[1]
    tn = 1536 if n % 1536 == 0 else n
    return pl.pallas_call(
        _ada_body,
        grid=(n // tn,),
        in_specs=[pl.BlockSpec((r, d), lambda j: (0, 0)),
                  pl.BlockSpec((d, tn), lambda j: (0, j)),
                  pl.BlockSpec((1, tn), lambda j: (0, j))],
        out_specs=pl.BlockSpec((r, tn), lambda j: (0, j)),
        out_shape=jax.ShapeDtypeStruct((r, n), F32),
        compiler_params=_params("arbitrary"),
        name="ada",
    )(c_all, w_ada, b_ada.reshape(1, n))


def _norm_mod(x, g, sc, sh):
    ms = jnp.mean(x * x, axis=-1, keepdims=True)
    y = x * lax.rsqrt(ms + EPS) * g
    return y * (1.0 + sc) + sh


def _rms(x, g):
    ms = jnp.mean(x * x, axis=-1, keepdims=True)
    return x * lax.rsqrt(ms + EPS) * g


def _pre_nsa_body(x_ref, sc_ref, sh_ref, g_ref, wq_ref, wkv_ref, wg_ref,
                  q_ref, kvc_ref, kvs_ref, kvw_ref, kvsb_ref, kvwb_ref, gate_ref, *t_refs):
    h = _norm_mod(x_ref[...], g_ref[...], sc_ref[...], sh_ref[...]).astype(BF16)
    q_ref[...] = (_dot(h, wq_ref[...]) * (HEAD_DIM ** -0.5 * LOG2E)).astype(BF16)
    kv = _dot(h, wkv_ref[...])
    kvc = kv[:, 0:KV_W]
    kvs = kv[:, KV_W:2 * KV_W]
    kvw = kv[:, 2 * KV_W:3 * KV_W]
    kvc_ref[...] = kvc
    kvs_ref[...] = kvs
    kvw_ref[...] = kvw
    kvsb_ref[...] = kvs.astype(BF16)
    kvwb_ref[...] = kvw.astype(BF16)
    gate_ref[...] = _sigmoid(_dot(h, wg_ref[...]))
    if t_refs:
        kvt = kv.T
        for j, t_ref in enumerate(t_refs):
            t_ref[...] = kvt[j * KV_W:(j + 1) * KV_W]


def _mod_spec(mod3, tm, rows_per_mod):
    r = mod3.shape[1]
    return pl.BlockSpec((None, r, mod3.shape[2]), lambda i: ((i * tm) // rows_per_mod, 0, 0))


def _pre_nsa(x, sc, sh, g, wq, wkv, wg, tm, rows_per_mod, feature_major=None):
    n, d = x.shape
    row = lambda w: pl.BlockSpec((tm, w), lambda i: (i, 0))
    full = lambda a: pl.BlockSpec(a.shape, lambda i: (0, 0))
    outs = [(NQ, BF16), (KV_W, F32), (KV_W, F32), (KV_W, F32), (KV_W, BF16), (KV_W, BF16), (LANES, F32)]
    out_specs = [row(w) for w, _ in outs]
    out_shape = [jax.ShapeDtypeStruct((n, w), t) for w, t in outs]
    if feature_major is not None:
        nbatch, per = feature_major
        tiles = per // tm
        out_specs += [pl.BlockSpec((None, KV_W, tm), lambda i: (i // tiles, 0, i % tiles))] * 3
        out_shape += [jax.ShapeDtypeStruct((nbatch, KV_W, per), F32)] * 3
    return pl.pallas_call(
        _pre_nsa_body,
        grid=(n // tm,),
        in_specs=[row(d), _mod_spec(sc, tm, rows_per_mod), _mod_spec(sh, tm, rows_per_mod), full(g),
                  full(wq), full(wkv), full(wg)],
        out_specs=out_specs,
        out_shape=out_shape,
        compiler_params=_params("parallel"),
        name="pre_nsa",
    )(x, sc, sh, g, wq, wkv, wg)


def _log_sigmoid(x):
    return jnp.minimum(x, 0.0) - jnp.log(1.0 + jnp.exp(-jnp.abs(x)))


def _pre_gla_body(x_ref, sc_ref, sh_ref, g_ref, wqkv_ref, wa_ref, wr_ref, a2_ref, ba_ref,
                  q_ref, k_ref, v_ref, lg_ref, r_ref):
    h = _norm_mod(x_ref[...], g_ref[...], sc_ref[...], sh_ref[...]).astype(BF16)
    qkv = _dot(h, wqkv_ref[...])
    q_ref[...] = qkv[:, 0:GQK] * GLA_DK ** -0.5
    k_ref[...] = qkv[:, GQK:2 * GQK]
    v_ref[...] = qkv[:, 2 * GQK:2 * GQK + GV]
    a = _dot(h, wa_ref[...])
    lg_ref[...] = _log_sigmoid(_dot_f32(a, a2_ref[...]) + ba_ref[...]) * (1.0 / GLA_TAU)
    r = _dot(h, wr_ref[...])
    r_ref[...] = (r * _sigmoid(r)).astype(BF16)


def _pre_gla(x, sc, sh, g, wqkv, wa, wr, a2, ba, tm, rows_per_mod):
    n, d = x.shape
    row = lambda w: pl.BlockSpec((tm, w), lambda i: (i, 0))
    full = lambda a: pl.BlockSpec(a.shape, lambda i: (0, 0))
    return pl.pallas_call(
        _pre_gla_body,
        grid=(n // tm,),
        in_specs=[row(d), _mod_spec(sc, tm, rows_per_mod), _mod_spec(sh, tm, rows_per_mod), full(g),
                  full(wqkv), full(wa), full(wr), full(a2), full(ba)],
        out_specs=[row(GQK), row(GQK), row(GV), row(GQK), row(GV)],
        out_shape=[jax.ShapeDtypeStruct((n, GQK), F32), jax.ShapeDtypeStruct((n, GQK), F32),
                   jax.ShapeDtypeStruct((n, GV), F32), jax.ShapeDtypeStruct((n, GQK), F32),
                   jax.ShapeDtypeStruct((n, GV), BF16)],
        compiler_params=_params("parallel"),
        name="pre_gla",
    )(x, sc, sh, g, wqkv, wa, wr, a2, ba)


def _gelu_tanh(x):
    return 0.5 * x * (1.0 + jnp.tanh(0.7978845608028654 * (x + 0.044715 * x * x * x)))


def _compress(get_xp, nch, pe_ref, w1_ref, b1_ref, w2_ref):
    first, second = [], []
    for kv in range(2):
        acc = jnp.zeros((nch + PE_ROWS, 2 * LANES), F32)
        for pp in range(CMP_STRIDE // 2):
            lhs = jnp.concatenate([get_xp(kv, 2 * pp), get_xp(kv, 2 * pp + 1)], axis=1).astype(BF16)
            acc = acc + _dot(jnp.concatenate([lhs, pe_ref[kv, pp]], axis=0), w1_ref[kv, pp])
        first.append(acc[0:nch, 0:LANES] + acc[nch:nch + 1, 0:LANES])
        second.append(acc[0:nch, LANES:2 * LANES] + acc[nch + 1:nch + 2, LANES:2 * LANES])
    first = jnp.concatenate(first, axis=1)
    second = jnp.concatenate(second, axis=1)
    if nch % 8 == 0:
        nxt = pltpu.roll(second, nch - 1, 0)
    else:
        nxt = jnp.concatenate([second[1:], second[:1]], axis=0)
    hid = _gelu_tanh(first + nxt + b1_ref[...])
    out = _dot(hid.astype(BF16), w2_ref[...])
    return jnp.where(_iota(out.shape, 0) < nch - 1, out, 0.0)


def _cmp_prompt_body(xk_ref, xv_ref, pe_ref, w1_ref, b1_ref, w2_ref, o_ref, *, nch):
    x_refs = (xk_ref, xv_ref)

    def get_xp(kv, p):
        return x_refs[kv][pl.ds(p, nch, stride=CMP_STRIDE), :]

    o_ref[...] = _compress(get_xp, nch, pe_ref, w1_ref, b1_ref, w2_ref).astype(o_ref.dtype)


def _cmp_prompt(kvc, cw):
    b, l, _ = kvc.shape
    nch = l // CMP_STRIDE
    full = lambda a: pl.BlockSpec(a.shape, lambda i: (0,) * a.ndim)
    return pl.pallas_call(
        functools.partial(_cmp_prompt_body, nch=nch),
        grid=(b,),
        in_specs=[pl.BlockSpec((None, l, LANES), lambda i: (i, 0, 0)),
                  pl.BlockSpec((None, l, LANES), lambda i: (i, 0, 1))] + [full(a) for a in cw],
        out_specs=pl.BlockSpec((None, nch, KV_W), lambda i: (i, 0, 0)),
        out_shape=jax.ShapeDtypeStruct((b, nch, KV_W), BF16),
        compiler_params=_params("parallel"),
        name="cmp_prompt",
    )(kvc, kvc, *cw)


def _cmp_sample_body(pt_ref, cache_ref, pe_ref, w1_ref, b1_ref, w2_ref, o_ref,
                     page_sc, ak_sc, av_sc, bk_sc, bv_sc, sem, *, n_seq, n_pages, nch):
    i = pl.program_id(0)
    cpp = PAGE_SIZE // CMP_STRIDE
    bufs = ((ak_sc, av_sc), (bk_sc, bv_sc))

    def page_copy(seq, slot, k):
        return pltpu.make_async_copy(cache_ref.at[pt_ref[seq, k]], page_sc.at[slot, k], sem.at[slot])

    def fetch(seq, slot):
        for k in range(n_pages):
            page_copy(seq, slot, k).start()

    def wait(slot):
        for k in range(n_pages):
            pltpu.make_async_copy(cache_ref.at[0], page_sc.at[slot, k], sem.at[slot]).wait()

    def regroup(slot):
        for k in range(n_pages):
            for kv in range(2):
                xt = page_sc[slot, k, kv * LANES:(kv + 1) * LANES, :].T
                for n in range(cpp):
                    for a in range(CMP_STRIDE // 8):
                        r0 = n * CMP_STRIDE + 8 * a
                        bufs[slot][kv][k, pl.ds(8 * a * cpp + n, 8, stride=cpp), :] = xt[r0:r0 + 8, :]

    def mlp(src):
        def get_xp(kv, p):
            return src[kv][:, p * cpp:(p + 1) * cpp, :].reshape(nch, LANES)

        o_ref[...] = _compress(get_xp, nch, pe_ref, w1_ref, b1_ref, w2_ref).astype(o_ref.dtype)

    @pl.when(i == 0)
    def _():
        fetch(0, 0)
        if n_seq == 1:
            page_sc[1] = jnp.zeros(page_sc.shape[1:], F32)
        bk_sc[...] = jnp.zeros(bk_sc.shape, F32)
        bv_sc[...] = jnp.zeros(bv_sc.shape, F32)

    for slot in range(2):
        @pl.when(i % 2 == slot)
        def _():
            @pl.when(i + 1 < n_seq)
            def _():
                fetch(i + 1, 1 - slot)

            @pl.when(i < n_seq)
            def _():
                wait(slot)

            regroup(slot)
            mlp(bufs[1 - slot])


def _cmp_sample(cache_t, page_table, cw):
    bd, n_pages = page_table.shape
    p_len = n_pages * PAGE_SIZE
    nch = p_len // CMP_STRIDE
    full = lambda a: pl.BlockSpec(a.shape, lambda i, pt: (0,) * a.ndim)
    return pl.pallas_call(
        functools.partial(_cmp_sample_body, n_seq=bd, n_pages=n_pages, nch=nch),
        grid_spec=pltpu.PrefetchScalarGridSpec(
            num_scalar_prefetch=1,
            grid=(bd + 1,),
            in_specs=[pl.BlockSpec(memory_space=pl.ANY)] + [full(a) for a in cw],
            out_specs=pl.BlockSpec((None, nch, KV_W), lambda i, pt: (jnp.maximum(i - 1, 0), 0, 0)),
            scratch_shapes=[pltpu.VMEM((2, n_pages, KV_W, PAGE_SIZE), F32)]
            + [pltpu.VMEM((n_pages, PAGE_SIZE, LANES), F32)] * 4 + [pltpu.SemaphoreType.DMA((2,))],
        ),
        out_shape=jax.ShapeDtypeStruct((bd, nch, KV_W), BF16),
        compiler_params=_params("arbitrary"),
        name="cmp_sample",
    )(page_table, cache_t, *cw)


def _head_queries(q, rows):
    half = _iota((rows, LANES), 1) >> 6
    out = []
    for hh in range(NSA_HEADS):
        g = hh // NSA_HPG
        blk = q[:, (hh // 2) * LANES:(hh // 2 + 1) * LANES].astype(F32)
        if hh % 2 != g:
            blk = pltpu.roll(blk, HEAD_DIM, 1) if rows % 8 == 0 else jnp.roll(blk, HEAD_DIM, 1)
        out.append(jnp.where(half == g, blk, 0.0).astype(BF16))
    return out


def _topk_select(imp, cur):
    impt = imp.T
    nb = impt.shape[0]
    jl = _iota(impt.shape, 0)
    jf = jl.astype(F32)
    forced = (jl == 0) | (jl == cur) | (jl == cur - 1)
    elig = (jl >= 1) & (jl <= cur - 2)
    dead = -1.0

    def step(_, vals):
        m = jnp.max(vals, axis=0, keepdims=True)
        idx = jnp.min(jnp.where(vals == m, jf, float(nb)), axis=0, keepdims=True)
        return jnp.where(jf == idx, dead, vals)

    vals = lax.fori_loop(0, N_SEL - 3, step, jnp.where(elig, impt, dead))
    return jnp.where(forced | (elig & (vals == dead)), 1.0, 0.0).T


def _masked_softmax2(s, valid):
    s = jnp.where(valid, s, NEG)
    m = jnp.max(s, axis=-1, keepdims=True)
    e = jnp.where(valid, jnp.exp2(s - m), 0.0)
    d = jnp.sum(e, axis=-1, keepdims=True)
    return e * (1.0 / jnp.where(d > 0.0, d, 1.0))


def _nsa_prompt_body(q_ref, gate_ref, ckv_ref, kvs_ref, kvw_ref, ov_ref, et_ref, o_ref, qa_sc, m_sc, acc_sc, out_sc,
                     *, tq, tk, n_cmp):
    i = pl.program_id(1)
    t0 = i * tq
    nc = ckv_ref.shape[0]
    nb = ov_ref.shape[1]
    nh = NSA_HEADS
    qh = _head_queries(q_ref[...], tq)
    for hh in range(nh):
        qa_sc[hh * tq:(hh + 1) * tq, 0:LANES] = qh[hh]
    half = _iota((tq, LANES), 1) >> 6
    trow = t0 + _iota((1, tq, 1), 1)
    gates = gate_ref[...]

    def gate(branch, hh):
        c = branch * nh + hh
        return gates[:, c:c + 1]

    ck = ckv_ref[:, 0:LANES]
    cv = ckv_ref[:, LANES:2 * LANES]
    s = _dot_nt(qa_sc[:, 0:LANES], ck).reshape(nh, tq, nc)
    ncol = _iota((1, tq, nc), 2)
    cvalid = (ncol * CMP_STRIDE + CMP_LEN <= trow + 1) & (ncol < n_cmp)
    p = _masked_softmax2(s, cvalid)
    for hh in range(nh):
        out_sc[hh] = gate(0, hh) * _dot(p[hh].astype(BF16), cv)
    ov = ov_ref[...]
    imp = []
    for g in range(NSA_GROUPS):
        psum = p[g * NSA_HPG]
        for h in range(1, NSA_HPG):
            psum = psum + p[g * NSA_HPG + h]
        imp.append(_dot01(psum, ov))
    imp = jnp.concatenate(imp, axis=0)
    cur = (t0 + (_iota((1, NSA_GROUPS * tq), 1) & (tq - 1))) >> 6
    sel = _topk_select(imp, cur)
    selneg = jnp.where(sel > 0.0, 0.0, NEG).astype(BF16)
    for hh in range(nh):
        g = hh // NSA_HPG
        qa_sc[hh * tq:(hh + 1) * tq, LANES:LANES + nb] = selneg[g * tq:(g + 1) * tq]

    def load_v(ref, start, width):
        v128 = ref[pl.ds(start, width), LANES:2 * LANES]
        lane_half = _iota(v128.shape, 1) >> 6
        return [jnp.where(lane_half == g, v128, jnp.ones_like(v128)) for g in range(NSA_GROUPS)]

    def online(s3, vaug, first):
        width = s3.shape[-1]
        tmax = jnp.max(s3, axis=-1, keepdims=True)
        if first:
            m_new = jnp.broadcast_to(tmax, (nh, tq, LANES))
        else:
            m_old = m_sc[...]
            m_new = jnp.maximum(m_old, tmax)
            alpha = jnp.exp2(m_old - m_new)
        pexp = jnp.concatenate([jnp.exp2((s3[:, :, c * LANES:(c + 1) * LANES] - m_new).astype(BF16))
                                for c in range(width // LANES)], axis=-1)
        for hh in range(nh):
            pv = _dot(pexp[hh], vaug[hh // NSA_HPG])
            acc_sc[hh] = pv if first else alpha[hh] * acc_sc[hh] + pv
        m_sc[...] = m_new

    def finish(branch):
        for hh in range(nh):
            acc = acc_sc[hh]
            den = pltpu.roll(acc, HEAD_DIM, 1)
            out_sc[hh] = out_sc[hh] + gate(branch, hh) * (acc / den)

    def slc_scores(start):
        kaug = jnp.concatenate([kvs_ref[pl.ds(start, tk), 0:LANES], et_ref[pl.ds(start, tk), :]], axis=1)
        return _dot_nt(qa_sc[...], kaug).reshape(nh, tq, tk)

    n_full = i // (tk // tq)
    last = pl.multiple_of(n_full * tk, tk)
    s3 = slc_scores(last)
    s3 = jnp.where(last + _iota((1, tq, tk), 2) <= trow, s3, NEG)
    online(s3, load_v(kvs_ref, last, tk), True)

    def slc_step(j, carry):
        start = pl.multiple_of(j * tk, tk)
        online(slc_scores(start), load_v(kvs_ref, start, tk), False)
        return carry

    lax.fori_loop(0, n_full, slc_step, 0)
    finish(1)

    d0 = pl.multiple_of(t0, tq)
    s3 = _dot_nt(qa_sc[:, 0:LANES], kvw_ref[pl.ds(d0, tq), 0:LANES]).reshape(nh, tq, tq)
    s3 = jnp.where(_iota((1, tq, tq), 2) <= _iota((1, tq, tq), 1), s3, NEG)
    online(s3, load_v(kvw_ref, d0, tq), True)

    @pl.when(i >= 1)
    def _():
        start = pl.multiple_of(jnp.maximum(t0 - WINDOW, 0), tq)
        s3 = _dot_nt(qa_sc[:, 0:LANES], kvw_ref[pl.ds(start, WINDOW), 0:LANES]).reshape(nh, tq, WINDOW)
        kpos = start + _iota((1, tq, WINDOW), 2)
        s3 = jnp.where((kpos < t0) & (trow - kpos < WINDOW), s3, NEG)
        online(s3, load_v(kvw_ref, start, WINDOW), False)

    finish(2)

    for pair in range(nh // 2):
        parts = []
        for hh in (2 * pair, 2 * pair + 1):
            x = out_sc[hh]
            if hh % 2 != hh // NSA_HPG:
                x = pltpu.roll(x, HEAD_DIM, 1)
            parts.append(x)
        o_ref[:, pair * LANES:(pair + 1) * LANES] = jnp.where(half == 0, parts[0], parts[1]).astype(o_ref.dtype)


def _nsa_prompt(q, gates, ckv, kvs_b, kvw_b, ov, et, tq, tk):
    b, l, _ = q.shape
    nb = l // SLC_BLOCK
    assert l % tk == 0 and tk % tq == 0 and WINDOW % tq == 0 and l >= WINDOW
    blk = lambda w: pl.BlockSpec((None, tq, w), lambda bi, i: (bi, i, 0))
    seq = lambda a: pl.BlockSpec((None,) + a.shape[1:], lambda bi, i: (bi, 0, 0))
    full = lambda a: pl.BlockSpec(a.shape, lambda bi, i: (0, 0))
    hs = (NSA_HEADS, tq, LANES)
    return pl.pallas_call(
        functools.partial(_nsa_prompt_body, tq=tq, tk=tk, n_cmp=l // CMP_STRIDE - 1),
        grid=(b, l // tq),
        in_specs=[blk(NQ), blk(LANES), seq(ckv), seq(kvs_b), seq(kvw_b), full(ov), full(et)],
        out_specs=blk(NQ),
        out_shape=jax.ShapeDtypeStruct((b, l, NQ), BF16),
        scratch_shapes=[pltpu.VMEM((NSA_HEADS * tq, LANES + nb), BF16), pltpu.VMEM(hs, F32), pltpu.VMEM(hs, F32),
                        pltpu.VMEM(hs, F32)],
        compiler_params=_params("parallel", "arbitrary"),
        name="nsa_prompt",
    )(q, gates, ckv, kvs_b, kvw_b, ov, et)


def _nsa_sample_body(pt_ref, cache_ref, q_ref, gate_ref, ckv_ref, win_ref, ns_ref, nw_ref, ov_ref, e_ref, o_ref,
                     page_sc, sem, *, n_steps, n_pages, nseq, s_new, n_cmp, p_len):
    i = pl.program_id(0)
    npg = nseq * n_pages

    def fetch(step, slot):
        for s in range(nseq):
            for k in range(n_pages):
                pltpu.make_async_copy(cache_ref.at[pt_ref[step * nseq + s, k]], page_sc.at[slot, s * n_pages + k],
                                      sem.at[slot]).start()

    def wait(slot):
        for j in range(npg):
            pltpu.make_async_copy(cache_ref.at[0], page_sc.at[slot, j], sem.at[slot]).wait()

    @pl.when(i == 0)
    def _():
        fetch(0, 0)

    for slot in range(2):
        @pl.when(i % 2 == slot)
        def _():
            @pl.when(i + 1 < n_steps)
            def _():
                fetch(i + 1, 1 - slot)

            wait(slot)
            _nsa_sample_attend([page_sc.at[slot, j] for j in range(npg)], q_ref, gate_ref, ckv_ref, win_ref, ns_ref,
                               nw_ref, ov_ref, e_ref, o_ref, n_pages=n_pages, nseq=nseq, s_new=s_new, n_cmp=n_cmp,
                               p_len=p_len)


def _nsa_sample_attend(page_refs, q_ref, gate_ref, ckv_ref, win_ref, ns_ref, nw_ref, ov_ref, e_ref, o_ref,
                       *, n_pages, nseq, s_new, n_cmp, p_len):
    rows1 = NSA_HEADS * s_new
    rows = nseq * rows1
    nc = ckv_ref.shape[1]
    nb = ov_ref.shape[1]
    w_buf = win_ref.shape[2]
    seqs = range(nseq)
    of = lambda x, s: x[s * rows1:(s + 1) * rows1]
    cat = lambda xs: xs[0] if len(xs) == 1 else jnp.concatenate(xs, axis=0)
    q2 = [jnp.concatenate(_head_queries(q_ref[s], s_new), axis=0) for s in seqs]
    ridx = _iota((rows, 1), 0)
    trow = ridx & (s_new - 1)
    grow = _div_pow2(ridx, s_new * NSA_HPG) & (NSA_GROUPS - 1)
    lane_half = _iota((rows, LANES), 1) >> 6

    def gate_rows(branch):
        cols = [gate_ref[s][:, branch * NSA_HEADS + hh:branch * NSA_HEADS + hh + 1]
                for s in seqs for hh in range(NSA_HEADS)]
        return jnp.concatenate(cols, axis=0)

    def pick(acc, den):
        return jnp.where(lane_half == grow, acc, 0.0) / den

    def attend(s_old, s_nw, vt_old, new_rows):
        s_old = cat(s_old)
        s_nw = jnp.where(_iota((rows, s_new), 1) <= trow, cat(s_nw), NEG)
        m = jnp.maximum(jnp.max(s_old, axis=-1, keepdims=True), jnp.max(s_nw, axis=-1, keepdims=True))
        e1 = jnp.exp2(s_old - m)
        e2 = jnp.exp2(s_nw - m)
        den = jnp.sum(e1, axis=-1, keepdims=True) + jnp.sum(e2, axis=-1, keepdims=True)
        e1 = e1.astype(BF16)
        e2 = e2.astype(BF16)
        acc = cat([_dot_nt(vt_old[s], of(e1, s)).T + _dot(of(e2, s), new_rows[s][:, LANES:2 * LANES]) for s in seqs])
        return pick(acc, den)

    ncol = _iota((rows, nc), 1)
    cvalid = (ncol * CMP_STRIDE + CMP_LEN <= p_len + trow + 1) & (ncol < n_cmp)
    p_c = _masked_softmax2(cat([_dot_nt(q2[s], ckv_ref[s, :, 0:LANES]) for s in seqs]), cvalid)
    o_c = cat([_dot(of(p_c, s).astype(BF16), ckv_ref[s, :, LANES:2 * LANES]) for s in seqs])
    out = gate_rows(0) * jnp.where(lane_half == grow, o_c, 0.0)
    psum = []
    for s in seqs:
        for g in range(NSA_GROUPS):
            acc = jnp.zeros((s_new, nc), F32)
            for h in range(NSA_HPG):
                r0 = s * rows1 + (g * NSA_HPG + h) * s_new
                acc = acc + p_c[r0:r0 + s_new]
            psum.append(acc)
    n_imp = nseq * NSA_GROUPS * s_new
    psum.append(jnp.zeros((LANES - n_imp, nc), F32))
    imp = _dot01(jnp.concatenate(psum, axis=0), ov_ref[...])
    cur = jnp.full((1, LANES), p_len // SLC_BLOCK, jnp.int32)
    sel = _topk_select(imp, cur)[0:n_imp]
    selneg = jnp.where(sel > 0.0, 0.0, NEG).astype(BF16)
    selneg_rows = [jnp.concatenate([selneg[(s * NSA_GROUPS + g) * s_new:(s * NSA_GROUPS + g + 1) * s_new]
                                    for g in range(NSA_GROUPS) for _ in range(NSA_HPG)], axis=0)
                   for s in seqs]

    e_hot = e_ref[...]
    s_past, s_nw, vts, news = [], [], [], []
    for s in seqs:
        kvt = jnp.concatenate([r[...] for r in page_refs[s * n_pages:(s + 1) * n_pages]],
                              axis=1).astype(BF16)
        kaug = jnp.concatenate([kvt[0:LANES], e_hot], axis=0)
        s_past.append(_dot(jnp.concatenate([q2[s], selneg_rows[s]], axis=1), kaug))
        ns = ns_ref[s].astype(BF16)
        s_nw.append(_dot_nt(q2[s], ns[:, 0:LANES]))
        vts.append(kvt[LANES:2 * LANES])
        news.append(ns)
    out = out + gate_rows(1) * attend(s_past, s_nw, vts, news)

    scol = _iota((rows1, w_buf), 1)
    tr1 = _iota((rows1, 1), 0) & (s_new - 1)
    wmask = (w_buf + tr1 - scol < WINDOW) & (p_len - w_buf + scol >= 0)
    s_b, s_nw, vts, news = [], [], [], []
    for s in seqs:
        wt = win_ref[s].astype(BF16)
        nw = nw_ref[s].astype(BF16)
        s_b.append(jnp.where(wmask, _dot(q2[s], wt[0:LANES]), NEG))
        s_nw.append(_dot_nt(q2[s], nw[:, 0:LANES]))
        vts.append(wt[LANES:2 * LANES])
        news.append(nw)
    out = out + gate_rows(2) * attend(s_b, s_nw, vts, news)

    lh = _iota((s_new, LANES), 1) >> 6
    for s in seqs:
        for pair in range(NSA_HEADS // 2):
            parts = []
            for hh in (2 * pair, 2 * pair + 1):
                x = out[s * rows1 + hh * s_new:s * rows1 + (hh + 1) * s_new]
                if hh % 2 != hh // NSA_HPG:
                    x = jnp.concatenate([x[:, HEAD_DIM:], x[:, :HEAD_DIM]], axis=1)
                parts.append(x)
            o_ref[s, :, pair * LANES:(pair + 1) * LANES] = jnp.where(lh == 0, parts[0], parts[1]).astype(o_ref.dtype)


def _nsa_sample(q, gates, ckv, cache_t, page_table, win_t, new_s, new_w, ov, e_all):
    bd, s_new, _ = q.shape
    n_pages = page_table.shape[1]
    p_len = n_pages * PAGE_SIZE
    assert s_new & (s_new - 1) == 0
    nseq = _tile_rows(bd, SAMPLE_SEQS_PER_STEP)
    full = lambda a: pl.BlockSpec(a.shape, lambda i, pt: (0, 0))
    seq = lambda a: pl.BlockSpec((nseq,) + a.shape[1:], lambda i, pt: (i, 0, 0))
    return pl.pallas_call(
        functools.partial(_nsa_sample_body, n_steps=bd // nseq, n_pages=n_pages, nseq=nseq, s_new=s_new,
                          n_cmp=p_len // CMP_STRIDE - 1, p_len=p_len),
        grid_spec=pltpu.PrefetchScalarGridSpec(
            num_scalar_prefetch=1,
            grid=(bd // nseq,),
            in_specs=[pl.BlockSpec(memory_space=pl.ANY), seq(q), seq(gates), seq(ckv), seq(win_t), seq(new_s),
                      seq(new_w), full(ov), full(e_all)],
            out_specs=pl.BlockSpec((nseq, s_new, NQ), lambda i, pt: (i, 0, 0)),
            scratch_shapes=[pltpu.VMEM((2, nseq * n_pages, KV_W, PAGE_SIZE), F32), pltpu.SemaphoreType.DMA((2,))],
        ),
        out_shape=jax.ShapeDtypeStruct((bd, s_new, NQ), BF16),
        compiler_params=_params("arbitrary"),
        name="nsa_sample",
    )(page_table, cache_t, q, gates, ckv, win_t, new_s, new_w, ov, e_all)


def _gla_body(q_ref, k_ref, v_ref, lg_ref, r_ref, gn_ref, s0_ref, o_ref, sfin_ref, s_sc, *, ns, tb, chunk, sub):
    j = pl.program_id(1)

    @pl.when(j == 0)
    def _():
        s_sc[...] = s0_ref[...]

    nchunk = tb // chunk
    nsub = chunk // sub
    ri = _iota((tb, tb), 0)
    ci = _iota((tb, tb), 1)
    tril = jnp.where((_div_pow2(ri, chunk) == _div_pow2(ci, chunk)) & (ci <= ri), 1.0, 0.0).astype(BF16)
    head_of_lane = _div_pow2(_iota((1, GQK), 1), GLA_DK)
    rows = _iota((chunk, GQK), 0)
    ones_dv = jnp.ones((chunk, GLA_DV), BF16)
    vhead = _div_pow2(_iota((chunk, GV), 1), GLA_DV)
    gn = gn_ref[...]

    units = [(si, c) for si in range(ns) for c in range(nchunk)]
    b_seq = [_dot01_l(tril, lg_ref[si]) for si in range(ns)]
    pre = {}
    for si, c in units:
        c0 = c * chunk
        bc = b_seq[si][c0:c0 + chunk]
        qc = q_ref[si, c0:c0 + chunk, :]
        kc = k_ref[si, c0:c0 + chunk, :]
        b_last = bc[chunk - 1:chunk]
        qdb = qc * jnp.exp(bc)
        kdl = kc * jnp.exp(b_last - bc)
        decay = jnp.exp(_dot01_tn(jnp.where(rows == chunk - 1, bc, 0.0), ones_dv))
        kds = []
        for sb in range(nsub):
            rr = sb * sub
            ref_row = bc[rr - 1:rr] if sb else jnp.zeros((1, GQK), F32)
            vis = rows < rr + sub
            kds.append((ref_row, jnp.where(vis, kc * jnp.exp(jnp.where(vis, ref_row - bc, 0.0)), 0.0).astype(BF16)))
        vc = v_ref[si, c0:c0 + chunk, :].astype(BF16)
        vbd = jnp.concatenate([jnp.where(vhead == h, vc, jnp.zeros_like(vc)) for h in range(GLA_HEADS)], axis=0)
        o_sub = []
        for sb in range(nsub):
            rr = sb * sub
            ref_row, kd = kds[sb]
            qd = qc[rr:rr + sub] * jnp.exp(bc[rr:rr + sub] - ref_row)
            qst = jnp.concatenate([jnp.where(head_of_lane == h, qd, 0.0) for h in range(GLA_HEADS)], axis=0)
            a = _dot_nt(qst.astype(BF16), kd)
            qrow = _iota((GLA_HEADS * sub, chunk), 0) & (sub - 1)
            a = jnp.where(_iota((GLA_HEADS * sub, chunk), 1) <= rr + qrow, a, 0.0)
            a_cat = jnp.concatenate([a[h * sub:(h + 1) * sub] for h in range(GLA_HEADS)],
                                    axis=1).astype(BF16)
            o_sub.append(a_cat)
        o_intra = _dot(o_sub[0] if nsub == 1 else jnp.concatenate(o_sub, axis=0), vbd)
        qdb_st = jnp.concatenate([jnp.where(head_of_lane == h, qdb, 0.0) for h in range(GLA_HEADS)],
                                 axis=0).astype(BF16)
        kv = _dot_tn(kdl.astype(BF16), vc)
        upd = jnp.concatenate([kv[h * GLA_DK:(h + 1) * GLA_DK, h * GLA_DV:(h + 1) * GLA_DV]
                               for h in range(GLA_HEADS)], axis=0)
        pre[si, c] = (decay, upd, o_intra, qdb_st)

    for si in range(ns):
        state = s_sc[si]
        for c in range(nchunk):
            c0 = c * chunk
            decay, upd, o_intra, qdb_st = pre[si, c]
            o_inter = _dot(qdb_st, state.astype(BF16))
            for h in range(GLA_HEADS):
                hs = slice(h * GLA_DV, (h + 1) * GLA_DV)
                o = o_inter[h * chunk:(h + 1) * chunk] + o_intra[:, hs]
                o_ref[si, c0:c0 + chunk, hs] = (_rms(o, gn) * r_ref[si, c0:c0 + chunk, hs].astype(F32)).astype(o_ref.dtype)
            state = decay * state + upd
        s_sc[si] = state

    @pl.when(j == pl.num_programs(1) - 1)
    def _():
        sfin_ref[...] = s_sc[...]


def _gla(q, k, v, lg, r, gn, s0, ns, tb, chunk):
    s, t, _ = q.shape
    sub = min(GLA_SUB, chunk)
    row = lambda w: pl.BlockSpec((ns, tb, w), lambda gi, j: (gi, j, 0))
    st = pl.BlockSpec((ns, GQK, GLA_DV), lambda gi, j: (gi, 0, 0))
    return pl.pallas_call(
        functools.partial(_gla_body, ns=ns, tb=tb, chunk=chunk, sub=sub),
        grid=(s // ns, t // tb),
        in_specs=[row(GQK), row(GQK), row(GV), row(GQK), row(GV), pl.BlockSpec(gn.shape, lambda gi, j: (0, 0)), st],
        out_specs=[row(GV), st],
        out_shape=[jax.ShapeDtypeStruct((s, t, GV), BF16), jax.ShapeDtypeStruct((s, GQK, GLA_DV), F32)],
        scratch_shapes=[pltpu.VMEM((ns, GQK, GLA_DV), F32)],
        compiler_params=_params("parallel", "arbitrary"),
        name="gla",
    )(q, k, v, lg, r, gn, s0)


def _mix_body(x_ref, sc_ref, sh_ref, ga_ref, gpre_ref, gpost_ref, on_ref, og_ref, wm_ref, wn_ref, wgo_ref, wo_ref,
              o_ref):
    x = x_ref[...]
    d = x.shape[1]
    h = _norm_mod(x, gpre_ref[...], sc_ref[...], sh_ref[...]).astype(BF16)
    m = _sigmoid(_dot(h, wm_ref[...]))
    y_a = _dot(on_ref[...], wn_ref[...])
    y_b = _dot(og_ref[...], wgo_ref[...])
    mixin = (m[:, 0:d] * y_a + m[:, d:2 * d] * y_b).astype(BF16)
    mix = _dot(mixin, wo_ref[...])
    o_ref[...] = x + ga_ref[...] * _rms(mix, gpost_ref[...])


def _mix(x, sc, sh, ga, gpre, gpost, o_nsa, o_gla, wm, wn, wgo, wo, tm, rows_per_mod):
    n, d = x.shape
    row = lambda w: pl.BlockSpec((tm, w), lambda i: (i, 0))
    full = lambda a: pl.BlockSpec(a.shape, lambda i: (0, 0))
    ms = lambda a: _mod_spec(a, tm, rows_per_mod)
    return pl.pallas_call(
        _mix_body,
        grid=(n // tm,),
        in_specs=[row(d), ms(sc), ms(sh), ms(ga), full(gpre), full(gpost), row(NQ), row(GV),
                  full(wm), full(wn), full(wgo), full(wo)],
        out_specs=row(d),
        out_shape=jax.ShapeDtypeStruct((n, d), F32),
        compiler_params=_params("parallel"),
        name="mix",
    )(x, sc, sh, ga, gpre, gpost, o_nsa, o_gla, wm, wn, wgo, wo)


def _ffn_body(x_ref, sc_ref, sh_ref, ga_ref, gpre_ref, gpost_ref, wg_ref, wu_ref, wd_ref, o_ref, h_sc, acc_sc):
    j = pl.program_id(1)

    @pl.when(j == 0)
    def _():
        h_sc[...] = _norm_mod(x_ref[...], gpre_ref[...], sc_ref[...], sh_ref[...]).astype(BF16)
        acc_sc[...] = jnp.zeros(acc_sc.shape, F32)

    h = h_sc[...]
    gt = _dot(h, wg_ref[...])
    up = _dot(h, wu_ref[...])
    a = (gt * _sigmoid(gt) * up).astype(BF16)
    acc_sc[...] += _dot(a, wd_ref[...])

    @pl.when(j == pl.num_programs(1) - 1)
    def _():
        o_ref[...] = x_ref[...] + ga_ref[...] * _rms(acc_sc[...], gpost_ref[...])


def _ffn(x, sc, sh, ga, gpre, gpost, wg, wu, wd, tm, rows_per_mod):
    n, d = x.shape
    dff = wg.shape[1]
    tf = dff // 2 if (dff // 2) % LANES == 0 else dff
    row = pl.BlockSpec((tm, d), lambda i, j: (i, 0))
    full = lambda a: pl.BlockSpec(a.shape, lambda i, j: (0, 0))
    ms = lambda a: pl.BlockSpec((None, a.shape[1], a.shape[2]), lambda i, j: ((i * tm) // rows_per_mod, 0, 0))
    return pl.pallas_call(
        _ffn_body,
        grid=(n // tm, dff // tf),
        in_specs=[row, ms(sc), ms(sh), ms(ga), full(gpre), full(gpost),
                  pl.BlockSpec((d, tf), lambda i, j: (0, j)), pl.BlockSpec((d, tf), lambda i, j: (0, j)),
                  pl.BlockSpec((tf, d), lambda i, j: (j, 0))],
        out_specs=row,
        out_shape=jax.ShapeDtypeStruct((n, d), F32),
        scratch_shapes=[pltpu.VMEM((tm, d), BF16), pltpu.VMEM((tm, d), F32)],
        compiler_params=_params("parallel", "arbitrary"),
        name="ffn",
    )(x, sc, sh, ga, gpre, gpost, wg, wu, wd)


def _split_w_in(w_in, d):
    sizes = (NQ, KV_W, KV_W, KV_W, 3 * NSA_HEADS, GQK, GQK, GV, 16, GV, 2 * d)
    out, o = [], 0
    for s in sizes:
        out.append(w_in[:, o:o + s])
        o += s
    return out


def _pad_cols(w, n):
    return jnp.pad(w, ((0, 0), (0, n - w.shape[1])))


def _compress_weights(w1k, b1k, w2k, pek, w1v, b1v, w2v, pev):
    hid = w2k.shape[0]
    npair = CMP_STRIDE // 2
    eye = np.eye(NSA_GROUPS, dtype=np.float32)

    def group_diag(w):
        out = jnp.einsum('gh,...ab->...gahb', eye, w)
        return out.reshape(w.shape[:-2] + (NSA_GROUPS * w.shape[-2], NSA_GROUPS * w.shape[-1]))

    def expand_w1(half):
        per_kv = [group_diag(w1.reshape(CMP_LEN, HEAD_DIM, hid)[half * CMP_STRIDE:(half + 1) * CMP_STRIDE])
                  for w1 in (w1k, w1v)]
        return jnp.stack(per_kv).reshape(2, npair, 2 * NSA_GROUPS * HEAD_DIM, NSA_GROUPS * hid)

    w1 = jnp.concatenate([expand_w1(0), expand_w1(1)], axis=-1).astype(BF16)

    def expand_pe(half):
        per_kv = [jnp.tile(pe[half * CMP_STRIDE:(half + 1) * CMP_STRIDE], (1, NSA_GROUPS)) for pe in (pek, pev)]
        return jnp.stack(per_kv).reshape(2, npair, 2 * NSA_GROUPS * HEAD_DIM)

    pe = jnp.pad(jnp.stack([expand_pe(0), expand_pe(1)], axis=2),
                 ((0, 0), (0, 0), (0, PE_ROWS - 2), (0, 0))).astype(BF16)

    b1 = jnp.concatenate([jnp.tile(b1k, NSA_GROUPS), jnp.tile(b1v, NSA_GROUPS)]).reshape(1, -1)
    w2d = [group_diag(w2k), group_diag(w2v)]
    zero = jnp.zeros_like(w2d[0])
    w2 = jnp.concatenate([jnp.concatenate([w2d[0], zero], axis=1),
                          jnp.concatenate([zero, w2d[1]], axis=1)], axis=0).astype(BF16)
    return pe, w1, b1, w2


def _overlap(n_cmp_pad, n_cmp, nb):
    n = np.arange(n_cmp_pad)[:, None]
    j = np.arange(nb)[None, :] * SLC_BLOCK
    ok = (n * CMP_STRIDE < j + SLC_BLOCK) & (n * CMP_STRIDE + CMP_LEN > j) & (n < n_cmp)
    return ok.astype(BF16)


def _block_onehot(nb, nkeys):
    return (np.arange(nb)[:, None] == (np.arange(nkeys)[None, :] // SLC_BLOCK)).astype(BF16)


def _tile_rows(n, cap):
    t = cap
    while n % t:
        t //= 2
    return t


def _feature_major(a):
    nd = a.ndim
    a = jnp.moveaxis(a, nd - 4, nd - 1)
    return a.reshape(a.shape[:nd - 4] + (KV_W, a.shape[-1]))


def _row_major(a_t):
    b, _, r = a_t.shape
    return jnp.moveaxis(a_t.reshape(b, 2, NSA_GROUPS, HEAD_DIM, r), 4, 1)


def kernel(x_prompt, x_sample, cache_cmp_kv, cache_slc_kv, state_win_kv, state_gla, page_table, c_prompt, c_sample, w_ada, b_ada, g_pre_mix, g_post_mix, w_in, cmp_w1_k, cmp_b1_k, cmp_w2_k, cmp_pe_k, cmp_w1_v, cmp_b1_v, cmp_w2_v, cmp_pe_v, w_gla_a2, b_gla_a, gla_norm_g, w_nsa_o, w_gla_o, w_out, g_pre_ffn, g_post_ffn, w_ff_gate, w_ff_up, w_ff_down):
    depth = w_ada.shape[0]
    bp, l, d = x_prompt.shape
    bd, s_new, _ = x_sample.shape
    n_pages = page_table.shape[1]
    p_len = n_pages * PAGE_SIZE
    w_buf = state_win_kv.shape[2]
    keep = min(WINDOW, l)
    np_, ns_ = bp * l, bd * s_new
    tmp = _tile_rows(np_ // bp, 512)
    tms = _tile_rows(ns_, 512)
    tq = _tile_rows(l, 256)
    tk = 2 * tq

    yp, ys = x_prompt.reshape(np_, d), x_sample.reshape(ns_, d)
    col = [[] for _ in range(8)]
    for li in range(depth):
        r_all = bp + bd
        r_pad = -(-r_all // 8) * 8
        c_all = jnp.pad(jnp.concatenate([c_prompt, c_sample], axis=0), ((0, r_pad - r_all), (0, 0)))
        mod = _ada(c_all, w_ada[li], b_ada[li])
        mods_p = [m.reshape(bp, 1, d) for m in jnp.split(mod[:bp], 6, axis=-1)]
        mods_s = [jnp.repeat(m, s_new, axis=0).reshape(ns_ // tms, tms, d) for m in jnp.split(mod[bp:r_all], 6, axis=-1)]

        (w_q, w_kc, w_ks, w_kw, w_gn, w_qg, w_kg, w_vg, w_ag, w_rg, w_mg) = _split_w_in(w_in[li], d)
        wq = w_q.astype(BF16)
        wkv = jnp.concatenate([w_kc, w_ks, w_kw], axis=1).astype(BF16)
        wgn = _pad_cols(w_gn, LANES).astype(BF16)
        wqkv = jnp.concatenate([w_qg, w_kg, w_vg], axis=1).astype(BF16)
        wa = _pad_cols(w_ag, LANES).astype(BF16)
        wr = w_rg.astype(BF16)
        a2 = jnp.pad(w_gla_a2[li], ((0, LANES - w_gla_a2.shape[1]), (0, 0)))
        ba = b_gla_a[li].reshape(1, GQK)
        cw = _compress_weights(cmp_w1_k[li], cmp_b1_k[li], cmp_w2_k[li], cmp_pe_k[li],
                               cmp_w1_v[li], cmp_b1_v[li], cmp_w2_v[li], cmp_pe_v[li])
        gpre = g_pre_mix[li].reshape(1, d)
        gpost = g_post_mix[li].reshape(1, d)
        gpre2 = g_pre_ffn[li].reshape(1, d)
        gpost2 = g_post_ffn[li].reshape(1, d)
        gn = gla_norm_g[li].reshape(1, GLA_DV)
        wm = w_mg.astype(BF16)
        wn = w_nsa_o[li].astype(BF16)
        wgo = w_gla_o[li].astype(BF16)
        wo = w_out[li].astype(BF16)
        wfg = w_ff_gate[li].astype(BF16)
        wfu = w_ff_up[li].astype(BF16)
        wfd = w_ff_down[li].astype(BF16)

        sh1, sc1, ga1, sh2, sc2, ga2 = mods_p
        (q, kvc, _, _, kvs_b, kvw_b, gates, kvc_t, kvs_t, kvw_t) = _pre_nsa(yp, sc1, sh1, gpre, wq, wkv, wgn, tmp, l,
                                                                            feature_major=(bp, l))
        qg, kg, vg, lg, rg = _pre_gla(yp, sc1, sh1, gpre, wqkv, wa, wr, a2, ba, tmp, l)
        ckv = _cmp_prompt(kvc.reshape(bp, l, KV_W), cw)
        nch = l // CMP_STRIDE
        nb = l // SLC_BLOCK
        o_nsa = _nsa_prompt(q.reshape(bp, l, NQ), gates.reshape(bp, l, LANES), ckv, kvs_b.reshape(bp, l, KV_W),
                            kvw_b.reshape(bp, l, KV_W), _overlap(nch, nch - 1, nb), _block_onehot(nb, l).T, tq, tk)
        chunk = GLA_CHUNK if l % GLA_CHUNK == 0 else l
        tb = 2 * LANES if (l % (2 * LANES) == 0 and LANES % chunk == 0) else chunk
        o_gla, s_fin_p = _gla(qg.reshape(bp, l, GQK), kg.reshape(bp, l, GQK), vg.reshape(bp, l, GV),
                              lg.reshape(bp, l, GQK), rg.reshape(bp, l, GV), gn,
                              jnp.zeros((bp, GQK, GLA_DV), F32), bp, tb, chunk)
        x1 = _mix(yp, sc1, sh1, ga1, gpre, gpost, o_nsa.reshape(np_, NQ), o_gla.reshape(np_, GV), wm, wn, wgo, wo, tmp, l)
        yp = _ffn(x1, sc2, sh2, ga2, gpre2, gpost2, wfg, wfu, wfd, tmp, l)
        col[0].append(_row_major(kvc_t))
        col[2].append(_row_major(kvs_t))
        col[4].append(_row_major(kvw_t[:, :, l - keep:]))
        col[6].append(s_fin_p.reshape(bp, GLA_HEADS, GLA_DK, GLA_DV).astype(state_gla.dtype))

        sh1, sc1, ga1, sh2, sc2, ga2 = mods_s
        q, kvc, kvs, kvw, _, _, gates = _pre_nsa(ys, sc1, sh1, gpre, wq, wkv, wgn, tms, tms)
        qg, kg, vg, lg, rg = _pre_gla(ys, sc1, sh1, gpre, wqkv, wa, wr, a2, ba, tms, tms)
        ckv = _cmp_sample(_feature_major(cache_cmp_kv[li]), page_table, cw)
        nch = p_len // CMP_STRIDE
        nb = p_len // SLC_BLOCK
        win_t = _feature_major(state_win_kv[li])
        kvw3 = kvw.reshape(bd, s_new, KV_W)
        o_nsa = _nsa_sample(q.reshape(bd, s_new, NQ), gates.reshape(bd, s_new, LANES), ckv,
                            _feature_major(cache_slc_kv[li]), page_table, win_t, kvs.reshape(bd, s_new, KV_W), kvw3,
                            _overlap(nch, nch - 1, nb), _block_onehot(nb, p_len))
        chunk = GLA_CHUNK if s_new % GLA_CHUNK == 0 else s_new
        o_gla, s_fin_s = _gla(qg.reshape(bd, s_new, GQK), kg.reshape(bd, s_new, GQK), vg.reshape(bd, s_new, GV),
                              lg.reshape(bd, s_new, GQK), rg.reshape(bd, s_new, GV), gn,
                              state_gla[li].astype(F32).reshape(bd, GQK, GLA_DV), _tile_rows(bd, 16), chunk, chunk)
        x1 = _mix(ys, sc1, sh1, ga1, gpre, gpost, o_nsa.reshape(ns_, NQ), o_gla.reshape(ns_, GV), wm, wn, wgo, wo, tms, tms)
        ys = _ffn(x1, sc2, sh2, ga2, gpre2, gpost2, wfg, wfu, wfd, tms, tms)
        win_new_t = jnp.concatenate([win_t, kvw3.transpose(0, 2, 1)], axis=2)[:, :, s_new:]
        win_new = jnp.moveaxis(win_new_t.reshape(bd, 2, NSA_GROUPS, HEAD_DIM, w_buf), 4, 1)
        col[1].append(kvc.reshape(bd, s_new, 2, NSA_GROUPS, HEAD_DIM))
        col[3].append(kvs.reshape(bd, s_new, 2, NSA_GROUPS, HEAD_DIM))
        col[5].append(win_new)
        col[7].append(s_fin_s.reshape(bd, GLA_HEADS, GLA_DK, GLA_DV).astype(state_gla.dtype))

    stacked = [c[0][None] if len(c) == 1 else jnp.stack(c) for c in col]
    return (yp.reshape(bp, l, d), ys.reshape(bd, s_new, d), *stacked)
```

```python
import functools
import math

import numpy as np
import jax
import jax.numpy as jnp
from jax import lax
from jax.experimental import pallas as pl
from jax.experimental.pallas import tpu as pltpu

F32 = jnp.float32
BF16 = jnp.bfloat16

NSA_HEADS = 8
NSA_GROUPS = 2
NSA_HPG = NSA_HEADS // NSA_GROUPS
HEAD_DIM = 64
CMP_STRIDE = 16
CMP_LEN = 32
SLC_BLOCK = 64
N_SEL = 16
WINDOW = 512
GLA_HEADS = 4
GLA_DK = 64
GLA_DV = 128
GLA_TAU = 16.0
GLA_CHUNK = 64
GLA_SUB = 16
PE_ROWS = 16
SAMPLE_SEQS_PER_STEP = 2
EPS = 1e-6
PAGE_SIZE = 128
NEG = -1e30
LOG2E = math.log2(math.e)
LANES = 128
VMEM_LIMIT = 56 * 1024 * 1024

KV_W = 2 * NSA_GROUPS * HEAD_DIM
NQ = NSA_HEADS * HEAD_DIM
GQK = GLA_HEADS * GLA_DK
GV = GLA_HEADS * GLA_DV


def _dot(a, b):
    return jnp.dot(a, b, preferred_element_type=F32)


def _dot_nt(a, b):
    return lax.dot_general(a, b, (((1,), (1,)), ((), ())), preferred_element_type=F32)


def _dot_f32(a, b):
    return jnp.dot(a, b, preferred_element_type=F32, precision=lax.Precision.HIGHEST)


def _dot01(x, m01):
    hi = x.astype(BF16)
    r1 = x - hi.astype(F32)
    mid = r1.astype(BF16)
    lo = (r1 - mid.astype(F32)).astype(BF16)
    return _dot(hi, m01) + _dot(mid, m01) + _dot(lo, m01)


def _dot01_l(m01, x):
    hi = x.astype(BF16)
    r1 = x - hi.astype(F32)
    mid = r1.astype(BF16)
    lo = (r1 - mid.astype(F32)).astype(BF16)
    return _dot(m01, hi) + _dot(m01, mid) + _dot(m01, lo)


def _dot_tn(a, b):
    return lax.dot_general(a, b, (((0,), (0,)), ((), ())), preferred_element_type=F32)


def _dot01_tn(x, m01):
    hi = x.astype(BF16)
    r1 = x - hi.astype(F32)
    mid = r1.astype(BF16)
    lo = (r1 - mid.astype(F32)).astype(BF16)
    return _dot_tn(hi, m01) + _dot_tn(mid, m01) + _dot_tn(lo, m01)


def _sigmoid(x):
    return 1.0 / (1.0 + jnp.exp(-x))


def _params(*sem):
    return pltpu.CompilerParams(dimension_semantics=sem, vmem_limit_bytes=VMEM_LIMIT)


def _iota(shape, dim):
    return lax.broadcasted_iota(jnp.int32, shape, dim)


def _div_pow2(x, n):
    assert n & (n - 1) == 0, n
    return x >> (n.bit_length() - 1)


def _ada_body(c_ref, w_ref, b_ref, o_ref):
    o_ref[...] = _dot_f32(c_ref[...], w_ref[...]) + b_ref[...]


def _ada(c_all, w_ada, b_ada):
    r, d = c_all.shape
    n = w_ada.shape[1]
    tn = 1536 if n % 1536 == 0 else n
    return pl.pallas_call(
        _ada_body,
        grid=(n // tn,),
        in_specs=[pl.BlockSpec((r, d), lambda j: (0, 0)),
                  pl.BlockSpec((d, tn), lambda j: (0, j)),
                  pl.BlockSpec((1, tn), lambda j: (0, j))],
        out_specs=pl.BlockSpec((r, tn), lambda j: (0, j)),
        out_shape=jax.ShapeDtypeStruct((r, n), F32),
        compiler_params=_params("arbitrary"),
        name="ada",
    )(c_all, w_ada, b_ada.reshape(1, n))


def _norm_mod(x, g, sc, sh):
    ms = jnp.mean(x * x, axis=-1, keepdims=True)
    y = x * lax.rsqrt(ms + EPS) * g
    return y * (1.0 + sc) + sh


def _rms(x, g):
    ms = jnp.mean(x * x, axis=-1, keepdims=True)
    return x * lax.rsqrt(ms + EPS) * g


def _pre_nsa_body(x_ref, sc_ref, sh_ref, g_ref, wq_ref, wkv_ref, wg_ref,
                  q_ref, kvc_ref, kvs_ref, kvw_ref, kvsb_ref, kvwb_ref, gate_ref, *t_refs):
    h = _norm_mod(x_ref[...], g_ref[...], sc_ref[...], sh_ref[...]).astype(BF16)
    q_ref[...] = (_dot(h, wq_ref[...]) * (HEAD_DIM ** -0.5 * LOG2E)).astype(BF16)
    kv = _dot(h, wkv_ref[...])
    kvc = kv[:, 0:KV_W]
    kvs = kv[:, KV_W:2 * KV_W]
    kvw = kv[:, 2 * KV_W:3 * KV_W]
    kvc_ref[...] = kvc
    kvs_ref[...] = kvs
    kvw_ref[...] = kvw
    kvsb_ref[...] = kvs.astype(BF16)
    kvwb_ref[...] = kvw.astype(BF16)
    gate_ref[...] = _sigmoid(_dot(h, wg_ref[...]))
    if t_refs:
        kvt = kv.T
        for j, t_ref in enumerate(t_refs):
            t_ref[...] = kvt[j * KV_W:(j + 1) * KV_W]


def _mod_spec(mod3, tm, rows_per_mod):
    r = mod3.shape[1]
    return pl.BlockSpec((None, r, mod3.shape[2]), lambda i: ((i * tm) // rows_per_mod, 0, 0))


def _pre_nsa(x, sc, sh, g, wq, wkv, wg, tm, rows_per_mod, feature_major=None):
    n, d = x.shape
    row = lambda w: pl.BlockSpec((tm, w), lambda i: (i, 0))
    full = lambda a: pl.BlockSpec(a.shape, lambda i: (0, 0))
    outs = [(NQ, BF16), (KV_W, F32), (KV_W, F32), (KV_W, F32), (KV_W, BF16), (KV_W, BF16), (LANES, F32)]
    out_specs = [row(w) for w, _ in outs]
    out_shape = [jax.ShapeDtypeStruct((n, w), t) for w, t in outs]
    if feature_major is not None:
        nbatch, per = feature_major
        tiles = per // tm
        out_specs += [pl.BlockSpec((None, KV_W, tm), lambda i: (i // tiles, 0, i % tiles))] * 3
        out_shape += [jax.ShapeDtypeStruct((nbatch, KV_W, per), F32)] * 3
    return pl.pallas_call(
        _pre_nsa_body,
        grid=(n // tm,),
        in_specs=[row(d), _mod_spec(sc, tm, rows_per_mod), _mod_spec(sh, tm, rows_per_mod), full(g),
                  full(wq), full(wkv), full(wg)],
        out_specs=out_specs,
        out_shape=out_shape,
        compiler_params=_params("parallel"),
        name="pre_nsa",
    )(x, sc, sh, g, wq, wkv, wg)


def _log_sigmoid(x):
    return jnp.minimum(x, 0.0) - jnp.log(1.0 + jnp.exp(-jnp.abs(x)))


def _pre_gla_body(x_ref, sc_ref, sh_ref, g_ref, wqkv_ref, wa_ref, wr_ref, a2_ref, ba_ref,
                  q_ref, k_ref, v_ref, lg_ref, r_ref):
    h = _norm_mod(x_ref[...], g_ref[...], sc_ref[...], sh_ref[...]).astype(BF16)
    qkv = _dot(h, wqkv_ref[...])
    q_ref[...] = qkv[:, 0:GQK] * GLA_DK ** -0.5
    k_ref[...] = qkv[:, GQK:2 * GQK]
    v_ref[...] = qkv[:, 2 * GQK:2 * GQK + GV]
    a = _dot(h, wa_ref[...])
    lg_ref[...] = _log_sigmoid(_dot_f32(a, a2_ref[...]) + ba_ref[...]) * (1.0 / GLA_TAU)
    r = _dot(h, wr_ref[...])
    r_ref[...] = (r * _sigmoid(r)).astype(BF16)


def _pre_gla(x, sc, sh, g, wqkv, wa, wr, a2, ba, tm, rows_per_mod):
    n, d = x.shape
    row = lambda w: pl.BlockSpec((tm, w), lambda i: (i, 0))
    full = lambda a: pl.BlockSpec(a.shape, lambda i: (0, 0))
    return pl.pallas_call(
        _pre_gla_body,
        grid=(n // tm,),
        in_specs=[row(d), _mod_spec(sc, tm, rows_per_mod), _mod_spec(sh, tm, rows_per_mod), full(g),
                  full(wqkv), full(wa), full(wr), full(a2), full(ba)],
        out_specs=[row(GQK), row(GQK), row(GV), row(GQK), row(GV)],
        out_shape=[jax.ShapeDtypeStruct((n, GQK), F32), jax.ShapeDtypeStruct((n, GQK), F32),
                   jax.ShapeDtypeStruct((n, GV), F32), jax.ShapeDtypeStruct((n, GQK), F32),
                   jax.ShapeDtypeStruct((n, GV), BF16)],
        compiler_params=_params("parallel"),
        name="pre_gla",
    )(x, sc, sh, g, wqkv, wa, wr, a2, ba)


def _gelu_tanh(x):
    return 0.5 * x * (1.0 + jnp.tanh(0.7978845608028654 * (x + 0.044715 * x * x * x)))


def _compress(get_xp, nch, pe_ref, w1_ref, b1_ref, w2_ref):
    first, second = [], []
    for kv in range(2):
        acc = jnp.zeros((nch + PE_ROWS, 2 * LANES), F32)
        for pp in range(CMP_STRIDE // 2):
            lhs = jnp.concatenate([get_xp(kv, 2 * pp), get_xp(kv, 2 * pp + 1)], axis=1).astype(BF16)
            acc = acc + _dot(jnp.concatenate([lhs, pe_ref[kv, pp]], axis=0), w1_ref[kv, pp])
        first.append(acc[0:nch, 0:LANES] + acc[nch:nch + 1, 0:LANES])
        second.append(acc[0:nch, LANES:2 * LANES] + acc[nch + 1:nch + 2, LANES:2 * LANES])
    first = jnp.concatenate(first, axis=1)
    second = jnp.concatenate(second, axis=1)
    if nch % 8 == 0:
        nxt = pltpu.roll(second, nch - 1, 0)
    else:
        nxt = jnp.concatenate([second[1:], second[:1]], axis=0)
    hid = _gelu_tanh(first + nxt + b1_ref[...])
    out = _dot(hid.astype(BF16), w2_ref[...])
    return jnp.where(_iota(out.shape, 0) < nch - 1, out, 0.0)


def _cmp_prompt_body(xk_ref, xv_ref, pe_ref, w1_ref, b1_ref, w2_ref, o_ref, *, nch):
    x_refs = (xk_ref, xv_ref)

    def get_xp(kv, p):
        return x_refs[kv][pl.ds(p, nch, stride=CMP_STRIDE), :]

    o_ref[...] = _compress(get_xp, nch, pe_ref, w1_ref, b1_ref, w2_ref).astype(o_ref.dtype)


def _cmp_prompt(kvc, cw):
    b, l, _ = kvc.shape
    nch = l // CMP_STRIDE
    full = lambda a: pl.BlockSpec(a.shape, lambda i: (0,) * a.ndim)
    return pl.pallas_call(
        functools.partial(_cmp_prompt_body, nch=nch),
        grid=(b,),
        in_specs=[pl.BlockSpec((None, l, LANES), lambda i: (i, 0, 0)),
                  pl.BlockSpec((None, l, LANES), lambda i: (i, 0, 1))] + [full(a) for a in cw],
        out_specs=pl.BlockSpec((None, nch, KV_W), lambda i: (i, 0, 0)),
        out_shape=jax.ShapeDtypeStruct((b, nch, KV_W), BF16),
        compiler_params=_params("parallel"),
        name="cmp_prompt",
    )(kvc, kvc, *cw)


def _cmp_sample_body(pt_ref, cache_ref, pe_ref, w1_ref, b1_ref, w2_ref, o_ref,
                     page_sc, ak_sc, av_sc, bk_sc, bv_sc, sem, *, n_seq, n_pages, nch):
    i = pl.program_id(0)
    cpp = PAGE_SIZE // CMP_STRIDE
    bufs = ((ak_sc, av_sc), (bk_sc, bv_sc))

    def page_copy(seq, slot, k):
        return pltpu.make_async_copy(cache_ref.at[pt_ref[seq, k]], page_sc.at[slot, k], sem.at[slot])

    def fetch(seq, slot):
        for k in range(n_pages):
            page_copy(seq, slot, k).start()

    def wait(slot):
        for k in range(n_pages):
            pltpu.make_async_copy(cache_ref.at[0], page_sc.at[slot, k], sem.at[slot]).wait()

    def regroup(slot):
        for k in range(n_pages):
            for kv in range(2):
                xt = page_sc[slot, k, kv * LANES:(kv + 1) * LANES, :].T
                for n in range(cpp):
                    for a in range(CMP_STRIDE // 8):
                        r0 = n * CMP_STRIDE + 8 * a
                        bufs[slot][kv][k, pl.ds(8 * a * cpp + n, 8, stride=cpp), :] = xt[r0:r0 + 8, :]

    def mlp(src):
        def get_xp(kv, p):
            return src[kv][:, p * cpp:(p + 1) * cpp, :].reshape(nch, LANES)

        o_ref[...] = _compress(get_xp, nch, pe_ref, w1_ref, b1_ref, w2_ref).astype(o_ref.dtype)

    @pl.when(i == 0)
    def _():
        fetch(0, 0)
        if n_seq == 1:
            page_sc[1] = jnp.zeros(page_sc.shape[1:], F32)
        bk_sc[...] = jnp.zeros(bk_sc.shape, F32)
        bv_sc[...] = jnp.zeros(bv_sc.shape, F32)

    for slot in range(2):
        @pl.when(i % 2 == slot)
        def _():
            @pl.when(i + 1 < n_seq)
            def _():
                fetch(i + 1, 1 - slot)

            @pl.when(i < n_seq)
            def _():
                wait(slot)

            regroup(slot)
            mlp(bufs[1 - slot])


def _cmp_sample(cache_t, page_table, cw):
    bd, n_pages = page_table.shape
    p_len = n_pages * PAGE_SIZE
    nch = p_len // CMP_STRIDE
    full = lambda a: pl.BlockSpec(a.shape, lambda i, pt: (0,) * a.ndim)
    return pl.pallas_call(
        functools.partial(_cmp_sample_body, n_seq=bd, n_pages=n_pages, nch=nch),
        grid_spec=pltpu.PrefetchScalarGridSpec(
            num_scalar_prefetch=1,
            grid=(bd + 1,),
            in_specs=[pl.BlockSpec(memory_space=pl.ANY)] + [full(a) for a in cw],
            out_specs=pl.BlockSpec((None, nch, KV_W), lambda i, pt: (jnp.maximum(i - 1, 0), 0, 0)),
            scratch_shapes=[pltpu.VMEM((2, n_pages, KV_W, PAGE_SIZE), F32)]
            + [pltpu.VMEM((n_pages, PAGE_SIZE, LANES), F32)] * 4 + [pltpu.SemaphoreType.DMA((2,))],
        ),
        out_shape=jax.ShapeDtypeStruct((bd, nch, KV_W), BF16),
        compiler_params=_params("arbitrary"),
        name="cmp_sample",
    )(page_table, cache_t, *cw)


def _head_queries(q, rows):
    half = _iota((rows, LANES), 1) >> 6
    out = []
    for hh in range(NSA_HEADS):
        g = hh // NSA_HPG
        blk = q[:, (hh // 2) * LANES:(hh // 2 + 1) * LANES].astype(F32)
        if hh % 2 != g:
            blk = pltpu.roll(blk, HEAD_DIM, 1) if rows % 8 == 0 else jnp.roll(blk, HEAD_DIM, 1)
        out.append(jnp.where(half == g, blk, 0.0).astype(BF16))
    return out


def _topk_select(imp, cur):
    impt = imp.T
    nb = impt.shape[0]
    jl = _iota(impt.shape, 0)
    jf = jl.astype(F32)
    forced = (jl == 0) | (jl == cur) | (jl == cur - 1)
    elig = (jl >= 1) & (jl <= cur - 2)
    dead = -1.0

    def step(_, vals):
        m = jnp.max(vals, axis=0, keepdims=True)
        idx = jnp.min(jnp.where(vals == m, jf, float(nb)), axis=0, keepdims=True)
        return jnp.where(jf == idx, dead, vals)

    vals = lax.fori_loop(0, N_SEL - 3, step, jnp.where(elig, impt, dead))
    return jnp.where(forced | (elig & (vals == dead)), 1.0, 0.0).T


def _masked_softmax2(s, valid):
    s = jnp.where(valid, s, NEG)
    m = jnp.max(s, axis=-1, keepdims=True)
    e = jnp.where(valid, jnp.exp2(s - m), 0.0)
    d = jnp.sum(e, axis=-1, keepdims=True)
    return e * (1.0 / jnp.where(d > 0.0, d, 1.0))


def _nsa_prompt_body(q_ref, gate_ref, ckv_ref, kvs_ref, kvw_ref, ov_ref, et_ref, o_ref, qa_sc, m_sc, acc_sc, out_sc, imp_sc,
                     *, tq, tk, n_cmp):
    i = pl.program_id(1)
    t0 = i * tq
    nc = ckv_ref.shape[0]
    nb = ov_ref.shape[1]
    nh = NSA_HEADS
    qh = _head_queries(q_ref[...], tq)
    for hh in range(nh):
        qa_sc[hh * tq:(hh + 1) * tq, 0:LANES] = qh[hh]
    half = _iota((tq, LANES), 1) >> 6
    trow = t0 + _iota((1, tq, 1), 1)
    gates = gate_ref[...]

    def gate(branch, hh):
        c = branch * nh + hh
        return gates[:, c:c + 1]

    def cmp_branch(w):
        s = _dot_nt(qa_sc[:, 0:LANES], ckv_ref[0:w, 0:LANES]).reshape(nh, tq, w)
        ncol = _iota((1, tq, w), 2)
        cvalid = (ncol * CMP_STRIDE + CMP_LEN <= trow + 1) & (ncol < n_cmp)
        p = _masked_softmax2(s, cvalid)
        cv = ckv_ref[0:w, LANES:2 * LANES]
        for hh in range(nh):
            out_sc[hh] = gate(0, hh) * _dot(p[hh].astype(BF16), cv)
        ov = ov_ref[0:w, :]
        for g in range(NSA_GROUPS):
            psum = p[g * NSA_HPG]
            for h in range(1, NSA_HPG):
                psum = psum + p[g * NSA_HPG + h]
            imp_sc[g * tq:(g + 1) * tq, :] = _dot01(psum, ov)

    widths = list(range(LANES, nc, LANES)) + [nc]
    variant = jnp.clip(((t0 + tq) // CMP_STRIDE - 2) // LANES, 0, len(widths) - 1)
    for v, w in enumerate(widths):
        @pl.when(variant == v)
        def _():
            cmp_branch(w)

    cur = (t0 + (_iota((1, NSA_GROUPS * tq), 1) & (tq - 1))) >> 6
    sel = _topk_select(imp_sc[...], cur)
    selneg = jnp.where(sel > 0.0, 0.0, NEG).astype(BF16)
    for hh in range(nh):
        g = hh // NSA_HPG
        qa_sc[hh * tq:(hh + 1) * tq, LANES:LANES + nb] = selneg[g * tq:(g + 1) * tq]

    def load_v(ref, start, width):
        v128 = ref[pl.ds(start, width), LANES:2 * LANES]
        lane_half = _iota(v128.shape, 1) >> 6
        return [jnp.where(lane_half == g, v128, jnp.ones_like(v128)) for g in range(NSA_GROUPS)]

    def online(s3, vaug, first):
        width = s3.shape[-1]
        tmax = jnp.max(s3, axis=-1, keepdims=True)
        if first:
            m_new = jnp.broadcast_to(tmax, (nh, tq, LANES))
        else:
            m_old = m_sc[...]
            m_new = jnp.maximum(m_old, tmax)
            alpha = jnp.exp2(m_old - m_new)
        pexp = jnp.concatenate([jnp.exp2((s3[:, :, c * LANES:(c + 1) * LANES] - m_new).astype(BF16))
                                for c in range(width // LANES)], axis=-1)
        for hh in range(nh):
            pv = _dot(pexp[hh], vaug[hh // NSA_HPG])
            acc_sc[hh] = pv if first else alpha[hh] * acc_sc[hh] + pv
        m_sc[...] = m_new

    def finish(branch):
        for hh in range(nh):
            acc = acc_sc[hh]
            den = pltpu.roll(acc, HEAD_DIM, 1)
            out_sc[hh] = out_sc[hh] + gate(branch, hh) * (acc / den)

    def slc_scores(start):
        kaug = jnp.concatenate([kvs_ref[pl.ds(start, tk), 0:LANES], et_ref[pl.ds(start, tk), :]], axis=1)
        return _dot_nt(qa_sc[...], kaug).reshape(nh, tq, tk)

    n_full = i // (tk // tq)
    last = pl.multiple_of(n_full * tk, tk)
    s3 = slc_scores(last)
    s3 = jnp.where(last + _iota((1, tq, tk), 2) <= trow, s3, NEG)
    online(s3, load_v(kvs_ref, last, tk), True)

    def slc_step(j, carry):
        start = pl.multiple_of(j * tk, tk)
        online(slc_scores(start), load_v(kvs_ref, start, tk), False)
        return carry

    lax.fori_loop(0, n_full, slc_step, 0)
    finish(1)

    d0 = pl.multiple_of(t0, tq)
    s3 = _dot_nt(qa_sc[:, 0:LANES], kvw_ref[pl.ds(d0, tq), 0:LANES]).reshape(nh, tq, tq)
    s3 = jnp.where(_iota((1, tq, tq), 2) <= _iota((1, tq, tq), 1), s3, NEG)
    online(s3, load_v(kvw_ref, d0, tq), True)

    @pl.when(i >= 1)
    def _():
        start = pl.multiple_of(jnp.maximum(t0 - WINDOW, 0), tq)
        s3 = _dot_nt(qa_sc[:, 0:LANES], kvw_ref[pl.ds(start, WINDOW), 0:LANES]).reshape(nh, tq, WINDOW)
        kpos = start + _iota((1, tq, WINDOW), 2)
        s3 = jnp.where((kpos < t0) & (trow - kpos < WINDOW), s3, NEG)
        online(s3, load_v(kvw_ref, start, WINDOW), False)

    finish(2)

    for pair in range(nh // 2):
        parts = []
        for hh in (2 * pair, 2 * pair + 1):
            x = out_sc[hh]
            if hh % 2 != hh // NSA_HPG:
                x = pltpu.roll(x, HEAD_DIM, 1)
            parts.append(x)
        o_ref[:, pair * LANES:(pair + 1) * LANES] = jnp.where(half == 0, parts[0], parts[1]).astype(o_ref.dtype)


def _nsa_prompt(q, gates, ckv, kvs_b, kvw_b, ov, et, tq, tk):
    b, l, _ = q.shape
    nb = l // SLC_BLOCK
    assert l % tk == 0 and tk % tq == 0 and WINDOW % tq == 0 and l >= WINDOW
    blk = lambda w: pl.BlockSpec((None, tq, w), lambda bi, i: (bi, i, 0))
    seq = lambda a: pl.BlockSpec((None,) + a.shape[1:], lambda bi, i: (bi, 0, 0))
    full = lambda a: pl.BlockSpec(a.shape, lambda bi, i: (0, 0))
    hs = (NSA_HEADS, tq, LANES)
    return pl.pallas_call(
        functools.partial(_nsa_prompt_body, tq=tq, tk=tk, n_cmp=l // CMP_STRIDE - 1),
        grid=(b, l // tq),
        in_specs=[blk(NQ), blk(LANES), seq(ckv), seq(kvs_b), seq(kvw_b), full(ov), full(et)],
        out_specs=blk(NQ),
        out_shape=jax.ShapeDtypeStruct((b, l, NQ), BF16),
        scratch_shapes=[pltpu.VMEM((NSA_HEADS * tq, LANES + nb), BF16), pltpu.VMEM(hs, F32), pltpu.VMEM(hs, F32),
                        pltpu.VMEM(hs, F32), pltpu.VMEM((NSA_GROUPS * tq, nb), F32)],
        compiler_params=_params("parallel", "arbitrary"),
        name="nsa_prompt",
    )(q, gates, ckv, kvs_b, kvw_b, ov, et)


def _nsa_sample_body(pt_ref, cache_ref, q_ref, gate_ref, ckv_ref, win_ref, ns_ref, nw_ref, ov_ref, e_ref, o_ref,
                     page_sc, sem, *, n_steps, n_pages, nseq, s_new, n_cmp, p_len):
    i = pl.program_id(0)
    npg = nseq * n_pages

    def fetch(step, slot):
        for s in range(nseq):
            for k in range(n_pages):
                pltpu.make_async_copy(cache_ref.at[pt_ref[step * nseq + s, k]], page_sc.at[slot, s * n_pages + k],
                                      sem.at[slot]).start()

    def wait(slot):
        for j in range(npg):
            pltpu.make_async_copy(cache_ref.at[0], page_sc.at[slot, j], sem.at[slot]).wait()

    @pl.when(i == 0)
    def _():
        fetch(0, 0)

    for slot in range(2):
        @pl.when(i % 2 == slot)
        def _():
            @pl.when(i + 1 < n_steps)
            def _():
                fetch(i + 1, 1 - slot)

            wait(slot)
            _nsa_sample_attend([page_sc.at[slot, j] for j in range(npg)], q_ref, gate_ref, ckv_ref, win_ref, ns_ref,
                               nw_ref, ov_ref, e_ref, o_ref, n_pages=n_pages, nseq=nseq, s_new=s_new, n_cmp=n_cmp,
                               p_len=p_len)


def _nsa_sample_attend(page_refs, q_ref, gate_ref, ckv_ref, win_ref, ns_ref, nw_ref, ov_ref, e_ref, o_ref,
                       *, n_pages, nseq, s_new, n_cmp, p_len):
    rows1 = NSA_HEADS * s_new
    rows = nseq * rows1
    nc = ckv_ref.shape[1]
    nb = ov_ref.shape[1]
    w_buf = win_ref.shape[2]
    seqs = range(nseq)
    of = lambda x, s: x[s * rows1:(s + 1) * rows1]
    cat = lambda xs: xs[0] if len(xs) == 1 else jnp.concatenate(xs, axis=0)
    q2 = [jnp.concatenate(_head_queries(q_ref[s], s_new), axis=0) for s in seqs]
    ridx = _iota((rows, 1), 0)
    trow = ridx & (s_new - 1)
    grow = _div_pow2(ridx, s_new * NSA_HPG) & (NSA_GROUPS - 1)
    lane_half = _iota((rows, LANES), 1) >> 6

    def gate_rows(branch):
        cols = [gate_ref[s][:, branch * NSA_HEADS + hh:branch * NSA_HEADS + hh + 1]
                for s in seqs for hh in range(NSA_HEADS)]
        return jnp.concatenate(cols, axis=0)

    def pick(acc, den):
        return jnp.where(lane_half == grow, acc, 0.0) / den

    def attend(s_old, s_nw, vt_old, new_rows):
        s_old = cat(s_old)
        s_nw = jnp.where(_iota((rows, s_new), 1) <= trow, cat(s_nw), NEG)
        m = jnp.maximum(jnp.max(s_old, axis=-1, keepdims=True), jnp.max(s_nw, axis=-1, keepdims=True))
        e1 = jnp.exp2(s_old - m)
        e2 = jnp.exp2(s_nw - m)
        den = jnp.sum(e1, axis=-1, keepdims=True) + jnp.sum(e2, axis=-1, keepdims=True)
        e1 = e1.astype(BF16)
        e2 = e2.astype(BF16)
        acc = cat([_dot_nt(vt_old[s], of(e1, s)).T + _dot(of(e2, s), new_rows[s][:, LANES:2 * LANES]) for s in seqs])
        return pick(acc, den)

    ncol = _iota((rows, nc), 1)
    cvalid = (ncol * CMP_STRIDE + CMP_LEN <= p_len + trow + 1) & (ncol < n_cmp)
    p_c = _masked_softmax2(cat([_dot_nt(q2[s], ckv_ref[s, :, 0:LANES]) for s in seqs]), cvalid)
    o_c = cat([_dot(of(p_c, s).astype(BF16), ckv_ref[s, :, LANES:2 * LANES]) for s in seqs])
    out = gate_rows(0) * jnp.where(lane_half == grow, o_c, 0.0)
    psum = []
    for s in seqs:
        for g in range(NSA_GROUPS):
            acc = jnp.zeros((s_new, nc), F32)
            for h in range(NSA_HPG):
                r0 = s * rows1 + (g * NSA_HPG + h) * s_new
                acc = acc + p_c[r0:r0 + s_new]
            psum.append(acc)
    n_imp = nseq * NSA_GROUPS * s_new
    psum.append(jnp.zeros((LANES - n_imp, nc), F32))
    imp = _dot01(jnp.concatenate(psum, axis=0), ov_ref[...])
    cur = jnp.full((1, LANES), p_len // SLC_BLOCK, jnp.int32)
    sel = _topk_select(imp, cur)[0:n_imp]
    selneg = jnp.where(sel > 0.0, 0.0, NEG).astype(BF16)
    selneg_rows = [jnp.concatenate([selneg[(s * NSA_GROUPS + g) * s_new:(s * NSA_GROUPS + g + 1) * s_new]
                                    for g in range(NSA_GROUPS) for _ in range(NSA_HPG)], axis=0)
                   for s in seqs]

    e_hot = e_ref[...]
    s_past, s_nw, vts, news = [], [], [], []
    for s in seqs:
        kvt = jnp.concatenate([r[...] for r in page_refs[s * n_pages:(s + 1) * n_pages]],
                              axis=1).astype(BF16)
        kaug = jnp.concatenate([kvt[0:LANES], e_hot], axis=0)
        s_past.append(_dot(jnp.concatenate([q2[s], selneg_rows[s]], axis=1), kaug))
        ns = ns_ref[s].astype(BF16)
        s_nw.append(_dot_nt(q2[s], ns[:, 0:LANES]))
        vts.append(kvt[LANES:2 * LANES])
        news.append(ns)
    out = out + gate_rows(1) * attend(s_past, s_nw, vts, news)

    scol = _iota((rows1, w_buf), 1)
    tr1 = _iota((rows1, 1), 0) & (s_new - 1)
    wmask = (w_buf + tr1 - scol < WINDOW) & (p_len - w_buf + scol >= 0)
    s_b, s_nw, vts, news = [], [], [], []
    for s in seqs:
        wt = win_ref[s].astype(BF16)
        nw = nw_ref[s].astype(BF16)
        s_b.append(jnp.where(wmask, _dot(q2[s], wt[0:LANES]), NEG))
        s_nw.append(_dot_nt(q2[s], nw[:, 0:LANES]))
        vts.append(wt[LANES:2 * LANES])
        news.append(nw)
    out = out + gate_rows(2) * attend(s_b, s_nw, vts, news)

    lh = _iota((s_new, LANES), 1) >> 6
    for s in seqs:
        for pair in range(NSA_HEADS // 2):
            parts = []
            for hh in (2 * pair, 2 * pair + 1):
                x = out[s * rows1 + hh * s_new:s * rows1 + (hh + 1) * s_new]
                if hh % 2 != hh // NSA_HPG:
                    x = jnp.concatenate([x[:, HEAD_DIM:], x[:, :HEAD_DIM]], axis=1)
                parts.append(x)
            o_ref[s, :, pair * LANES:(pair + 1) * LANES] = jnp.where(lh == 0, parts[0], parts[1]).astype(o_ref.dtype)


def _nsa_sample(q, gates, ckv, cache_t, page_table, win_t, new_s, new_w, ov, e_all):
    bd, s_new, _ = q.shape
    n_pages = page_table.shape[1]
    p_len = n_pages * PAGE_SIZE
    assert s_new & (s_new - 1) == 0
    nseq = _tile_rows(bd, SAMPLE_SEQS_PER_STEP)
    full = lambda a: pl.BlockSpec(a.shape, lambda i, pt: (0, 0))
    seq = lambda a: pl.BlockSpec((nseq,) + a.shape[1:], lambda i, pt: (i, 0, 0))
    return pl.pallas_call(
        functools.partial(_nsa_sample_body, n_steps=bd // nseq, n_pages=n_pages, nseq=nseq, s_new=s_new,
                          n_cmp=p_len // CMP_STRIDE - 1, p_len=p_len),
        grid_spec=pltpu.PrefetchScalarGridSpec(
            num_scalar_prefetch=1,
            grid=(bd // nseq,),
            in_specs=[pl.BlockSpec(memory_space=pl.ANY), seq(q), seq(gates), seq(ckv), seq(win_t), seq(new_s),
                      seq(new_w), full(ov), full(e_all)],
            out_specs=pl.BlockSpec((nseq, s_new, NQ), lambda i, pt: (i, 0, 0)),
            scratch_shapes=[pltpu.VMEM((2, nseq * n_pages, KV_W, PAGE_SIZE), F32), pltpu.SemaphoreType.DMA((2,))],
        ),
        out_shape=jax.ShapeDtypeStruct((bd, s_new, NQ), BF16),
        compiler_params=_params("arbitrary"),
        name="nsa_sample",
    )(page_table, cache_t, q, gates, ckv, win_t, new_s, new_w, ov, e_all)


def _gla_body(q_ref, k_ref, v_ref, lg_ref, r_ref, gn_ref, s0_ref, o_ref, sfin_ref, s_sc, *, ns, tb, chunk, sub):
    j = pl.program_id(1)

    @pl.when(j == 0)
    def _():
        s_sc[...] = s0_ref[...]

    nchunk = tb // chunk
    nsub = chunk // sub
    ri = _iota((tb, tb), 0)
    ci = _iota((tb, tb), 1)
    tril = jnp.where((_div_pow2(ri, chunk) == _div_pow2(ci, chunk)) & (ci <= ri), 1.0, 0.0).astype(BF16)
    head_of_lane = _div_pow2(_iota((1, GQK), 1), GLA_DK)
    rows = _iota((chunk, GQK), 0)
    ones_dv = jnp.ones((chunk, GLA_DV), BF16)
    vhead = _div_pow2(_iota((chunk, GV), 1), GLA_DV)
    gn = gn_ref[...]

    units = [(si, c) for si in range(ns) for c in range(nchunk)]
    b_seq = [_dot01_l(tril, lg_ref[si]) for si in range(ns)]
    bt_seq = [b.T for b in b_seq] if tb % LANES == 0 else None
    pre = {}
    for si, c in units:
        c0 = c * chunk
        bc = b_seq[si][c0:c0 + chunk]
        qc = q_ref[si, c0:c0 + chunk, :]
        kc = k_ref[si, c0:c0 + chunk, :]
        b_last = bc[chunk - 1:chunk]
        qdb = qc * jnp.exp(bc)
        kdl = kc * jnp.exp(b_last - bc)
        if bt_seq is not None:
            decay = jnp.exp(jnp.broadcast_to(bt_seq[si][:, c0 + chunk - 1:c0 + chunk], (GQK, GLA_DV)))
        else:
            decay = jnp.exp(_dot01_tn(jnp.where(rows == chunk - 1, bc, 0.0), ones_dv))
        kds = []
        for sb in range(nsub):
            rr = sb * sub
            ref_row = bc[rr - 1:rr] if sb else jnp.zeros((1, GQK), F32)
            vis = rows < rr + sub
            kds.append((ref_row, jnp.where(vis, kc * jnp.exp(jnp.where(vis, ref_row - bc, 0.0)), 0.0).astype(BF16)))
        vc = v_ref[si, c0:c0 + chunk, :].astype(BF16)
        vbd = jnp.concatenate([jnp.where(vhead == h, vc, jnp.zeros_like(vc)) for h in range(GLA_HEADS)], axis=0)
        o_sub = []
        for sb in range(nsub):
            rr = sb * sub
            ref_row, kd = kds[sb]
            qd = qc[rr:rr + sub] * jnp.exp(bc[rr:rr + sub] - ref_row)
            qst = jnp.concatenate([jnp.where(head_of_lane == h, qd, 0.0) for h in range(GLA_HEADS)], axis=0)
            a = _dot_nt(qst.astype(BF16), kd)
            qrow = _iota((GLA_HEADS * sub, chunk), 0) & (sub - 1)
            a = jnp.where(_iota((GLA_HEADS * sub, chunk), 1) <= rr + qrow, a, 0.0)
            a_cat = jnp.concatenate([a[h * sub:(h + 1) * sub] for h in range(GLA_HEADS)],
                                    axis=1).astype(BF16)
            o_sub.append(a_cat)
        o_intra = _dot(o_sub[0] if nsub == 1 else jnp.concatenate(o_sub, axis=0), vbd)
        qdb_st = jnp.concatenate([jnp.where(head_of_lane == h, qdb, 0.0) for h in range(GLA_HEADS)],
                                 axis=0).astype(BF16)
        kv = _dot_tn(kdl.astype(BF16), vc)
        upd = jnp.concatenate([kv[h * GLA_DK:(h + 1) * GLA_DK, h * GLA_DV:(h + 1) * GLA_DV]
                               for h in range(GLA_HEADS)], axis=0)
        pre[si, c] = (decay, upd, o_intra, qdb_st)

    for si in range(ns):
        state = s_sc[si]
        for c in range(nchunk):
            c0 = c * chunk
            decay, upd, o_intra, qdb_st = pre[si, c]
            o_inter = _dot(qdb_st, state.astype(BF16))
            for h in range(GLA_HEADS):
                hs = slice(h * GLA_DV, (h + 1) * GLA_DV)
                o = o_inter[h * chunk:(h + 1) * chunk] + o_intra[:, hs]
                o_ref[si, c0:c0 + chunk, hs] = (_rms(o, gn) * r_ref[si, c0:c0 + chunk, hs].astype(F32)).astype(o_ref.dtype)
            state = decay * state + upd
        s_sc[si] = state

    @pl.when(j == pl.num_programs(1) - 1)
    def _():
        sfin_ref[...] = s_sc[...]


def _gla(q, k, v, lg, r, gn, s0, ns, tb, chunk):
    s, t, _ = q.shape
    sub = min(GLA_SUB, chunk)
    row = lambda w: pl.BlockSpec((ns, tb, w), lambda gi, j: (gi, j, 0))
    st = pl.BlockSpec((ns, GQK, GLA_DV), lambda gi, j: (gi, 0, 0))
    return pl.pallas_call(
        functools.partial(_gla_body, ns=ns, tb=tb, chunk=chunk, sub=sub),
        grid=(s // ns, t // tb),
        in_specs=[row(GQK), row(GQK), row(GV), row(GQK), row(GV), pl.BlockSpec(gn.shape, lambda gi, j: (0, 0)), st],
        out_specs=[row(GV), st],
        out_shape=[jax.ShapeDtypeStruct((s, t, GV), BF16), jax.ShapeDtypeStruct((s, GQK, GLA_DV), F32)],
        scratch_shapes=[pltpu.VMEM((ns, GQK, GLA_DV), F32)],
        compiler_params=_params("parallel", "arbitrary"),
        name="gla",
    )(q, k, v, lg, r, gn, s0)


def _mix_body(x_ref, sc_ref, sh_ref, ga_ref, gpre_ref, gpost_ref, on_ref, og_ref, wm_ref, wn_ref, wgo_ref, wo_ref,
              o_ref):
    x = x_ref[...]
    d = x.shape[1]
    h = _norm_mod(x, gpre_ref[...], sc_ref[...], sh_ref[...]).astype(BF16)
    m = _sigmoid(_dot(h, wm_ref[...]))
    y_a = _dot(on_ref[...], wn_ref[...])
    y_b = _dot(og_ref[...], wgo_ref[...])
    mixin = (m[:, 0:d] * y_a + m[:, d:2 * d] * y_b).astype(BF16)
    mix = _dot(mixin, wo_ref[...])
    o_ref[...] = x + ga_ref[...] * _rms(mix, gpost_ref[...])


def _mix(x, sc, sh, ga, gpre, gpost, o_nsa, o_gla, wm, wn, wgo, wo, tm, rows_per_mod):
    n, d = x.shape
    row = lambda w: pl.BlockSpec((tm, w), lambda i: (i, 0))
    full = lambda a: pl.BlockSpec(a.shape, lambda i: (0, 0))
    ms = lambda a: _mod_spec(a, tm, rows_per_mod)
    return pl.pallas_call(
        _mix_body,
        grid=(n // tm,),
        in_specs=[row(d), ms(sc), ms(sh), ms(ga), full(gpre), full(gpost), row(NQ), row(GV),
                  full(wm), full(wn), full(wgo), full(wo)],
        out_specs=row(d),
        out_shape=jax.ShapeDtypeStruct((n, d), F32),
        compiler_params=_params("parallel"),
        name="mix",
    )(x, sc, sh, ga, gpre, gpost, o_nsa, o_gla, wm, wn, wgo, wo)


def _ffn_body(x_ref, sc_ref, sh_ref, ga_ref, gpre_ref, gpost_ref, wg_ref, wu_ref, wd_ref, o_ref, h_sc, acc_sc):
    j = pl.program_id(1)

    @pl.when(j == 0)
    def _():
        h_sc[...] = _norm_mod(x_ref[...], gpre_ref[...], sc_ref[...], sh_ref[...]).astype(BF16)
        acc_sc[...] = jnp.zeros(acc_sc.shape, F32)

    h = h_sc[...]
    gt = _dot(h, wg_ref[...])
    up = _dot(h, wu_ref[...])
    a = (gt * _sigmoid(gt) * up).astype(BF16)
    acc_sc[...] += _dot(a, wd_ref[...])

    @pl.when(j == pl.num_programs(1) - 1)
    def _():
        o_ref[...] = x_ref[...] + ga_ref[...] * _rms(acc_sc[...], gpost_ref[...])


def _ffn(x, sc, sh, ga, gpre, gpost, wg, wu, wd, tm, rows_per_mod):
    n, d = x.shape
    dff = wg.shape[1]
    tf = dff // 2 if (dff // 2) % LANES == 0 else dff
    row = pl.BlockSpec((tm, d), lambda i, j: (i, 0))
    full = lambda a: pl.BlockSpec(a.shape, lambda i, j: (0, 0))
    ms = lambda a: pl.BlockSpec((None, a.shape[1], a.shape[2]), lambda i, j: ((i * tm) // rows_per_mod, 0, 0))
    return pl.pallas_call(
        _ffn_body,
        grid=(n // tm, dff // tf),
        in_specs=[row, ms(sc), ms(sh), ms(ga), full(gpre), full(gpost),
                  pl.BlockSpec((d, tf), lambda i, j: (0, j)), pl.BlockSpec((d, tf), lambda i, j: (0, j)),
                  pl.BlockSpec((tf, d), lambda i, j: (j, 0))],
        out_specs=row,
        out_shape=jax.ShapeDtypeStruct((n, d), F32),
        scratch_shapes=[pltpu.VMEM((tm, d), BF16), pltpu.VMEM((tm, d), F32)],
        compiler_params=_params("parallel", "arbitrary"),
        name="ffn",
    )(x, sc, sh, ga, gpre, gpost, wg, wu, wd)


def _split_w_in(w_in, d):
    sizes = (NQ, KV_W, KV_W, KV_W, 3 * NSA_HEADS, GQK, GQK, GV, 16, GV, 2 * d)
    out, o = [], 0
    for s in sizes:
        out.append(w_in[:, o:o + s])
        o += s
    return out


def _pad_cols(w, n):
    return jnp.pad(w, ((0, 0), (0, n - w.shape[1])))


def _compress_weights(w1k, b1k, w2k, pek, w1v, b1v, w2v, pev):
    hid = w2k.shape[0]
    npair = CMP_STRIDE // 2
    eye = np.eye(NSA_GROUPS, dtype=np.float32)

    def group_diag(w):
        out = jnp.einsum('gh,...ab->...gahb', eye, w)
        return out.reshape(w.shape[:-2] + (NSA_GROUPS * w.shape[-2], NSA_GROUPS * w.shape[-1]))

    def expand_w1(half):
        per_kv = [group_diag(w1.reshape(CMP_LEN, HEAD_DIM, hid)[half * CMP_STRIDE:(half + 1) * CMP_STRIDE])
                  for w1 in (w1k, w1v)]
        return jnp.stack(per_kv).reshape(2, npair, 2 * NSA_GROUPS * HEAD_DIM, NSA_GROUPS * hid)

    w1 = jnp.concatenate([expand_w1(0), expand_w1(1)], axis=-1).astype(BF16)

    def expand_pe(half):
        per_kv = [jnp.tile(pe[half * CMP_STRIDE:(half + 1) * CMP_STRIDE], (1, NSA_GROUPS)) for pe in (pek, pev)]
        return jnp.stack(per_kv).reshape(2, npair, 2 * NSA_GROUPS * HEAD_DIM)

    pe = jnp.pad(jnp.stack([expand_pe(0), expand_pe(1)], axis=2),
                 ((0, 0), (0, 0), (0, PE_ROWS - 2), (0, 0))).astype(BF16)

    b1 = jnp.concatenate([jnp.tile(b1k, NSA_GROUPS), jnp.tile(b1v, NSA_GROUPS)]).reshape(1, -1)
    w2d = [group_diag(w2k), group_diag(w2v)]
    zero = jnp.zeros_like(w2d[0])
    w2 = jnp.concatenate([jnp.concatenate([w2d[0], zero], axis=1),
                          jnp.concatenate([zero, w2d[1]], axis=1)], axis=0).astype(BF16)
    return pe, w1, b1, w2


def _overlap(n_cmp_pad, n_cmp, nb):
    n = np.arange(n_cmp_pad)[:, None]
    j = np.arange(nb)[None, :] * SLC_BLOCK
    ok = (n * CMP_STRIDE < j + SLC_BLOCK) & (n * CMP_STRIDE + CMP_LEN > j) & (n < n_cmp)
    return ok.astype(BF16)


def _block_onehot(nb, nkeys):
    return (np.arange(nb)[:, None] == (np.arange(nkeys)[None, :] // SLC_BLOCK)).astype(BF16)


def _tile_rows(n, cap):
    t = cap
    while n % t:
        t //= 2
    return t


def _feature_major(a):
    nd = a.ndim
    a = jnp.moveaxis(a, nd - 4, nd - 1)
    return a.reshape(a.shape[:nd - 4] + (KV_W, a.shape[-1]))


def _row_major(a_t):
    b, _, r = a_t.shape
    return jnp.moveaxis(a_t.reshape(b, 2, NSA_GROUPS, HEAD_DIM, r), 4, 1)


def kernel(x_prompt, x_sample, cache_cmp_kv, cache_slc_kv, state_win_kv, state_gla, page_table, c_prompt, c_sample, w_ada, b_ada, g_pre_mix, g_post_mix, w_in, cmp_w1_k, cmp_b1_k, cmp_w2_k, cmp_pe_k, cmp_w1_v, cmp_b1_v, cmp_w2_v, cmp_pe_v, w_gla_a2, b_gla_a, gla_norm_g, w_nsa_o, w_gla_o, w_out, g_pre_ffn, g_post_ffn, w_ff_gate, w_ff_up, w_ff_down):
    depth = w_ada.shape[0]
    bp, l, d = x_prompt.shape
    bd, s_new, _ = x_sample.shape
    n_pages = page_table.shape[1]
    p_len = n_pages * PAGE_SIZE
    w_buf = state_win_kv.shape[2]
    keep = min(WINDOW, l)
    np_, ns_ = bp * l, bd * s_new
    tmp = _tile_rows(np_ // bp, 512)
    tms = _tile_rows(ns_, 512)
    tq = _tile_rows(l, 256)
    tk = 2 * tq

    yp, ys = x_prompt.reshape(np_, d), x_sample.reshape(ns_, d)
    col = [[] for _ in range(8)]
    for li in range(depth):
        r_all = bp + bd
        r_pad = -(-r_all // 8) * 8
        c_all = jnp.pad(jnp.concatenate([c_prompt, c_sample], axis=0), ((0, r_pad - r_all), (0, 0)))
        mod = _ada(c_all, w_ada[li], b_ada[li])
        mods_p = [m.reshape(bp, 1, d) for m in jnp.split(mod[:bp], 6, axis=-1)]
        mods_s = [jnp.repeat(m, s_new, axis=0).reshape(ns_ // tms, tms, d) for m in jnp.split(mod[bp:r_all], 6, axis=-1)]

        (w_q, w_kc, w_ks, w_kw, w_gn, w_qg, w_kg, w_vg, w_ag, w_rg, w_mg) = _split_w_in(w_in[li], d)
        wq = w_q.astype(BF16)
        wkv = jnp.concatenate([w_kc, w_ks, w_kw], axis=1).astype(BF16)
        wgn = _pad_cols(w_gn, LANES).astype(BF16)
        wqkv = jnp.concatenate([w_qg, w_kg, w_vg], axis=1).astype(BF16)
        wa = _pad_cols(w_ag, LANES).astype(BF16)
        wr = w_rg.astype(BF16)
        a2 = jnp.pad(w_gla_a2[li], ((0, LANES - w_gla_a2.shape[1]), (0, 0)))
        ba = b_gla_a[li].reshape(1, GQK)
        cw = _compress_weights(cmp_w1_k[li], cmp_b1_k[li], cmp_w2_k[li], cmp_pe_k[li],
                               cmp_w1_v[li], cmp_b1_v[li], cmp_w2_v[li], cmp_pe_v[li])
        gpre = g_pre_mix[li].reshape(1, d)
        gpost = g_post_mix[li].reshape(1, d)
        gpre2 = g_pre_ffn[li].reshape(1, d)
        gpost2 = g_post_ffn[li].reshape(1, d)
        gn = gla_norm_g[li].reshape(1, GLA_DV)
        wm = w_mg.astype(BF16)
        wn = w_nsa_o[li].astype(BF16)
        wgo = w_gla_o[li].astype(BF16)
        wo = w_out[li].astype(BF16)
        wfg = w_ff_gate[li].astype(BF16)
        wfu = w_ff_up[li].astype(BF16)
        wfd = w_ff_down[li].astype(BF16)

        sh1, sc1, ga1, sh2, sc2, ga2 = mods_p
        (q, kvc, _, _, kvs_b, kvw_b, gates, kvc_t, kvs_t, kvw_t) = _pre_nsa(yp, sc1, sh1, gpre, wq, wkv, wgn, tmp, l,
                                                                            feature_major=(bp, l))
        qg, kg, vg, lg, rg = _pre_gla(yp, sc1, sh1, gpre, wqkv, wa, wr, a2, ba, tmp, l)
        ckv = _cmp_prompt(kvc.reshape(bp, l, KV_W), cw)
        nch = l // CMP_STRIDE
        nb = l // SLC_BLOCK
        o_nsa = _nsa_prompt(q.reshape(bp, l, NQ), gates.reshape(bp, l, LANES), ckv, kvs_b.reshape(bp, l, KV_W),
                            kvw_b.reshape(bp, l, KV_W), _overlap(nch, nch - 1, nb), _block_onehot(nb, l).T, tq, tk)
        chunk = GLA_CHUNK if l % GLA_CHUNK == 0 else l
        tb = 2 * LANES if (l % (2 * LANES) == 0 and LANES % chunk == 0) else chunk
        o_gla, s_fin_p = _gla(qg.reshape(bp, l, GQK), kg.reshape(bp, l, GQK), vg.reshape(bp, l, GV),
                              lg.reshape(bp, l, GQK), rg.reshape(bp, l, GV), gn,
                              jnp.zeros((bp, GQK, GLA_DV), F32), bp, tb, chunk)
        x1 = _mix(yp, sc1, sh1, ga1, gpre, gpost, o_nsa.reshape(np_, NQ), o_gla.reshape(np_, GV), wm, wn, wgo, wo, tmp, l)
        yp = _ffn(x1, sc2, sh2, ga2, gpre2, gpost2, wfg, wfu, wfd, tmp, l)
        col[0].append(_row_major(kvc_t))
        col[2].append(_row_major(kvs_t))
        col[4].append(_row_major(kvw_t[:, :, l - keep:]))
        col[6].append(s_fin_p.reshape(bp, GLA_HEADS, GLA_DK, GLA_DV).astype(state_gla.dtype))

        sh1, sc1, ga1, sh2, sc2, ga2 = mods_s
        q, kvc, kvs, kvw, _, _, gates = _pre_nsa(ys, sc1, sh1, gpre, wq, wkv, wgn, tms, tms)
        qg, kg, vg, lg, rg = _pre_gla(ys, sc1, sh1, gpre, wqkv, wa, wr, a2, ba, tms, tms)
        ckv = _cmp_sample(_feature_major(cache_cmp_kv[li]), page_table, cw)
        nch = p_len // CMP_STRIDE
        nb = p_len // SLC_BLOCK
        win_t = _feature_major(state_win_kv[li])
        kvw3 = kvw.reshape(bd, s_new, KV_W)
        o_nsa = _nsa_sample(q.reshape(bd, s_new, NQ), gates.reshape(bd, s_new, LANES), ckv,
                            _feature_major(cache_slc_kv[li]), page_table, win_t, kvs.reshape(bd, s_new, KV_W), kvw3,
                            _overlap(nch, nch - 1, nb), _block_onehot(nb, p_len))
        chunk = GLA_CHUNK if s_new % GLA_CHUNK == 0 else s_new
        o_gla, s_fin_s = _gla(qg.reshape(bd, s_new, GQK), kg.reshape(bd, s_new, GQK), vg.reshape(bd, s_new, GV),
                              lg.reshape(bd, s_new, GQK), rg.reshape(bd, s_new, GV), gn,
                              state_gla[li].astype(F32).reshape(bd, GQK, GLA_DV), _tile_rows(bd, 16), chunk, chunk)
        x1 = _mix(ys, sc1, sh1, ga1, gpre, gpost, o_nsa.reshape(ns_, NQ), o_gla.reshape(ns_, GV), wm, wn, wgo, wo, tms, tms)
        ys = _ffn(x1, sc2, sh2, ga2, gpre2, gpost2, wfg, wfu, wfd, tms, tms)
        win_new_t = jnp.concatenate([win_t, kvw3.transpose(0, 2, 1)], axis=2)[:, :, s_new:]
        win_new = jnp.moveaxis(win_new_t.reshape(bd, 2, NSA_GROUPS, HEAD_DIM, w_buf), 4, 1)
        col[1].append(kvc.reshape(bd, s_new, 2, NSA_GROUPS, HEAD_DIM))
        col[3].append(kvs.reshape(bd, s_new, 2, NSA_GROUPS, HEAD_DIM))
        col[5].append(win_new)
        col[7].append(s_fin_s.reshape(bd, GLA_HEADS, GLA_DK, GLA_DV).astype(state_gla.dtype))

    stacked = [c[0][None] if len(c) == 1 else jnp.stack(c) for c in col]
    return (yp.reshape(bp, l, d), ys.reshape(bd, s_new, d), *stacked)
```

```python
import functools
import math

import numpy as np
import jax
import jax.numpy as jnp
from jax import lax
from jax.experimental import pallas as pl
from jax.experimental.pallas import tpu as pltpu

F32 = jnp.float32
BF16 = jnp.bfloat16

NSA_HEADS = 8
NSA_GROUPS = 2
NSA_HPG = NSA_HEADS // NSA_GROUPS
HEAD_DIM = 64
CMP_STRIDE = 16
CMP_LEN = 32
SLC_BLOCK = 64
N_SEL = 16
WINDOW = 512
GLA_HEADS = 4
GLA_DK = 64
GLA_DV = 128
GLA_TAU = 16.0
GLA_CHUNK = 64
GLA_SUB = 16
PE_ROWS = 16
SAMPLE_SEQS_PER_STEP = 2
EPS = 1e-6
PAGE_SIZE = 128
NEG = -1e30
LOG2E = math.log2(math.e)
LANES = 128
VMEM_LIMIT = 56 * 1024 * 1024

KV_W = 2 * NSA_GROUPS * HEAD_DIM
NQ = NSA_HEADS * HEAD_DIM
GQK = GLA_HEADS * GLA_DK
GV = GLA_HEADS * GLA_DV


def _dot(a, b):
    return jnp.dot(a, b, preferred_element_type=F32)


def _dot_nt(a, b):
    return lax.dot_general(a, b, (((1,), (1,)), ((), ())), preferred_element_type=F32)


def _dot_f32(a, b):
    return jnp.dot(a, b, preferred_element_type=F32, precision=lax.Precision.HIGHEST)


def _dot01(x, m01):
    hi = x.astype(BF16)
    r1 = x - hi.astype(F32)
    mid = r1.astype(BF16)
    lo = (r1 - mid.astype(F32)).astype(BF16)
    return _dot(hi, m01) + _dot(mid, m01) + _dot(lo, m01)


def _dot01_l(m01, x):
    hi = x.astype(BF16)
    r1 = x - hi.astype(F32)
    mid = r1.astype(BF16)
    lo = (r1 - mid.astype(F32)).astype(BF16)
    return _dot(m01, hi) + _dot(m01, mid) + _dot(m01, lo)


def _dot_tn(a, b):
    return lax.dot_general(a, b, (((0,), (0,)), ((), ())), preferred_element_type=F32)


def _dot01_tn(x, m01):
    hi = x.astype(BF16)
    r1 = x - hi.astype(F32)
    mid = r1.astype(BF16)
    lo = (r1 - mid.astype(F32)).astype(BF16)
    return _dot_tn(hi, m01) + _dot_tn(mid, m01) + _dot_tn(lo, m01)


def _sigmoid(x):
    return 1.0 / (1.0 + jnp.exp(-x))


def _params(*sem):
    return pltpu.CompilerParams(dimension_semantics=sem, vmem_limit_bytes=VMEM_LIMIT)


def _iota(shape, dim):
    return lax.broadcasted_iota(jnp.int32, shape, dim)


def _div_pow2(x, n):
    assert n & (n - 1) == 0, n
    return x >> (n.bit_length() - 1)


def _ada_body(c_ref, w_ref, b_ref, o_ref):
    o_ref[...] = _dot_f32(c_ref[...], w_ref[...]) + b_ref[...]


def _ada(c_all, w_ada, b_ada):
    r, d = c_all.shape
    n = w_ada.shape[1]
    tn = 1536 if n % 1536 == 0 else n
    return pl.pallas_call(
        _ada_body,
        grid=(n // tn,),
        in_specs=[pl.BlockSpec((r, d), lambda j: (0, 0)),
                  pl.BlockSpec((d, tn), lambda j: (0, j)),
                  pl.BlockSpec((1, tn), lambda j: (0, j))],
        out_specs=pl.BlockSpec((r, tn), lambda j: (0, j)),
        out_shape=jax.ShapeDtypeStruct((r, n), F32),
        compiler_params=_params("arbitrary"),
        name="ada",
    )(c_all, w_ada, b_ada.reshape(1, n))


def _norm_mod(x, g, sc, sh):
    ms = jnp.mean(x * x, axis=-1, keepdims=True)
    y = x * lax.rsqrt(ms + EPS) * g
    return y * (1.0 + sc) + sh


def _rms(x, g):
    ms = jnp.mean(x * x, axis=-1, keepdims=True)
    return x * lax.rsqrt(ms + EPS) * g


def _pre_nsa_body(x_ref, sc_ref, sh_ref, g_ref, wq_ref, wkv_ref, wg_ref,
                  q_ref, kvc_ref, kvs_ref, kvw_ref, kvsb_ref, kvwb_ref, gate_ref, *t_refs):
    h = _norm_mod(x_ref[...], g_ref[...], sc_ref[...], sh_ref[...]).astype(BF16)
    q_ref[...] = (_dot(h, wq_ref[...]) * (HEAD_DIM ** -0.5 * LOG2E)).astype(BF16)
    kv = _dot(h, wkv_ref[...])
    kvc = kv[:, 0:KV_W]
    kvs = kv[:, KV_W:2 * KV_W]
    kvw = kv[:, 2 * KV_W:3 * KV_W]
    kvc_ref[...] = kvc
    kvs_ref[...] = kvs
    kvw_ref[...] = kvw
    kvsb_ref[...] = kvs.astype(BF16)
    kvwb_ref[...] = kvw.astype(BF16)
    gate_ref[...] = _sigmoid(_dot(h, wg_ref[...]))
    if t_refs:
        kvt = kv.T
        for j, t_ref in enumerate(t_refs):
            t_ref[...] = kvt[j * KV_W:(j + 1) * KV_W]


def _mod_spec(mod3, tm, rows_per_mod):
    r = mod3.shape[1]
    return pl.BlockSpec((None, r, mod3.shape[2]), lambda i: ((i * tm) // rows_per_mod, 0, 0))


def _pre_nsa(x, sc, sh, g, wq, wkv, wg, tm, rows_per_mod, feature_major=None):
    n, d = x.shape
    row = lambda w: pl.BlockSpec((tm, w), lambda i: (i, 0))
    full = lambda a: pl.BlockSpec(a.shape, lambda i: (0, 0))
    outs = [(NQ, BF16), (KV_W, F32), (KV_W, F32), (KV_W, F32), (KV_W, BF16), (KV_W, BF16), (LANES, F32)]
    out_specs = [row(w) for w, _ in outs]
    out_shape = [jax.ShapeDtypeStruct((n, w), t) for w, t in outs]
    if feature_major is not None:
        nbatch, per = feature_major
        tiles = per // tm
        out_specs += [pl.BlockSpec((None, KV_W, tm), lambda i: (i // tiles, 0, i % tiles))] * 3
        out_shape += [jax.ShapeDtypeStruct((nbatch, KV_W, per), F32)] * 3
    return pl.pallas_call(
        _pre_nsa_body,
        grid=(n // tm,),
        in_specs=[row(d), _mod_spec(sc, tm, rows_per_mod), _mod_spec(sh, tm, rows_per_mod), full(g),
                  full(wq), full(wkv), full(wg)],
        out_specs=out_specs,
        out_shape=out_shape,
        compiler_params=_params("parallel"),
        name="pre_nsa",
    )(x, sc, sh, g, wq, wkv, wg)


def _log_sigmoid(x):
    return jnp.minimum(x, 0.0) - jnp.log(1.0 + jnp.exp(-jnp.abs(x)))


def _pre_gla_body(x_ref, sc_ref, sh_ref, g_ref, wqkv_ref, wa_ref, wr_ref, a2_ref, ba_ref,
                  q_ref, k_ref, v_ref, lg_ref, r_ref):
    h = _norm_mod(x_ref[...], g_ref[...], sc_ref[...], sh_ref[...]).astype(BF16)
    qkv = _dot(h, wqkv_ref[...])
    q_ref[...] = qkv[:, 0:GQK] * GLA_DK ** -0.5
    k_ref[...] = qkv[:, GQK:2 * GQK]
    v_ref[...] = qkv[:, 2 * GQK:2 * GQK + GV]
    a = _dot(h, wa_ref[...])
    lg_ref[...] = _log_sigmoid(_dot_f32(a, a2_ref[...]) + ba_ref[...]) * (1.0 / GLA_TAU)
    r = _dot(h, wr_ref[...])
    r_ref[...] = (r * _sigmoid(r)).astype(BF16)


def _pre_gla(x, sc, sh, g, wqkv, wa, wr, a2, ba, tm, rows_per_mod):
    n, d = x.shape
    row = lambda w: pl.BlockSpec((tm, w), lambda i: (i, 0))
    full = lambda a: pl.BlockSpec(a.shape, lambda i: (0, 0))
    return pl.pallas_call(
        _pre_gla_body,
        grid=(n // tm,),
        in_specs=[row(d), _mod_spec(sc, tm, rows_per_mod), _mod_spec(sh, tm, rows_per_mod), full(g),
                  full(wqkv), full(wa), full(wr), full(a2), full(ba)],
        out_specs=[row(GQK), row(GQK), row(GV), row(GQK), row(GV)],
        out_shape=[jax.ShapeDtypeStruct((n, GQK), F32), jax.ShapeDtypeStruct((n, GQK), F32),
                   jax.ShapeDtypeStruct((n, GV), F32), jax.ShapeDtypeStruct((n, GQK), F32),
                   jax.ShapeDtypeStruct((n, GV), BF16)],
        compiler_params=_params("parallel"),
        name="pre_gla",
    )(x, sc, sh, g, wqkv, wa, wr, a2, ba)


def _gelu_tanh(x):
    return 0.5 * x * (1.0 + jnp.tanh(0.7978845608028654 * (x + 0.044715 * x * x * x)))


def _compress(get_xp, nch, pe_ref, w1_ref, b1_ref, w2_ref):
    first, second = [], []
    for kv in range(2):
        acc = jnp.zeros((nch + PE_ROWS, 2 * LANES), F32)
        for pp in range(CMP_STRIDE // 2):
            lhs = jnp.concatenate([get_xp(kv, 2 * pp), get_xp(kv, 2 * pp + 1)], axis=1).astype(BF16)
            acc = acc + _dot(jnp.concatenate([lhs, pe_ref[kv, pp]], axis=0), w1_ref[kv, pp])
        first.append(acc[0:nch, 0:LANES] + acc[nch:nch + 1, 0:LANES])
        second.append(acc[0:nch, LANES:2 * LANES] + acc[nch + 1:nch + 2, LANES:2 * LANES])
    first = jnp.concatenate(first, axis=1)
    second = jnp.concatenate(second, axis=1)
    if nch % 8 == 0:
        nxt = pltpu.roll(second, nch - 1, 0)
    else:
        nxt = jnp.concatenate([second[1:], second[:1]], axis=0)
    hid = _gelu_tanh(first + nxt + b1_ref[...])
    out = _dot(hid.astype(BF16), w2_ref[...])
    return jnp.where(_iota(out.shape, 0) < nch - 1, out, 0.0)


def _cmp_prompt_body(xk_ref, xv_ref, pe_ref, w1_ref, b1_ref, w2_ref, o_ref, *, nch):
    x_refs = (xk_ref, xv_ref)

    def get_xp(kv, p):
        return x_refs[kv][pl.ds(p, nch, stride=CMP_STRIDE), :]

    o_ref[...] = _compress(get_xp, nch, pe_ref, w1_ref, b1_ref, w2_ref).astype(o_ref.dtype)


def _cmp_prompt(kvc, cw):
    b, l, _ = kvc.shape
    nch = l // CMP_STRIDE
    full = lambda a: pl.BlockSpec(a.shape, lambda i: (0,) * a.ndim)
    return pl.pallas_call(
        functools.partial(_cmp_prompt_body, nch=nch),
        grid=(b,),
        in_specs=[pl.BlockSpec((None, l, LANES), lambda i: (i, 0, 0)),
                  pl.BlockSpec((None, l, LANES), lambda i: (i, 0, 1))] + [full(a) for a in cw],
        out_specs=pl.BlockSpec((None, nch, KV_W), lambda i: (i, 0, 0)),
        out_shape=jax.ShapeDtypeStruct((b, nch, KV_W), BF16),
        compiler_params=_params("parallel"),
        name="cmp_prompt",
    )(kvc, kvc, *cw)


def _cmp_sample_body(pt_ref, cache_ref, pe_ref, w1_ref, b1_ref, w2_ref, o_ref,
                     page_sc, ak_sc, av_sc, bk_sc, bv_sc, sem, *, n_seq, n_pages, nch):
    i = pl.program_id(0)
    cpp = PAGE_SIZE // CMP_STRIDE
    bufs = ((ak_sc, av_sc), (bk_sc, bv_sc))

    def page_copy(seq, slot, k):
        return pltpu.make_async_copy(cache_ref.at[pt_ref[seq, k]], page_sc.at[slot, k], sem.at[slot])

    def fetch(seq, slot):
        for k in range(n_pages):
            page_copy(seq, slot, k).start()

    def wait(slot):
        for k in range(n_pages):
            pltpu.make_async_copy(cache_ref.at[0], page_sc.at[slot, k], sem.at[slot]).wait()

    def regroup(slot):
        for k in range(n_pages):
            for kv in range(2):
                xt = page_sc[slot, k, kv * LANES:(kv + 1) * LANES, :].T
                for n in range(cpp):
                    for a in range(CMP_STRIDE // 8):
                        r0 = n * CMP_STRIDE + 8 * a
                        bufs[slot][kv][k, pl.ds(8 * a * cpp + n, 8, stride=cpp), :] = xt[r0:r0 + 8, :]

    def mlp(src):
        def get_xp(kv, p):
            return src[kv][:, p * cpp:(p + 1) * cpp, :].reshape(nch, LANES)

        o_ref[...] = _compress(get_xp, nch, pe_ref, w1_ref, b1_ref, w2_ref).astype(o_ref.dtype)

    @pl.when(i == 0)
    def _():
        fetch(0, 0)
        if n_seq == 1:
            page_sc[1] = jnp.zeros(page_sc.shape[1:], F32)
        bk_sc[...] = jnp.zeros(bk_sc.shape, F32)
        bv_sc[...] = jnp.zeros(bv_sc.shape, F32)

    for slot in range(2):
        @pl.when(i % 2 == slot)
        def _():
            @pl.when(i + 1 < n_seq)
            def _():
                fetch(i + 1, 1 - slot)

            @pl.when(i < n_seq)
            def _():
                wait(slot)

            regroup(slot)
            mlp(bufs[1 - slot])


def _cmp_sample(cache_t, page_table, cw):
    bd, n_pages = page_table.shape
    p_len = n_pages * PAGE_SIZE
    nch = p_len // CMP_STRIDE
    full = lambda a: pl.BlockSpec(a.shape, lambda i, pt: (0,) * a.ndim)
    return pl.pallas_call(
        functools.partial(_cmp_sample_body, n_seq=bd, n_pages=n_pages, nch=nch),
        grid_spec=pltpu.PrefetchScalarGridSpec(
            num_scalar_prefetch=1,
            grid=(bd + 1,),
            in_specs=[pl.BlockSpec(memory_space=pl.ANY)] + [full(a) for a in cw],
            out_specs=pl.BlockSpec((None, nch, KV_W), lambda i, pt: (jnp.maximum(i - 1, 0), 0, 0)),
            scratch_shapes=[pltpu.VMEM((2, n_pages, KV_W, PAGE_SIZE), F32)]
            + [pltpu.VMEM((n_pages, PAGE_SIZE, LANES), F32)] * 4 + [pltpu.SemaphoreType.DMA((2,))],
        ),
        out_shape=jax.ShapeDtypeStruct((bd, nch, KV_W), BF16),
        compiler_params=_params("arbitrary"),
        name="cmp_sample",
    )(page_table, cache_t, *cw)


def _head_queries(q, rows):
    half = _iota((rows, LANES), 1) >> 6
    out = []
    for hh in range(NSA_HEADS):
        g = hh // NSA_HPG
        blk = q[:, (hh // 2) * LANES:(hh // 2 + 1) * LANES].astype(F32)
        if hh % 2 != g:
            blk = pltpu.roll(blk, HEAD_DIM, 1) if rows % 8 == 0 else jnp.roll(blk, HEAD_DIM, 1)
        out.append(jnp.where(half == g, blk, 0.0).astype(BF16))
    return out


def _topk_select(imp, cur):
    impt = imp.T
    nb = impt.shape[0]
    jl = _iota(impt.shape, 0)
    jf = jl.astype(F32)
    forced = (jl == 0) | (jl == cur) | (jl == cur - 1)
    elig = (jl >= 1) & (jl <= cur - 2)
    dead = -1.0

    def step(_, vals):
        m = jnp.max(vals, axis=0, keepdims=True)
        idx = jnp.min(jnp.where(vals == m, jf, float(nb)), axis=0, keepdims=True)
        return jnp.where(jf == idx, dead, vals)

    vals = lax.fori_loop(0, N_SEL - 3, step, jnp.where(elig, impt, dead))
    return jnp.where(forced | (elig & (vals == dead)), 1.0, 0.0).T


def _masked_softmax2(s, valid):
    s = jnp.where(valid, s, NEG)
    m = jnp.max(s, axis=-1, keepdims=True)
    e = jnp.where(valid, jnp.exp2(s - m), 0.0)
    d = jnp.sum(e, axis=-1, keepdims=True)
    return e * (1.0 / jnp.where(d > 0.0, d, 1.0))


def _nsa_prompt_body(q_ref, gate_ref, ckv_ref, kvs_ref, kvw_ref, ov_ref, et_ref, o_ref, qa_sc, m_sc, acc_sc, out_sc, imp_sc,
                     *, tq, tk, n_cmp):
    i = pl.program_id(1)
    t0 = i * tq
    nc = ckv_ref.shape[0]
    nb = ov_ref.shape[1]
    nh = NSA_HEADS
    qh = _head_queries(q_ref[...], tq)
    for hh in range(nh):
        qa_sc[hh * tq:(hh + 1) * tq, 0:LANES] = qh[hh]
    half = _iota((tq, LANES), 1) >> 6
    trow = t0 + _iota((1, tq, 1), 1)
    gates = gate_ref[...]

    def gate(branch, hh):
        c = branch * nh + hh
        return gates[:, c:c + 1]

    def cmp_branch(w):
        s = _dot_nt(qa_sc[:, 0:LANES], ckv_ref[0:w, 0:LANES]).reshape(nh, tq, w)
        ncol = _iota((1, tq, w), 2)
        cvalid = (ncol * CMP_STRIDE + CMP_LEN <= trow + 1) & (ncol < n_cmp)
        p = _masked_softmax2(s, cvalid)
        cv = ckv_ref[0:w, LANES:2 * LANES]
        for hh in range(nh):
            out_sc[hh] = gate(0, hh) * _dot(p[hh].astype(BF16), cv)
        ov = ov_ref[0:w, :]
        for g in range(NSA_GROUPS):
            psum = p[g * NSA_HPG]
            for h in range(1, NSA_HPG):
                psum = psum + p[g * NSA_HPG + h]
            imp_sc[g * tq:(g + 1) * tq, :] = _dot01(psum, ov)

    widths = list(range(LANES, nc, LANES)) + [nc]
    variant = jnp.clip(((t0 + tq) // CMP_STRIDE - 2) // LANES, 0, len(widths) - 1)
    for v, w in enumerate(widths):
        @pl.when(variant == v)
        def _():
            cmp_branch(w)

    cur = (t0 + (_iota((1, NSA_GROUPS * tq), 1) & (tq - 1))) >> 6
    sel = _topk_select(imp_sc[...], cur)
    selneg = jnp.where(sel > 0.0, 0.0, NEG).astype(BF16)
    for hh in range(nh):
        g = hh // NSA_HPG
        qa_sc[hh * tq:(hh + 1) * tq, LANES:LANES + nb] = selneg[g * tq:(g + 1) * tq]

    def load_v(ref, start, width):
        v128 = ref[pl.ds(start, width), LANES:2 * LANES]
        lane_half = _iota(v128.shape, 1) >> 6
        return [jnp.where(lane_half == g, v128, jnp.ones_like(v128)) for g in range(NSA_GROUPS)]

    def online(s3, vaug, first):
        width = s3.shape[-1]
        tmax = jnp.max(s3, axis=-1, keepdims=True)
        if first:
            m_new = jnp.broadcast_to(tmax, (nh, tq, LANES))
        else:
            m_old = m_sc[...]
            m_new = jnp.maximum(m_old, tmax)
            alpha = jnp.exp2(m_old - m_new)
        pexp = jnp.concatenate([jnp.exp2((s3[:, :, c * LANES:(c + 1) * LANES] - m_new).astype(BF16))
                                for c in range(width // LANES)], axis=-1)
        for hh in range(nh):
            pv = _dot(pexp[hh], vaug[hh // NSA_HPG])
            acc_sc[hh] = pv if first else alpha[hh] * acc_sc[hh] + pv
        m_sc[...] = m_new

    def finish(branch):
        for hh in range(nh):
            acc = acc_sc[hh]
            den = pltpu.roll(acc, HEAD_DIM, 1)
            out_sc[hh] = out_sc[hh] + gate(branch, hh) * (acc / den)

    def slc_scores(start, width):
        kaug = jnp.concatenate([kvs_ref[pl.ds(start, width), 0:LANES], et_ref[pl.ds(start, width), :]], axis=1)
        return _dot_nt(qa_sc[...], kaug).reshape(nh, tq, width)

    def slc_first(start, width):
        s3 = slc_scores(start, width)
        s3 = jnp.where(start + _iota((1, tq, width), 2) <= trow, s3, NEG)
        online(s3, load_v(kvs_ref, start, width), True)

    n_full = i // (tk // tq)
    tail = i % (tk // tq)

    @pl.when(tail == tk // tq - 1)
    def _():
        slc_first(pl.multiple_of(n_full * tk, tk), tk)

    @pl.when(tail != tk // tq - 1)
    def _():
        slc_first(pl.multiple_of(t0, tq), tq)

    def slc_step(j, carry):
        start = pl.multiple_of(j * tk, tk)
        online(slc_scores(start, tk), load_v(kvs_ref, start, tk), False)
        return carry

    lax.fori_loop(0, n_full, slc_step, 0)
    finish(1)

    d0 = pl.multiple_of(t0, tq)
    s3 = _dot_nt(qa_sc[:, 0:LANES], kvw_ref[pl.ds(d0, tq), 0:LANES]).reshape(nh, tq, tq)
    s3 = jnp.where(_iota((1, tq, tq), 2) <= _iota((1, tq, tq), 1), s3, NEG)
    online(s3, load_v(kvw_ref, d0, tq), True)

    @pl.when(i >= 1)
    def _():
        start = pl.multiple_of(jnp.maximum(t0 - WINDOW, 0), tq)
        s3 = _dot_nt(qa_sc[:, 0:LANES], kvw_ref[pl.ds(start, WINDOW), 0:LANES]).reshape(nh, tq, WINDOW)
        kpos = start + _iota((1, tq, WINDOW), 2)
        s3 = jnp.where((kpos < t0) & (trow - kpos < WINDOW), s3, NEG)
        online(s3, load_v(kvw_ref, start, WINDOW), False)

    finish(2)

    for pair in range(nh // 2):
        parts = []
        for hh in (2 * pair, 2 * pair + 1):
            x = out_sc[hh]
            if hh % 2 != hh // NSA_HPG:
                x = pltpu.roll(x, HEAD_DIM, 1)
            parts.append(x)
        o_ref[:, pair * LANES:(pair + 1) * LANES] = jnp.where(half == 0, parts[0], parts[1]).astype(o_ref.dtype)


def _nsa_prompt(q, gates, ckv, kvs_b, kvw_b, ov, et, tq, tk):
    b, l, _ = q.shape
    nb = l // SLC_BLOCK
    assert l % tk == 0 and tk == 2 * tq and WINDOW % tq == 0 and l >= WINDOW
    blk = lambda w: pl.BlockSpec((None, tq, w), lambda bi, i: (bi, i, 0))
    seq = lambda a: pl.BlockSpec((None,) + a.shape[1:], lambda bi, i: (bi, 0, 0))
    full = lambda a: pl.BlockSpec(a.shape, lambda bi, i: (0, 0))
    hs = (NSA_HEADS, tq, LANES)
    return pl.pallas_call(
        functools.partial(_nsa_prompt_body, tq=tq, tk=tk, n_cmp=l // CMP_STRIDE - 1),
        grid=(b, l // tq),
        in_specs=[blk(NQ), blk(LANES), seq(ckv), seq(kvs_b), seq(kvw_b), full(ov), full(et)],
        out_specs=blk(NQ),
        out_shape=jax.ShapeDtypeStruct((b, l, NQ), BF16),
        scratch_shapes=[pltpu.VMEM((NSA_HEADS * tq, LANES + nb), BF16), pltpu.VMEM(hs, F32), pltpu.VMEM(hs, F32),
                        pltpu.VMEM(hs, F32), pltpu.VMEM((NSA_GROUPS * tq, nb), F32)],
        compiler_params=_params("parallel", "arbitrary"),
        name="nsa_prompt",
    )(q, gates, ckv, kvs_b, kvw_b, ov, et)


def _nsa_sample_body(pt_ref, cache_ref, q_ref, gate_ref, ckv_ref, win_ref, ns_ref, nw_ref, ov_ref, e_ref, o_ref,
                     page_sc, sem, *, n_steps, n_pages, nseq, s_new, n_cmp, p_len):
    i = pl.program_id(0)
    npg = nseq * n_pages

    def fetch(step, slot):
        for s in range(nseq):
            for k in range(n_pages):
                pltpu.make_async_copy(cache_ref.at[pt_ref[step * nseq + s, k]], page_sc.at[slot, s * n_pages + k],
                                      sem.at[slot]).start()

    def wait(slot):
        for j in range(npg):
            pltpu.make_async_copy(cache_ref.at[0], page_sc.at[slot, j], sem.at[slot]).wait()

    @pl.when(i == 0)
    def _():
        fetch(0, 0)

    for slot in range(2):
        @pl.when(i % 2 == slot)
        def _():
            @pl.when(i + 1 < n_steps)
            def _():
                fetch(i + 1, 1 - slot)

            wait(slot)
            _nsa_sample_attend([page_sc.at[slot, j] for j in range(npg)], q_ref, gate_ref, ckv_ref, win_ref, ns_ref,
                               nw_ref, ov_ref, e_ref, o_ref, n_pages=n_pages, nseq=nseq, s_new=s_new, n_cmp=n_cmp,
                               p_len=p_len)


def _nsa_sample_attend(page_refs, q_ref, gate_ref, ckv_ref, win_ref, ns_ref, nw_ref, ov_ref, e_ref, o_ref,
                       *, n_pages, nseq, s_new, n_cmp, p_len):
    rows1 = NSA_HEADS * s_new
    rows = nseq * rows1
    nc = ckv_ref.shape[1]
    nb = ov_ref.shape[1]
    w_buf = win_ref.shape[2]
    seqs = range(nseq)
    of = lambda x, s: x[s * rows1:(s + 1) * rows1]
    cat = lambda xs: xs[0] if len(xs) == 1 else jnp.concatenate(xs, axis=0)
    q2 = [jnp.concatenate(_head_queries(q_ref[s], s_new), axis=0) for s in seqs]
    ridx = _iota((rows, 1), 0)
    trow = ridx & (s_new - 1)
    grow = _div_pow2(ridx, s_new * NSA_HPG) & (NSA_GROUPS - 1)
    lane_half = _iota((rows, LANES), 1) >> 6

    def gate_rows(branch):
        cols = [gate_ref[s][:, branch * NSA_HEADS + hh:branch * NSA_HEADS + hh + 1]
                for s in seqs for hh in range(NSA_HEADS)]
        return jnp.concatenate(cols, axis=0)

    def pick(acc, den):
        return jnp.where(lane_half == grow, acc, 0.0) / den

    def attend(s_old, s_nw, vt_old, new_rows):
        s_old = cat(s_old)
        s_nw = jnp.where(_iota((rows, s_new), 1) <= trow, cat(s_nw), NEG)
        m = jnp.maximum(jnp.max(s_old, axis=-1, keepdims=True), jnp.max(s_nw, axis=-1, keepdims=True))
        e1 = jnp.exp2(s_old - m)
        e2 = jnp.exp2(s_nw - m)
        den = jnp.sum(e1, axis=-1, keepdims=True) + jnp.sum(e2, axis=-1, keepdims=True)
        e1 = e1.astype(BF16)
        e2 = e2.astype(BF16)
        acc = cat([_dot_nt(vt_old[s], of(e1, s)).T + _dot(of(e2, s), new_rows[s][:, LANES:2 * LANES]) for s in seqs])
        return pick(acc, den)

    ncol = _iota((rows, nc), 1)
    cvalid = (ncol * CMP_STRIDE + CMP_LEN <= p_len + trow + 1) & (ncol < n_cmp)
    p_c = _masked_softmax2(cat([_dot_nt(q2[s], ckv_ref[s, :, 0:LANES]) for s in seqs]), cvalid)
    o_c = cat([_dot(of(p_c, s).astype(BF16), ckv_ref[s, :, LANES:2 * LANES]) for s in seqs])
    out = gate_rows(0) * jnp.where(lane_half == grow, o_c, 0.0)
    psum = []
    for s in seqs:
        for g in range(NSA_GROUPS):
            acc = jnp.zeros((s_new, nc), F32)
            for h in range(NSA_HPG):
                r0 = s * rows1 + (g * NSA_HPG + h) * s_new
                acc = acc + p_c[r0:r0 + s_new]
            psum.append(acc)
    n_imp = nseq * NSA_GROUPS * s_new
    psum.append(jnp.zeros((LANES - n_imp, nc), F32))
    imp = _dot01(jnp.concatenate(psum, axis=0), ov_ref[...])
    cur = jnp.full((1, LANES), p_len // SLC_BLOCK, jnp.int32)
    sel = _topk_select(imp, cur)[0:n_imp]
    selneg = jnp.where(sel > 0.0, 0.0, NEG).astype(BF16)
    selneg_rows = [jnp.concatenate([selneg[(s * NSA_GROUPS + g) * s_new:(s * NSA_GROUPS + g + 1) * s_new]
                                    for g in range(NSA_GROUPS) for _ in range(NSA_HPG)], axis=0)
                   for s in seqs]

    e_hot = e_ref[...]
    s_past, s_nw, vts, news = [], [], [], []
    for s in seqs:
        kvt = jnp.concatenate([r[...] for r in page_refs[s * n_pages:(s + 1) * n_pages]],
                              axis=1).astype(BF16)
        kaug = jnp.concatenate([kvt[0:LANES], e_hot], axis=0)
        s_past.append(_dot(jnp.concatenate([q2[s], selneg_rows[s]], axis=1), kaug))
        ns = ns_ref[s].astype(BF16)
        s_nw.append(_dot_nt(q2[s], ns[:, 0:LANES]))
        vts.append(kvt[LANES:2 * LANES])
        news.append(ns)
    out = out + gate_rows(1) * attend(s_past, s_nw, vts, news)

    scol = _iota((rows1, w_buf), 1)
    tr1 = _iota((rows1, 1), 0) & (s_new - 1)
    wmask = (w_buf + tr1 - scol < WINDOW) & (p_len - w_buf + scol >= 0)
    s_b, s_nw, vts, news = [], [], [], []
    for s in seqs:
        wt = win_ref[s].astype(BF16)
        nw = nw_ref[s].astype(BF16)
        s_b.append(jnp.where(wmask, _dot(q2[s], wt[0:LANES]), NEG))
        s_nw.append(_dot_nt(q2[s], nw[:, 0:LANES]))
        vts.append(wt[LANES:2 * LANES])
        news.append(nw)
    out = out + gate_rows(2) * attend(s_b, s_nw, vts, news)

    lh = _iota((s_new, LANES), 1) >> 6
    for s in seqs:
        for pair in range(NSA_HEADS // 2):
            parts = []
            for hh in (2 * pair, 2 * pair + 1):
                x = out[s * rows1 + hh * s_new:s * rows1 + (hh + 1) * s_new]
                if hh % 2 != hh // NSA_HPG:
                    x = jnp.concatenate([x[:, HEAD_DIM:], x[:, :HEAD_DIM]], axis=1)
                parts.append(x)
            o_ref[s, :, pair * LANES:(pair + 1) * LANES] = jnp.where(lh == 0, parts[0], parts[1]).astype(o_ref.dtype)


def _nsa_sample(q, gates, ckv, cache_t, page_table, win_t, new_s, new_w, ov, e_all):
    bd, s_new, _ = q.shape
    n_pages = page_table.shape[1]
    p_len = n_pages * PAGE_SIZE
    assert s_new & (s_new - 1) == 0
    nseq = _tile_rows(bd, SAMPLE_SEQS_PER_STEP)
    full = lambda a: pl.BlockSpec(a.shape, lambda i, pt: (0, 0))
    seq = lambda a: pl.BlockSpec((nseq,) + a.shape[1:], lambda i, pt: (i, 0, 0))
    return pl.pallas_call(
        functools.partial(_nsa_sample_body, n_steps=bd // nseq, n_pages=n_pages, nseq=nseq, s_new=s_new,
                          n_cmp=p_len // CMP_STRIDE - 1, p_len=p_len),
        grid_spec=pltpu.PrefetchScalarGridSpec(
            num_scalar_prefetch=1,
            grid=(bd // nseq,),
            in_specs=[pl.BlockSpec(memory_space=pl.ANY), seq(q), seq(gates), seq(ckv), seq(win_t), seq(new_s),
                      seq(new_w), full(ov), full(e_all)],
            out_specs=pl.BlockSpec((nseq, s_new, NQ), lambda i, pt: (i, 0, 0)),
            scratch_shapes=[pltpu.VMEM((2, nseq * n_pages, KV_W, PAGE_SIZE), F32), pltpu.SemaphoreType.DMA((2,))],
        ),
        out_shape=jax.ShapeDtypeStruct((bd, s_new, NQ), BF16),
        compiler_params=_params("arbitrary"),
        name="nsa_sample",
    )(page_table, cache_t, q, gates, ckv, win_t, new_s, new_w, ov, e_all)


def _gla_body(q_ref, k_ref, v_ref, lg_ref, r_ref, gn_ref, s0_ref, o_ref, sfin_ref, s_sc, *, ns, tb, chunk, sub):
    j = pl.program_id(1)

    @pl.when(j == 0)
    def _():
        s_sc[...] = s0_ref[...]

    nchunk = tb // chunk
    nsub = chunk // sub
    ri = _iota((tb, tb), 0)
    ci = _iota((tb, tb), 1)
    tril = jnp.where((_div_pow2(ri, chunk) == _div_pow2(ci, chunk)) & (ci <= ri), 1.0, 0.0).astype(BF16)
    head_of_lane = _div_pow2(_iota((1, GQK), 1), GLA_DK)
    rows = _iota((chunk, GQK), 0)
    ones_dv = jnp.ones((chunk, GLA_DV), BF16)
    vhead = _div_pow2(_iota((chunk, GV), 1), GLA_DV)
    gn = gn_ref[...]

    units = [(si, c) for si in range(ns) for c in range(nchunk)]
    b_seq = [_dot01_l(tril, lg_ref[si]) for si in range(ns)]
    bt_seq = [b.T for b in b_seq] if tb % LANES == 0 else None
    pre = {}
    for si, c in units:
        c0 = c * chunk
        bc = b_seq[si][c0:c0 + chunk]
        qc = q_ref[si, c0:c0 + chunk, :]
        kc = k_ref[si, c0:c0 + chunk, :]
        b_last = bc[chunk - 1:chunk]
        qdb = qc * jnp.exp(bc)
        kdl = kc * jnp.exp(b_last - bc)
        if bt_seq is not None:
            decay = jnp.exp(jnp.broadcast_to(bt_seq[si][:, c0 + chunk - 1:c0 + chunk], (GQK, GLA_DV)))
        else:
            decay = jnp.exp(_dot01_tn(jnp.where(rows == chunk - 1, bc, 0.0), ones_dv))
        kds = []
        for sb in range(nsub):
            rr = sb * sub
            ref_row = bc[rr - 1:rr] if sb else jnp.zeros((1, GQK), F32)
            vis = rows < rr + sub
            kds.append((ref_row, jnp.where(vis, kc * jnp.exp(jnp.where(vis, ref_row - bc, 0.0)), 0.0).astype(BF16)))
        vc = v_ref[si, c0:c0 + chunk, :].astype(BF16)
        vbd = jnp.concatenate([jnp.where(vhead == h, vc, jnp.zeros_like(vc)) for h in range(GLA_HEADS)], axis=0)
        o_sub = []
        for sb in range(nsub):
            rr = sb * sub
            ref_row, kd = kds[sb]
            qd = qc[rr:rr + sub] * jnp.exp(bc[rr:rr + sub] - ref_row)
            qst = jnp.concatenate([jnp.where(head_of_lane == h, qd, 0.0) for h in range(GLA_HEADS)], axis=0)
            a = _dot_nt(qst.astype(BF16), kd)
            qrow = _iota((GLA_HEADS * sub, chunk), 0) & (sub - 1)
            a = jnp.where(_iota((GLA_HEADS * sub, chunk), 1) <= rr + qrow, a, 0.0)
            a_cat = jnp.concatenate([a[h * sub:(h + 1) * sub] for h in range(GLA_HEADS)],
                                    axis=1).astype(BF16)
            o_sub.append(a_cat)
        o_intra = _dot(o_sub[0] if nsub == 1 else jnp.concatenate(o_sub, axis=0), vbd)
        qdb_st = jnp.concatenate([jnp.where(head_of_lane == h, qdb, 0.0) for h in range(GLA_HEADS)],
                                 axis=0).astype(BF16)
        kv = _dot_tn(kdl.astype(BF16), vc)
        upd = jnp.concatenate([kv[h * GLA_DK:(h + 1) * GLA_DK, h * GLA_DV:(h + 1) * GLA_DV]
                               for h in range(GLA_HEADS)], axis=0)
        pre[si, c] = (decay, upd, o_intra, qdb_st)

    for si in range(ns):
        state = s_sc[si]
        for c in range(nchunk):
            c0 = c * chunk
            decay, upd, o_intra, qdb_st = pre[si, c]
            o_inter = _dot(qdb_st, state.astype(BF16))
            for h in range(GLA_HEADS):
                hs = slice(h * GLA_DV, (h + 1) * GLA_DV)
                o = o_inter[h * chunk:(h + 1) * chunk] + o_intra[:, hs]
                o_ref[si, c0:c0 + chunk, hs] = (_rms(o, gn) * r_ref[si, c0:c0 + chunk, hs].astype(F32)).astype(o_ref.dtype)
            state = decay * state + upd
        s_sc[si] = state

    @pl.when(j == pl.num_programs(1) - 1)
    def _():
        sfin_ref[...] = s_sc[...]


def _gla(q, k, v, lg, r, gn, s0, ns, tb, chunk):
    s, t, _ = q.shape
    sub = min(GLA_SUB, chunk)
    row = lambda w: pl.BlockSpec((ns, tb, w), lambda gi, j: (gi, j, 0))
    st = pl.BlockSpec((ns, GQK, GLA_DV), lambda gi, j: (gi, 0, 0))
    return pl.pallas_call(
        functools.partial(_gla_body, ns=ns, tb=tb, chunk=chunk, sub=sub),
        grid=(s // ns, t // tb),
        in_specs=[row(GQK), row(GQK), row(GV), row(GQK), row(GV), pl.BlockSpec(gn.shape, lambda gi, j: (0, 0)), st],
        out_specs=[row(GV), st],
        out_shape=[jax.ShapeDtypeStruct((s, t, GV), BF16), jax.ShapeDtypeStruct((s, GQK, GLA_DV), F32)],
        scratch_shapes=[pltpu.VMEM((ns, GQK, GLA_DV), F32)],
        compiler_params=_params("parallel", "arbitrary"),
        name="gla",
    )(q, k, v, lg, r, gn, s0)


def _mix_body(x_ref, sc_ref, sh_ref, ga_ref, gpre_ref, gpost_ref, on_ref, og_ref, wm_ref, wn_ref, wgo_ref, wo_ref,
              o_ref):
    x = x_ref[...]
    d = x.shape[1]
    h = _norm_mod(x, gpre_ref[...], sc_ref[...], sh_ref[...]).astype(BF16)
    m = _sigmoid(_dot(h, wm_ref[...]))
    y_a = _dot(on_ref[...], wn_ref[...])
    y_b = _dot(og_ref[...], wgo_ref[...])
    mixin = (m[:, 0:d] * y_a + m[:, d:2 * d] * y_b).astype(BF16)
    mix = _dot(mixin, wo_ref[...])
    o_ref[...] = x + ga_ref[...] * _rms(mix, gpost_ref[...])


def _mix(x, sc, sh, ga, gpre, gpost, o_nsa, o_gla, wm, wn, wgo, wo, tm, rows_per_mod):
    n, d = x.shape
    row = lambda w: pl.BlockSpec((tm, w), lambda i: (i, 0))
    full = lambda a: pl.BlockSpec(a.shape, lambda i: (0, 0))
    ms = lambda a: _mod_spec(a, tm, rows_per_mod)
    return pl.pallas_call(
        _mix_body,
        grid=(n // tm,),
        in_specs=[row(d), ms(sc), ms(sh), ms(ga), full(gpre), full(gpost), row(NQ), row(GV),
                  full(wm), full(wn), full(wgo), full(wo)],
        out_specs=row(d),
        out_shape=jax.ShapeDtypeStruct((n, d), F32),
        compiler_params=_params("parallel"),
        name="mix",
    )(x, sc, sh, ga, gpre, gpost, o_nsa, o_gla, wm, wn, wgo, wo)


def _ffn_body(x_ref, sc_ref, sh_ref, ga_ref, gpre_ref, gpost_ref, wg_ref, wu_ref, wd_ref, o_ref, h_sc, acc_sc):
    j = pl.program_id(1)

    @pl.when(j == 0)
    def _():
        h_sc[...] = _norm_mod(x_ref[...], gpre_ref[...], sc_ref[...], sh_ref[...]).astype(BF16)
        acc_sc[...] = jnp.zeros(acc_sc.shape, F32)

    h = h_sc[...]
    gt = _dot(h, wg_ref[...])
    up = _dot(h, wu_ref[...])
    a = (gt * _sigmoid(gt) * up).astype(BF16)
    acc_sc[...] += _dot(a, wd_ref[...])

    @pl.when(j == pl.num_programs(1) - 1)
    def _():
        o_ref[...] = x_ref[...] + ga_ref[...] * _rms(acc_sc[...], gpost_ref[...])


def _ffn(x, sc, sh, ga, gpre, gpost, wg, wu, wd, tm, rows_per_mod):
    n, d = x.shape
    dff = wg.shape[1]
    tf = dff // 2 if (dff // 2) % LANES == 0 else dff
    row = pl.BlockSpec((tm, d), lambda i, j: (i, 0))
    full = lambda a: pl.BlockSpec(a.shape, lambda i, j: (0, 0))
    ms = lambda a: pl.BlockSpec((None, a.shape[1], a.shape[2]), lambda i, j: ((i * tm) // rows_per_mod, 0, 0))
    return pl.pallas_call(
        _ffn_body,
        grid=(n // tm, dff // tf),
        in_specs=[row, ms(sc), ms(sh), ms(ga), full(gpre), full(gpost),
                  pl.BlockSpec((d, tf), lambda i, j: (0, j)), pl.BlockSpec((d, tf), lambda i, j: (0, j)),
                  pl.BlockSpec((tf, d), lambda i, j: (j, 0))],
        out_specs=row,
        out_shape=jax.ShapeDtypeStruct((n, d), F32),
        scratch_shapes=[pltpu.VMEM((tm, d), BF16), pltpu.VMEM((tm, d), F32)],
        compiler_params=_params("parallel", "arbitrary"),
        name="ffn",
    )(x, sc, sh, ga, gpre, gpost, wg, wu, wd)


def _split_w_in(w_in, d):
    sizes = (NQ, KV_W, KV_W, KV_W, 3 * NSA_HEADS, GQK, GQK, GV, 16, GV, 2 * d)
    out, o = [], 0
    for s in sizes:
        out.append(w_in[:, o:o + s])
        o += s
    return out


def _pad_cols(w, n):
    return jnp.pad(w, ((0, 0), (0, n - w.shape[1])))


def _compress_weights(w1k, b1k, w2k, pek, w1v, b1v, w2v, pev):
    hid = w2k.shape[0]
    npair = CMP_STRIDE // 2
    eye = np.eye(NSA_GROUPS, dtype=np.float32)

    def group_diag(w):
        out = jnp.einsum('gh,...ab->...gahb', eye, w)
        return out.reshape(w.shape[:-2] + (NSA_GROUPS * w.shape[-2], NSA_GROUPS * w.shape[-1]))

    def expand_w1(half):
        per_kv = [group_diag(w1.reshape(CMP_LEN, HEAD_DIM, hid)[half * CMP_STRIDE:(half + 1) * CMP_STRIDE])
                  for w1 in (w1k, w1v)]
        return jnp.stack(per_kv).reshape(2, npair, 2 * NSA_GROUPS * HEAD_DIM, NSA_GROUPS * hid)

    w1 = jnp.concatenate([expand_w1(0), expand_w1(1)], axis=-1).astype(BF16)

    def expand_pe(half):
        per_kv = [jnp.tile(pe[half * CMP_STRIDE:(half + 1) * CMP_STRIDE], (1, NSA_GROUPS)) for pe in (pek, pev)]
        return jnp.stack(per_kv).reshape(2, npair, 2 * NSA_GROUPS * HEAD_DIM)

    pe = jnp.pad(jnp.stack([expand_pe(0), expand_pe(1)], axis=2),
                 ((0, 0), (0, 0), (0, PE_ROWS - 2), (0, 0))).astype(BF16)

    b1 = jnp.concatenate([jnp.tile(b1k, NSA_GROUPS), jnp.tile(b1v, NSA_GROUPS)]).reshape(1, -1)
    w2d = [group_diag(w2k), group_diag(w2v)]
    zero = jnp.zeros_like(w2d[0])
    w2 = jnp.concatenate([jnp.concatenate([w2d[0], zero], axis=1),
                          jnp.concatenate([zero, w2d[1]], axis=1)], axis=0).astype(BF16)
    return pe, w1, b1, w2


def _overlap(n_cmp_pad, n_cmp, nb):
    n = np.arange(n_cmp_pad)[:, None]
    j = np.arange(nb)[None, :] * SLC_BLOCK
    ok = (n * CMP_STRIDE < j + SLC_BLOCK) & (n * CMP_STRIDE + CMP_LEN > j) & (n < n_cmp)
    return ok.astype(BF16)


def _block_onehot(nb, nkeys):
    return (np.arange(nb)[:, None] == (np.arange(nkeys)[None, :] // SLC_BLOCK)).astype(BF16)


def _tile_rows(n, cap):
    t = cap
    while n % t:
        t //= 2
    return t


def _feature_major(a):
    nd = a.ndim
    a = jnp.moveaxis(a, nd - 4, nd - 1)
    return a.reshape(a.shape[:nd - 4] + (KV_W, a.shape[-1]))


def _row_major(a_t):
    b, _, r = a_t.shape
    return jnp.moveaxis(a_t.reshape(b, 2, NSA_GROUPS, HEAD_DIM, r), 4, 1)


def kernel(x_prompt, x_sample, cache_cmp_kv, cache_slc_kv, state_win_kv, state_gla, page_table, c_prompt, c_sample, w_ada, b_ada, g_pre_mix, g_post_mix, w_in, cmp_w1_k, cmp_b1_k, cmp_w2_k, cmp_pe_k, cmp_w1_v, cmp_b1_v, cmp_w2_v, cmp_pe_v, w_gla_a2, b_gla_a, gla_norm_g, w_nsa_o, w_gla_o, w_out, g_pre_ffn, g_post_ffn, w_ff_gate, w_ff_up, w_ff_down):
    depth = w_ada.shape[0]
    bp, l, d = x_prompt.shape
    bd, s_new, _ = x_sample.shape
    n_pages = page_table.shape[1]
    p_len = n_pages * PAGE_SIZE
    w_buf = state_win_kv.shape[2]
    keep = min(WINDOW, l)
    np_, ns_ = bp * l, bd * s_new
    tmp = _tile_rows(np_ // bp, 512)
    tms = _tile_rows(ns_, 512)
    tq = _tile_rows(l, 256)
    tk = 2 * tq

    yp, ys = x_prompt.reshape(np_, d), x_sample.reshape(ns_, d)
    col = [[] for _ in range(8)]
    for li in range(depth):
        r_all = bp + bd
        r_pad = -(-r_all // 8) * 8
        c_all = jnp.pad(jnp.concatenate([c_prompt, c_sample], axis=0), ((0, r_pad - r_all), (0, 0)))
        mod = _ada(c_all, w_ada[li], b_ada[li])
        mods_p = [m.reshape(bp, 1, d) for m in jnp.split(mod[:bp], 6, axis=-1)]
        mods_s = [jnp.repeat(m, s_new, axis=0).reshape(ns_ // tms, tms, d) for m in jnp.split(mod[bp:r_all], 6, axis=-1)]

        (w_q, w_kc, w_ks, w_kw, w_gn, w_qg, w_kg, w_vg, w_ag, w_rg, w_mg) = _split_w_in(w_in[li], d)
        wq = w_q.astype(BF16)
        wkv = jnp.concatenate([w_kc, w_ks, w_kw], axis=1).astype(BF16)
        wgn = _pad_cols(w_gn, LANES).astype(BF16)
        wqkv = jnp.concatenate([w_qg, w_kg, w_vg], axis=1).astype(BF16)
        wa = _pad_cols(w_ag, LANES).astype(BF16)
        wr = w_rg.astype(BF16)
        a2 = jnp.pad(w_gla_a2[li], ((0, LANES - w_gla_a2.shape[1]), (0, 0)))
        ba = b_gla_a[li].reshape(1, GQK)
        cw = _compress_weights(cmp_w1_k[li], cmp_b1_k[li], cmp_w2_k[li], cmp_pe_k[li],
                               cmp_w1_v[li], cmp_b1_v[li], cmp_w2_v[li], cmp_pe_v[li])
        gpre = g_pre_mix[li].reshape(1, d)
        gpost = g_post_mix[li].reshape(1, d)
        gpre2 = g_pre_ffn[li].reshape(1, d)
        gpost2 = g_post_ffn[li].reshape(1, d)
        gn = gla_norm_g[li].reshape(1, GLA_DV)
        wm = w_mg.astype(BF16)
        wn = w_nsa_o[li].astype(BF16)
        wgo = w_gla_o[li].astype(BF16)
        wo = w_out[li].astype(BF16)
        wfg = w_ff_gate[li].astype(BF16)
        wfu = w_ff_up[li].astype(BF16)
        wfd = w_ff_down[li].astype(BF16)

        sh1, sc1, ga1, sh2, sc2, ga2 = mods_p
        (q, kvc, _, _, kvs_b, kvw_b, gates, kvc_t, kvs_t, kvw_t) = _pre_nsa(yp, sc1, sh1, gpre, wq, wkv, wgn, tmp, l,
                                                                            feature_major=(bp, l))
        qg, kg, vg, lg, rg = _pre_gla(yp, sc1, sh1, gpre, wqkv, wa, wr, a2, ba, tmp, l)
        ckv = _cmp_prompt(kvc.reshape(bp, l, KV_W), cw)
        nch = l // CMP_STRIDE
        nb = l // SLC_BLOCK
        o_nsa = _nsa_prompt(q.reshape(bp, l, NQ), gates.reshape(bp, l, LANES), ckv, kvs_b.reshape(bp, l, KV_W),
                            kvw_b.reshape(bp, l, KV_W), _overlap(nch, nch - 1, nb), _block_onehot(nb, l).T, tq, tk)
        chunk = GLA_CHUNK if l % GLA_CHUNK == 0 else l
        tb = 2 * LANES if (l % (2 * LANES) == 0 and LANES % chunk == 0) else chunk
        o_gla, s_fin_p = _gla(qg.reshape(bp, l, GQK), kg.reshape(bp, l, GQK), vg.reshape(bp, l, GV),
                              lg.reshape(bp, l, GQK), rg.reshape(bp, l, GV), gn,
                              jnp.zeros((bp, GQK, GLA_DV), F32), bp, tb, chunk)
        x1 = _mix(yp, sc1, sh1, ga1, gpre, gpost, o_nsa.reshape(np_, NQ), o_gla.reshape(np_, GV), wm, wn, wgo, wo, tmp, l)
        yp = _ffn(x1, sc2, sh2, ga2, gpre2, gpost2, wfg, wfu, wfd, tmp, l)
        col[0].append(_row_major(kvc_t))
        col[2].append(_row_major(kvs_t))
        col[4].append(_row_major(kvw_t[:, :, l - keep:]))
        col[6].append(s_fin_p.reshape(bp, GLA_HEADS, GLA_DK, GLA_DV).astype(state_gla.dtype))

        sh1, sc1, ga1, sh2, sc2, ga2 = mods_s
        q, kvc, kvs, kvw, _, _, gates = _pre_nsa(ys, sc1, sh1, gpre, wq, wkv, wgn, tms, tms)
        qg, kg, vg, lg, rg = _pre_gla(ys, sc1, sh1, gpre, wqkv, wa, wr, a2, ba, tms, tms)
        ckv = _cmp_sample(_feature_major(cache_cmp_kv[li]), page_table, cw)
        nch = p_len // CMP_STRIDE
        nb = p_len // SLC_BLOCK
        win_t = _feature_major(state_win_kv[li])
        kvw3 = kvw.reshape(bd, s_new, KV_W)
        o_nsa = _nsa_sample(q.reshape(bd, s_new, NQ), gates.reshape(bd, s_new, LANES), ckv,
                            _feature_major(cache_slc_kv[li]), page_table, win_t, kvs.reshape(bd, s_new, KV_W), kvw3,
                            _overlap(nch, nch - 1, nb), _block_onehot(nb, p_len))
        chunk = GLA_CHUNK if s_new % GLA_CHUNK == 0 else s_new
        o_gla, s_fin_s = _gla(qg.reshape(bd, s_new, GQK), kg.reshape(bd, s_new, GQK), vg.reshape(bd, s_new, GV),
                              lg.reshape(bd, s_new, GQK), rg.reshape(bd, s_new, GV), gn,
                              state_gla[li].astype(F32).reshape(bd, GQK, GLA_DV), _tile_rows(bd, 16), chunk, chunk)
        x1 = _mix(ys, sc1, sh1, ga1, gpre, gpost, o_nsa.reshape(ns_, NQ), o_gla.reshape(ns_, GV), wm, wn, wgo, wo, tms, tms)
        ys = _ffn(x1, sc2, sh2, ga2, gpre2, gpost2, wfg, wfu, wfd, tms, tms)
        win_new_t = jnp.concatenate([win_t, kvw3.transpose(0, 2, 1)], axis=2)[:, :, s_new:]
        win_new = jnp.moveaxis(win_new_t.reshape(bd, 2, NSA_GROUPS, HEAD_DIM, w_buf), 4, 1)
        col[1].append(kvc.reshape(bd, s_new, 2, NSA_GROUPS, HEAD_DIM))
        col[3].append(kvs.reshape(bd, s_new, 2, NSA_GROUPS, HEAD_DIM))
        col[5].append(win_new)
        col[7].append(s_fin_s.reshape(bd, GLA_HEADS, GLA_DK, GLA_DV).astype(state_gla.dtype))

    stacked = [c[0][None] if len(c) == 1 else jnp.stack(c) for c in col]
    return (yp.reshape(bp, l, d), ys.reshape(bd, s_new, d), *stacked)
```

```python
import functools
import math

import numpy as np
import jax
import jax.numpy as jnp
from jax import lax
from jax.experimental import pallas as pl
from jax.experimental.pallas import tpu as pltpu

F32 = jnp.float32
BF16 = jnp.bfloat16

NSA_HEADS = 8
NSA_GROUPS = 2
NSA_HPG = NSA_HEADS // NSA_GROUPS
HEAD_DIM = 64
CMP_STRIDE = 16
CMP_LEN = 32
SLC_BLOCK = 64
N_SEL = 16
WINDOW = 512
GLA_HEADS = 4
GLA_DK = 64
GLA_DV = 128
GLA_TAU = 16.0
GLA_CHUNK = 64
GLA_SUB = 16
PE_ROWS = 16
SAMPLE_SEQS_PER_STEP = 2
EPS = 1e-6
PAGE_SIZE = 128
NEG = -1e30
LOG2E = math.log2(math.e)
LANES = 128
VMEM_LIMIT = 56 * 1024 * 1024

KV_W = 2 * NSA_GROUPS * HEAD_DIM
NQ = NSA_HEADS * HEAD_DIM
GQK = GLA_HEADS * GLA_DK
GV = GLA_HEADS * GLA_DV


def _dot(a, b):
    return jnp.dot(a, b, preferred_element_type=F32)


def _dot_nt(a, b):
    return lax.dot_general(a, b, (((1,), (1,)), ((), ())), preferred_element_type=F32)


def _dot_f32(a, b):
    return jnp.dot(a, b, preferred_element_type=F32, precision=lax.Precision.HIGHEST)


def _dot01(x, m01):
    hi = x.astype(BF16)
    r1 = x - hi.astype(F32)
    mid = r1.astype(BF16)
    lo = (r1 - mid.astype(F32)).astype(BF16)
    return _dot(hi, m01) + _dot(mid, m01) + _dot(lo, m01)


def _dot01_l(m01, x):
    hi = x.astype(BF16)
    r1 = x - hi.astype(F32)
    mid = r1.astype(BF16)
    lo = (r1 - mid.astype(F32)).astype(BF16)
    return _dot(m01, hi) + _dot(m01, mid) + _dot(m01, lo)


def _dot_tn(a, b):
    return lax.dot_general(a, b, (((0,), (0,)), ((), ())), preferred_element_type=F32)


def _dot01_tn(x, m01):
    hi = x.astype(BF16)
    r1 = x - hi.astype(F32)
    mid = r1.astype(BF16)
    lo = (r1 - mid.astype(F32)).astype(BF16)
    return _dot_tn(hi, m01) + _dot_tn(mid, m01) + _dot_tn(lo, m01)


def _sigmoid(x):
    return 1.0 / (1.0 + jnp.exp(-x))


def _params(*sem):
    return pltpu.CompilerParams(dimension_semantics=sem, vmem_limit_bytes=VMEM_LIMIT)


def _iota(shape, dim):
    return lax.broadcasted_iota(jnp.int32, shape, dim)


def _div_pow2(x, n):
    assert n & (n - 1) == 0, n
    return x >> (n.bit_length() - 1)


def _ada_body(c_ref, w_ref, b_ref, o_ref):
    o_ref[...] = _dot_f32(c_ref[...], w_ref[...]) + b_ref[...]


def _ada(c_all, w_ada, b_ada):
    r, d = c_all.shape
    n = w_ada.shape[1]
    tn = 1536 if n % 1536 == 0 else n
    return pl.pallas_call(
        _ada_body,
        grid=(n // tn,),
        in_specs=[pl.BlockSpec((r, d), lambda j: (0, 0)),
                  pl.BlockSpec((d, tn), lambda j: (0, j)),
                  pl.BlockSpec((1, tn), lambda j: (0, j))],
        out_specs=pl.BlockSpec((r, tn), lambda j: (0, j)),
        out_shape=jax.ShapeDtypeStruct((r, n), F32),
        compiler_params=_params("arbitrary"),
        name="ada",
    )(c_all, w_ada, b_ada.reshape(1, n))


def _norm_mod(x, g, sc, sh):
    ms = jnp.mean(x * x, axis=-1, keepdims=True)
    y = x * lax.rsqrt(ms + EPS) * g
    return y * (1.0 + sc) + sh


def _rms(x, g):
    ms = jnp.mean(x * x, axis=-1, keepdims=True)
    return x * lax.rsqrt(ms + EPS) * g


def _pre_nsa_body(x_ref, sc_ref, sh_ref, g_ref, wq_ref, wkv_ref, wg_ref,
                  q_ref, kvc_ref, kvs_ref, kvw_ref, kvsb_ref, kvwb_ref, gate_ref, *t_refs):
    h = _norm_mod(x_ref[...], g_ref[...], sc_ref[...], sh_ref[...]).astype(BF16)
    q_ref[...] = (_dot(h, wq_ref[...]) * (HEAD_DIM ** -0.5 * LOG2E)).astype(BF16)
    kv = _dot(h, wkv_ref[...])
    kvc = kv[:, 0:KV_W]
    kvs = kv[:, KV_W:2 * KV_W]
    kvw = kv[:, 2 * KV_W:3 * KV_W]
    kvc_ref[...] = kvc
    kvs_ref[...] = kvs
    kvw_ref[...] = kvw
    kvsb_ref[...] = kvs.astype(BF16)
    kvwb_ref[...] = kvw.astype(BF16)
    gate_ref[...] = _sigmoid(_dot(h, wg_ref[...]))
    if t_refs:
        kvt = kv.T
        for j, t_ref in enumerate(t_refs):
            t_ref[...] = kvt[j * KV_W:(j + 1) * KV_W]


def _mod_spec(mod3, tm, rows_per_mod):
    r = mod3.shape[1]
    return pl.BlockSpec((None, r, mod3.shape[2]), lambda i: ((i * tm) // rows_per_mod, 0, 0))


def _pre_nsa(x, sc, sh, g, wq, wkv, wg, tm, rows_per_mod, feature_major=None):
    n, d = x.shape
    row = lambda w: pl.BlockSpec((tm, w), lambda i: (i, 0))
    full = lambda a: pl.BlockSpec(a.shape, lambda i: (0, 0))
    outs = [(NQ, BF16), (KV_W, F32), (KV_W, F32), (KV_W, F32), (KV_W, BF16), (KV_W, BF16), (LANES, F32)]
    out_specs = [row(w) for w, _ in outs]
    out_shape = [jax.ShapeDtypeStruct((n, w), t) for w, t in outs]
    if feature_major is not None:
        nbatch, per = feature_major
        tiles = per // tm
        out_specs += [pl.BlockSpec((None, KV_W, tm), lambda i: (i // tiles, 0, i % tiles))] * 3
        out_shape += [jax.ShapeDtypeStruct((nbatch, KV_W, per), F32)] * 3
    return pl.pallas_call(
        _pre_nsa_body,
        grid=(n // tm,),
        in_specs=[row(d), _mod_spec(sc, tm, rows_per_mod), _mod_spec(sh, tm, rows_per_mod), full(g),
                  full(wq), full(wkv), full(wg)],
        out_specs=out_specs,
        out_shape=out_shape,
        compiler_params=_params("parallel"),
        name="pre_nsa",
    )(x, sc, sh, g, wq, wkv, wg)


def _log_sigmoid(x):
    return jnp.minimum(x, 0.0) - jnp.log(1.0 + jnp.exp(-jnp.abs(x)))


def _pre_gla_body(x_ref, sc_ref, sh_ref, g_ref, wqkv_ref, wa_ref, wr_ref, a2_ref, ba_ref,
                  q_ref, k_ref, v_ref, lg_ref, r_ref):
    h = _norm_mod(x_ref[...], g_ref[...], sc_ref[...], sh_ref[...]).astype(BF16)
    qkv = _dot(h, wqkv_ref[...])
    q_ref[...] = qkv[:, 0:GQK] * GLA_DK ** -0.5
    k_ref[...] = qkv[:, GQK:2 * GQK]
    v_ref[...] = qkv[:, 2 * GQK:2 * GQK + GV]
    a = _dot(h, wa_ref[...])
    lg_ref[...] = _log_sigmoid(_dot_f32(a, a2_ref[...]) + ba_ref[...]) * (1.0 / GLA_TAU)
    r = _dot(h, wr_ref[...])
    r_ref[...] = (r * _sigmoid(r)).astype(BF16)


def _pre_gla(x, sc, sh, g, wqkv, wa, wr, a2, ba, tm, rows_per_mod):
    n, d = x.shape
    row = lambda w: pl.BlockSpec((tm, w), lambda i: (i, 0))
    full = lambda a: pl.BlockSpec(a.shape, lambda i: (0, 0))
    return pl.pallas_call(
        _pre_gla_body,
        grid=(n // tm,),
        in_specs=[row(d), _mod_spec(sc, tm, rows_per_mod), _mod_spec(sh, tm, rows_per_mod), full(g),
                  full(wqkv), full(wa), full(wr), full(a2), full(ba)],
        out_specs=[row(GQK), row(GQK), row(GV), row(GQK), row(GV)],
        out_shape=[jax.ShapeDtypeStruct((n, GQK), F32), jax.ShapeDtypeStruct((n, GQK), F32),
                   jax.ShapeDtypeStruct((n, GV), F32), jax.ShapeDtypeStruct((n, GQK), F32),
                   jax.ShapeDtypeStruct((n, GV), BF16)],
        compiler_params=_params("parallel"),
        name="pre_gla",
    )(x, sc, sh, g, wqkv, wa, wr, a2, ba)


def _gelu_tanh(x):
    return 0.5 * x * (1.0 + jnp.tanh(0.7978845608028654 * (x + 0.044715 * x * x * x)))


def _compress(get_xp, nch, pe_ref, w1_ref, b1_ref, w2_ref):
    first, second = [], []
    for kv in range(2):
        acc = jnp.zeros((nch + PE_ROWS, 2 * LANES), F32)
        for pp in range(CMP_STRIDE // 2):
            lhs = jnp.concatenate([get_xp(kv, 2 * pp), get_xp(kv, 2 * pp + 1)], axis=1).astype(BF16)
            acc = acc + _dot(jnp.concatenate([lhs, pe_ref[kv, pp]], axis=0), w1_ref[kv, pp])
        first.append(acc[0:nch, 0:LANES] + acc[nch:nch + 1, 0:LANES])
        second.append(acc[0:nch, LANES:2 * LANES] + acc[nch + 1:nch + 2, LANES:2 * LANES])
    first = jnp.concatenate(first, axis=1)
    second = jnp.concatenate(second, axis=1)
    if nch % 8 == 0:
        nxt = pltpu.roll(second, nch - 1, 0)
    else:
        nxt = jnp.concatenate([second[1:], second[:1]], axis=0)
    hid = _gelu_tanh(first + nxt + b1_ref[...])
    out = _dot(hid.astype(BF16), w2_ref[...])
    return jnp.where(_iota(out.shape, 0) < nch - 1, out, 0.0)


def _cmp_prompt_body(xk_ref, xv_ref, pe_ref, w1_ref, b1_ref, w2_ref, o_ref, *, nch):
    x_refs = (xk_ref, xv_ref)

    def get_xp(kv, p):
        return x_refs[kv][pl.ds(p, nch, stride=CMP_STRIDE), :]

    o_ref[...] = _compress(get_xp, nch, pe_ref, w1_ref, b1_ref, w2_ref).astype(o_ref.dtype)


def _cmp_prompt(kvc, cw):
    b, l, _ = kvc.shape
    nch = l // CMP_STRIDE
    full = lambda a: pl.BlockSpec(a.shape, lambda i: (0,) * a.ndim)
    return pl.pallas_call(
        functools.partial(_cmp_prompt_body, nch=nch),
        grid=(b,),
        in_specs=[pl.BlockSpec((None, l, LANES), lambda i: (i, 0, 0)),
                  pl.BlockSpec((None, l, LANES), lambda i: (i, 0, 1))] + [full(a) for a in cw],
        out_specs=pl.BlockSpec((None, nch, KV_W), lambda i: (i, 0, 0)),
        out_shape=jax.ShapeDtypeStruct((b, nch, KV_W), BF16),
        compiler_params=_params("parallel"),
        name="cmp_prompt",
    )(kvc, kvc, *cw)


def _cmp_sample_body(pt_ref, cache_ref, pe_ref, w1_ref, b1_ref, w2_ref, o_ref,
                     page_sc, ak_sc, av_sc, bk_sc, bv_sc, sem, *, n_seq, n_pages, nch):
    i = pl.program_id(0)
    cpp = PAGE_SIZE // CMP_STRIDE
    bufs = ((ak_sc, av_sc), (bk_sc, bv_sc))

    def page_copy(seq, slot, k):
        return pltpu.make_async_copy(cache_ref.at[pt_ref[seq, k]], page_sc.at[slot, k], sem.at[slot])

    def fetch(seq, slot):
        for k in range(n_pages):
            page_copy(seq, slot, k).start(priority=k % 2)

    def wait(slot):
        for k in range(n_pages):
            pltpu.make_async_copy(cache_ref.at[0], page_sc.at[slot, k], sem.at[slot]).wait()

    def regroup(slot):
        for k in range(n_pages):
            for kv in range(2):
                xt = page_sc[slot, k, kv * LANES:(kv + 1) * LANES, :].T
                for n in range(cpp):
                    for a in range(CMP_STRIDE // 8):
                        r0 = n * CMP_STRIDE + 8 * a
                        bufs[slot][kv][k, pl.ds(8 * a * cpp + n, 8, stride=cpp), :] = xt[r0:r0 + 8, :]

    def mlp(src):
        def get_xp(kv, p):
            return src[kv][:, p * cpp:(p + 1) * cpp, :].reshape(nch, LANES)

        o_ref[...] = _compress(get_xp, nch, pe_ref, w1_ref, b1_ref, w2_ref).astype(o_ref.dtype)

    @pl.when(i == 0)
    def _():
        fetch(0, 0)
        if n_seq == 1:
            page_sc[1] = jnp.zeros(page_sc.shape[1:], F32)
        bk_sc[...] = jnp.zeros(bk_sc.shape, F32)
        bv_sc[...] = jnp.zeros(bv_sc.shape, F32)

    for slot in range(2):
        @pl.when(i % 2 == slot)
        def _():
            @pl.when(i + 1 < n_seq)
            def _():
                fetch(i + 1, 1 - slot)

            @pl.when(i < n_seq)
            def _():
                wait(slot)

            regroup(slot)
            mlp(bufs[1 - slot])


def _cmp_sample(cache_t, page_table, cw):
    bd, n_pages = page_table.shape
    p_len = n_pages * PAGE_SIZE
    nch = p_len // CMP_STRIDE
    full = lambda a: pl.BlockSpec(a.shape, lambda i, pt: (0,) * a.ndim)
    return pl.pallas_call(
        functools.partial(_cmp_sample_body, n_seq=bd, n_pages=n_pages, nch=nch),
        grid_spec=pltpu.PrefetchScalarGridSpec(
            num_scalar_prefetch=1,
            grid=(bd + 1,),
            in_specs=[pl.BlockSpec(memory_space=pl.ANY)] + [full(a) for a in cw],
            out_specs=pl.BlockSpec((None, nch, KV_W), lambda i, pt: (jnp.maximum(i - 1, 0), 0, 0)),
            scratch_shapes=[pltpu.VMEM((2, n_pages, KV_W, PAGE_SIZE), F32)]
            + [pltpu.VMEM((n_pages, PAGE_SIZE, LANES), F32)] * 4 + [pltpu.SemaphoreType.DMA((2,))],
        ),
        out_shape=jax.ShapeDtypeStruct((bd, nch, KV_W), BF16),
        compiler_params=_params("arbitrary"),
        name="cmp_sample",
    )(page_table, cache_t, *cw)


def _head_queries(q, rows):
    half = _iota((rows, LANES), 1) >> 6
    out = []
    for hh in range(NSA_HEADS):
        g = hh // NSA_HPG
        blk = q[:, (hh // 2) * LANES:(hh // 2 + 1) * LANES].astype(F32)
        if hh % 2 != g:
            blk = pltpu.roll(blk, HEAD_DIM, 1) if rows % 8 == 0 else jnp.roll(blk, HEAD_DIM, 1)
        out.append(jnp.where(half == g, blk, 0.0).astype(BF16))
    return out


def _topk_select(imp, cur):
    impt = imp.T
    nb = impt.shape[0]
    jl = _iota(impt.shape, 0)
    jf = jl.astype(F32)
    forced = (jl == 0) | (jl == cur) | (jl == cur - 1)
    elig = (jl >= 1) & (jl <= cur - 2)
    dead = -1.0

    def step(_, vals):
        m = jnp.max(vals, axis=0, keepdims=True)
        idx = jnp.min(jnp.where(vals == m, jf, float(nb)), axis=0, keepdims=True)
        return jnp.where(jf == idx, dead, vals)

    vals = lax.fori_loop(0, N_SEL - 3, step, jnp.where(elig, impt, dead))
    return jnp.where(forced | (elig & (vals == dead)), 1.0, 0.0).T


def _masked_softmax2(s, valid):
    s = jnp.where(valid, s, NEG)
    m = jnp.max(s, axis=-1, keepdims=True)
    e = jnp.where(valid, jnp.exp2(s - m), 0.0)
    d = jnp.sum(e, axis=-1, keepdims=True)
    return e * (1.0 / jnp.where(d > 0.0, d, 1.0))


def _nsa_prompt_body(q_ref, gate_ref, ckv_ref, kvs_ref, kvw_ref, ov_ref, et_ref, o_ref, qa_sc, m_sc, acc_sc, out_sc, imp_sc,
                     *, tq, tk, n_cmp):
    i = pl.program_id(1)
    t0 = i * tq
    nc = ckv_ref.shape[0]
    nb = ov_ref.shape[1]
    nh = NSA_HEADS
    qh = _head_queries(q_ref[...], tq)
    for hh in range(nh):
        qa_sc[hh * tq:(hh + 1) * tq, 0:LANES] = qh[hh]
    half = _iota((tq, LANES), 1) >> 6
    trow = t0 + _iota((1, tq, 1), 1)
    gates = gate_ref[...]

    def gate(branch, hh):
        c = branch * nh + hh
        return gates[:, c:c + 1]

    def cmp_branch(w):
        s = _dot_nt(qa_sc[:, 0:LANES], ckv_ref[0:w, 0:LANES]).reshape(nh, tq, w)
        ncol = _iota((1, tq, w), 2)
        cvalid = (ncol * CMP_STRIDE + CMP_LEN <= trow + 1) & (ncol < n_cmp)
        p = _masked_softmax2(s, cvalid)
        cv = ckv_ref[0:w, LANES:2 * LANES]
        for hh in range(nh):
            out_sc[hh] = gate(0, hh) * _dot(p[hh].astype(BF16), cv)
        ov = ov_ref[0:w, :]
        for g in range(NSA_GROUPS):
            psum = p[g * NSA_HPG]
            for h in range(1, NSA_HPG):
                psum = psum + p[g * NSA_HPG + h]
            imp_sc[g * tq:(g + 1) * tq, :] = _dot01(psum, ov)

    widths = list(range(LANES, nc, LANES)) + [nc]
    variant = jnp.clip(((t0 + tq) // CMP_STRIDE - 2) // LANES, 0, len(widths) - 1)
    for v, w in enumerate(widths):
        @pl.when(variant == v)
        def _():
            cmp_branch(w)

    cur = (t0 + (_iota((1, NSA_GROUPS * tq), 1) & (tq - 1))) >> 6
    sel = _topk_select(imp_sc[...], cur)
    selneg = jnp.where(sel > 0.0, 0.0, NEG).astype(BF16)
    for hh in range(nh):
        g = hh // NSA_HPG
        qa_sc[hh * tq:(hh + 1) * tq, LANES:LANES + nb] = selneg[g * tq:(g + 1) * tq]

    def load_v(ref, start, width):
        v128 = ref[pl.ds(start, width), LANES:2 * LANES]
        lane_half = _iota(v128.shape, 1) >> 6
        return [jnp.where(lane_half == g, v128, jnp.ones_like(v128)) for g in range(NSA_GROUPS)]

    def online(s3, vaug, first):
        width = s3.shape[-1]
        tmax = jnp.max(s3, axis=-1, keepdims=True)
        if first:
            m_new = jnp.broadcast_to(tmax, (nh, tq, LANES))
        else:
            m_old = m_sc[...]
            m_new = jnp.maximum(m_old, tmax)
            alpha = jnp.exp2(m_old - m_new)
        pexp = jnp.concatenate([jnp.exp2((s3[:, :, c * LANES:(c + 1) * LANES] - m_new).astype(BF16))
                                for c in range(width // LANES)], axis=-1)
        for hh in range(nh):
            pv = _dot(pexp[hh], vaug[hh // NSA_HPG])
            acc_sc[hh] = pv if first else alpha[hh] * acc_sc[hh] + pv
        m_sc[...] = m_new

    def finish(branch):
        for hh in range(nh):
            acc = acc_sc[hh]
            den = pltpu.roll(acc, HEAD_DIM, 1)
            out_sc[hh] = out_sc[hh] + gate(branch, hh) * (acc / den)

    def slc_scores(start):
        kaug = jnp.concatenate([kvs_ref[pl.ds(start, tk), 0:LANES], et_ref[pl.ds(start, tk), :]], axis=1)
        return _dot_nt(qa_sc[...], kaug).reshape(nh, tq, tk)

    n_full = i // (tk // tq)
    last = pl.multiple_of(n_full * tk, tk)
    s3 = slc_scores(last)
    s3 = jnp.where(last + _iota((1, tq, tk), 2) <= trow, s3, NEG)
    online(s3, load_v(kvs_ref, last, tk), True)

    def slc_step(j, carry):
        start = pl.multiple_of(j * tk, tk)
        online(slc_scores(start), load_v(kvs_ref, start, tk), False)
        return carry

    lax.fori_loop(0, n_full, slc_step, 0)
    finish(1)

    d0 = pl.multiple_of(t0, tq)
    s3 = _dot_nt(qa_sc[:, 0:LANES], kvw_ref[pl.ds(d0, tq), 0:LANES]).reshape(nh, tq, tq)
    s3 = jnp.where(_iota((1, tq, tq), 2) <= _iota((1, tq, tq), 1), s3, NEG)
    online(s3, load_v(kvw_ref, d0, tq), True)

    @pl.when(i >= 1)
    def _():
        start = pl.multiple_of(jnp.maximum(t0 - WINDOW, 0), tq)
        s3 = _dot_nt(qa_sc[:, 0:LANES], kvw_ref[pl.ds(start, WINDOW), 0:LANES]).reshape(nh, tq, WINDOW)
        kpos = start + _iota((1, tq, WINDOW), 2)
        s3 = jnp.where((kpos < t0) & (trow - kpos < WINDOW), s3, NEG)
        online(s3, load_v(kvw_ref, start, WINDOW), False)

    finish(2)

    for pair in range(nh // 2):
        parts = []
        for hh in (2 * pair, 2 * pair + 1):
            x = out_sc[hh]
            if hh % 2 != hh // NSA_HPG:
                x = pltpu.roll(x, HEAD_DIM, 1)
            parts.append(x)
        o_ref[:, pair * LANES:(pair + 1) * LANES] = jnp.where(half == 0, parts[0], parts[1]).astype(o_ref.dtype)


def _nsa_prompt(q, gates, ckv, kvs_b, kvw_b, ov, et, tq, tk):
    b, l, _ = q.shape
    nb = l // SLC_BLOCK
    assert l % tk == 0 and tk % tq == 0 and WINDOW % tq == 0 and l >= WINDOW
    blk = lambda w: pl.BlockSpec((None, tq, w), lambda bi, i: (bi, i, 0))
    seq = lambda a: pl.BlockSpec((None,) + a.shape[1:], lambda bi, i: (bi, 0, 0))
    full = lambda a: pl.BlockSpec(a.shape, lambda bi, i: (0, 0))
    hs = (NSA_HEADS, tq, LANES)
    return pl.pallas_call(
        functools.partial(_nsa_prompt_body, tq=tq, tk=tk, n_cmp=l // CMP_STRIDE - 1),
        grid=(b, l // tq),
        in_specs=[blk(NQ), blk(LANES), seq(ckv), seq(kvs_b), seq(kvw_b), full(ov), full(et)],
        out_specs=blk(NQ),
        out_shape=jax.ShapeDtypeStruct((b, l, NQ), BF16),
        scratch_shapes=[pltpu.VMEM((NSA_HEADS * tq, LANES + nb), BF16), pltpu.VMEM(hs, F32), pltpu.VMEM(hs, F32),
                        pltpu.VMEM(hs, F32), pltpu.VMEM((NSA_GROUPS * tq, nb), F32)],
        compiler_params=_params("parallel", "arbitrary"),
        name="nsa_prompt",
    )(q, gates, ckv, kvs_b, kvw_b, ov, et)


def _nsa_sample_body(pt_ref, cache_ref, q_ref, gate_ref, ckv_ref, win_ref, ns_ref, nw_ref, ov_ref, e_ref, o_ref,
                     page_sc, sem, *, n_steps, n_pages, nseq, s_new, n_cmp, p_len):
    i = pl.program_id(0)
    npg = nseq * n_pages

    def fetch(step, slot):
        for s in range(nseq):
            for k in range(n_pages):
                pltpu.make_async_copy(cache_ref.at[pt_ref[step * nseq + s, k]], page_sc.at[slot, s * n_pages + k],
                                      sem.at[slot]).start(priority=k % 2)

    def wait(slot):
        for j in range(npg):
            pltpu.make_async_copy(cache_ref.at[0], page_sc.at[slot, j], sem.at[slot]).wait()

    @pl.when(i == 0)
    def _():
        fetch(0, 0)

    for slot in range(2):
        @pl.when(i % 2 == slot)
        def _():
            @pl.when(i + 1 < n_steps)
            def _():
                fetch(i + 1, 1 - slot)

            wait(slot)
            _nsa_sample_attend([page_sc.at[slot, j] for j in range(npg)], q_ref, gate_ref, ckv_ref, win_ref, ns_ref,
                               nw_ref, ov_ref, e_ref, o_ref, n_pages=n_pages, nseq=nseq, s_new=s_new, n_cmp=n_cmp,
                               p_len=p_len)


def _nsa_sample_attend(page_refs, q_ref, gate_ref, ckv_ref, win_ref, ns_ref, nw_ref, ov_ref, e_ref, o_ref,
                       *, n_pages, nseq, s_new, n_cmp, p_len):
    rows1 = NSA_HEADS * s_new
    rows = nseq * rows1
    nc = ckv_ref.shape[1]
    nb = ov_ref.shape[1]
    w_buf = win_ref.shape[2]
    seqs = range(nseq)
    of = lambda x, s: x[s * rows1:(s + 1) * rows1]
    cat = lambda xs: xs[0] if len(xs) == 1 else jnp.concatenate(xs, axis=0)
    q2 = [jnp.concatenate(_head_queries(q_ref[s], s_new), axis=0) for s in seqs]
    ridx = _iota((rows, 1), 0)
    trow = ridx & (s_new - 1)
    grow = _div_pow2(ridx, s_new * NSA_HPG) & (NSA_GROUPS - 1)
    lane_half = _iota((rows, LANES), 1) >> 6

    def gate_rows(branch):
        cols = [gate_ref[s][:, branch * NSA_HEADS + hh:branch * NSA_HEADS + hh + 1]
                for s in seqs for hh in range(NSA_HEADS)]
        return jnp.concatenate(cols, axis=0)

    def pick(acc, den):
        return jnp.where(lane_half == grow, acc, 0.0) / den

    def attend(s_old, s_nw, vt_old, new_rows):
        s_old = cat(s_old)
        s_nw = jnp.where(_iota((rows, s_new), 1) <= trow, cat(s_nw), NEG)
        m = jnp.maximum(jnp.max(s_old, axis=-1, keepdims=True), jnp.max(s_nw, axis=-1, keepdims=True))
        e1 = jnp.exp2(s_old - m)
        e2 = jnp.exp2(s_nw - m)
        den = jnp.sum(e1, axis=-1, keepdims=True) + jnp.sum(e2, axis=-1, keepdims=True)
        e1 = e1.astype(BF16)
        e2 = e2.astype(BF16)
        acc = cat([_dot_nt(vt_old[s], of(e1, s)).T + _dot(of(e2, s), new_rows[s][:, LANES:2 * LANES]) for s in seqs])
        return pick(acc, den)

    ncol = _iota((rows, nc), 1)
    cvalid = (ncol * CMP_STRIDE + CMP_LEN <= p_len + trow + 1) & (ncol < n_cmp)
    p_c = _masked_softmax2(cat([_dot_nt(q2[s], ckv_ref[s, :, 0:LANES]) for s in seqs]), cvalid)
    o_c = cat([_dot(of(p_c, s).astype(BF16), ckv_ref[s, :, LANES:2 * LANES]) for s in seqs])
    out = gate_rows(0) * jnp.where(lane_half == grow, o_c, 0.0)
    psum = []
    for s in seqs:
        for g in range(NSA_GROUPS):
            acc = jnp.zeros((s_new, nc), F32)
            for h in range(NSA_HPG):
                r0 = s * rows1 + (g * NSA_HPG + h) * s_new
                acc = acc + p_c[r0:r0 + s_new]
            psum.append(acc)
    n_imp = nseq * NSA_GROUPS * s_new
    psum.append(jnp.zeros((LANES - n_imp, nc), F32))
    imp = _dot01(jnp.concatenate(psum, axis=0), ov_ref[...])
    cur = jnp.full((1, LANES), p_len // SLC_BLOCK, jnp.int32)
    sel = _topk_select(imp, cur)[0:n_imp]
    selneg = jnp.where(sel > 0.0, 0.0, NEG).astype(BF16)
    selneg_rows = [jnp.concatenate([selneg[(s * NSA_GROUPS + g) * s_new:(s * NSA_GROUPS + g + 1) * s_new]
                                    for g in range(NSA_GROUPS) for _ in range(NSA_HPG)], axis=0)
                   for s in seqs]

    e_hot = e_ref[...]
    s_past, s_nw, vts, news = [], [], [], []
    for s in seqs:
        kvt = jnp.concatenate([r[...] for r in page_refs[s * n_pages:(s + 1) * n_pages]],
                              axis=1).astype(BF16)
        kaug = jnp.concatenate([kvt[0:LANES], e_hot], axis=0)
        s_past.append(_dot(jnp.concatenate([q2[s], selneg_rows[s]], axis=1), kaug))
        ns = ns_ref[s].astype(BF16)
        s_nw.append(_dot_nt(q2[s], ns[:, 0:LANES]))
        vts.append(kvt[LANES:2 * LANES])
        news.append(ns)
    out = out + gate_rows(1) * attend(s_past, s_nw, vts, news)

    scol = _iota((rows1, w_buf), 1)
    tr1 = _iota((rows1, 1), 0) & (s_new - 1)
    wmask = (w_buf + tr1 - scol < WINDOW) & (p_len - w_buf + scol >= 0)
    s_b, s_nw, vts, news = [], [], [], []
    for s in seqs:
        wt = win_ref[s].astype(BF16)
        nw = nw_ref[s].astype(BF16)
        s_b.append(jnp.where(wmask, _dot(q2[s], wt[0:LANES]), NEG))
        s_nw.append(_dot_nt(q2[s], nw[:, 0:LANES]))
        vts.append(wt[LANES:2 * LANES])
        news.append(nw)
    out = out + gate_rows(2) * attend(s_b, s_nw, vts, news)

    lh = _iota((s_new, LANES), 1) >> 6
    for s in seqs:
        for pair in range(NSA_HEADS // 2):
            parts = []
            for hh in (2 * pair, 2 * pair + 1):
                x = out[s * rows1 + hh * s_new:s * rows1 + (hh + 1) * s_new]
                if hh % 2 != hh // NSA_HPG:
                    x = jnp.concatenate([x[:, HEAD_DIM:], x[:, :HEAD_DIM]], axis=1)
                parts.append(x)
            o_ref[s, :, pair * LANES:(pair + 1) * LANES] = jnp.where(lh == 0, parts[0], parts[1]).astype(o_ref.dtype)


def _nsa_sample(q, gates, ckv, cache_t, page_table, win_t, new_s, new_w, ov, e_all):
    bd, s_new, _ = q.shape
    n_pages = page_table.shape[1]
    p_len = n_pages * PAGE_SIZE
    assert s_new & (s_new - 1) == 0
    nseq = _tile_rows(bd, SAMPLE_SEQS_PER_STEP)
    full = lambda a: pl.BlockSpec(a.shape, lambda i, pt: (0, 0))
    seq = lambda a: pl.BlockSpec((nseq,) + a.shape[1:], lambda i, pt: (i, 0, 0))
    return pl.pallas_call(
        functools.partial(_nsa_sample_body, n_steps=bd // nseq, n_pages=n_pages, nseq=nseq, s_new=s_new,
                          n_cmp=p_len // CMP_STRIDE - 1, p_len=p_len),
        grid_spec=pltpu.PrefetchScalarGridSpec(
            num_scalar_prefetch=1,
            grid=(bd // nseq,),
            in_specs=[pl.BlockSpec(memory_space=pl.ANY), seq(q), seq(gates), seq(ckv), seq(win_t), seq(new_s),
                      seq(new_w), full(ov), full(e_all)],
            out_specs=pl.BlockSpec((nseq, s_new, NQ), lambda i, pt: (i, 0, 0)),
            scratch_shapes=[pltpu.VMEM((2, nseq * n_pages, KV_W, PAGE_SIZE), F32), pltpu.SemaphoreType.DMA((2,))],
        ),
        out_shape=jax.ShapeDtypeStruct((bd, s_new, NQ), BF16),
        compiler_params=_params("arbitrary"),
        name="nsa_sample",
    )(page_table, cache_t, q, gates, ckv, win_t, new_s, new_w, ov, e_all)


def _gla_body(q_ref, k_ref, v_ref, lg_ref, r_ref, gn_ref, s0_ref, o_ref, sfin_ref, s_sc, *, ns, tb, chunk, sub):
    j = pl.program_id(1)

    @pl.when(j == 0)
    def _():
        s_sc[...] = s0_ref[...]

    nchunk = tb // chunk
    nsub = chunk // sub
    ri = _iota((tb, tb), 0)
    ci = _iota((tb, tb), 1)
    tril = jnp.where((_div_pow2(ri, chunk) == _div_pow2(ci, chunk)) & (ci <= ri), 1.0, 0.0).astype(BF16)
    head_of_lane = _div_pow2(_iota((1, GQK), 1), GLA_DK)
    rows = _iota((chunk, GQK), 0)
    ones_dv = jnp.ones((chunk, GLA_DV), BF16)
    vhead = _div_pow2(_iota((chunk, GV), 1), GLA_DV)
    gn = gn_ref[...]

    units = [(si, c) for si in range(ns) for c in range(nchunk)]
    b_seq = [_dot01_l(tril, lg_ref[si]) for si in range(ns)]
    bt_seq = [b.T for b in b_seq] if tb % LANES == 0 else None
    pre = {}
    for si, c in units:
        c0 = c * chunk
        bc = b_seq[si][c0:c0 + chunk]
        qc = q_ref[si, c0:c0 + chunk, :]
        kc = k_ref[si, c0:c0 + chunk, :]
        b_last = bc[chunk - 1:chunk]
        qdb = qc * jnp.exp(bc)
        kdl = kc * jnp.exp(b_last - bc)
        if bt_seq is not None:
            decay = jnp.exp(jnp.broadcast_to(bt_seq[si][:, c0 + chunk - 1:c0 + chunk], (GQK, GLA_DV)))
        else:
            decay = jnp.exp(_dot01_tn(jnp.where(rows == chunk - 1, bc, 0.0), ones_dv))
        kds = []
        for sb in range(nsub):
            rr = sb * sub
            ref_row = bc[rr - 1:rr] if sb else jnp.zeros((1, GQK), F32)
            vis = rows < rr + sub
            kds.append((ref_row, jnp.where(vis, kc * jnp.exp(jnp.where(vis, ref_row - bc, 0.0)), 0.0).astype(BF16)))
        vc = v_ref[si, c0:c0 + chunk, :].astype(BF16)
        vbd = jnp.concatenate([jnp.where(vhead == h, vc, jnp.zeros_like(vc)) for h in range(GLA_HEADS)], axis=0)
        o_sub = []
        for sb in range(nsub):
            rr = sb * sub
            ref_row, kd = kds[sb]
            qd = qc[rr:rr + sub] * jnp.exp(bc[rr:rr + sub] - ref_row)
            qst = jnp.concatenate([jnp.where(head_of_lane == h, qd, 0.0) for h in range(GLA_HEADS)], axis=0)
            a = _dot_nt(qst.astype(BF16), kd)
            qrow = _iota((GLA_HEADS * sub, chunk), 0) & (sub - 1)
            a = jnp.where(_iota((GLA_HEADS * sub, chunk), 1) <= rr + qrow, a, 0.0)
            a_cat = jnp.concatenate([a[h * sub:(h + 1) * sub] for h in range(GLA_HEADS)],
                                    axis=1).astype(BF16)
            o_sub.append(a_cat)
        o_intra = _dot(o_sub[0] if nsub == 1 else jnp.concatenate(o_sub, axis=0), vbd)
        qdb_st = jnp.concatenate([jnp.where(head_of_lane == h, qdb, 0.0) for h in range(GLA_HEADS)],
                                 axis=0).astype(BF16)
        kv = _dot_tn(kdl.astype(BF16), vc)
        upd = jnp.concatenate([kv[h * GLA_DK:(h + 1) * GLA_DK, h * GLA_DV:(h + 1) * GLA_DV]
                               for h in range(GLA_HEADS)], axis=0)
        pre[si, c] = (decay, upd, o_intra, qdb_st)

    for si in range(ns):
        state = s_sc[si]
        for c in range(nchunk):
            c0 = c * chunk
            decay, upd, o_intra, qdb_st = pre[si, c]
            o_inter = _dot(qdb_st, state.astype(BF16))
            for h in range(GLA_HEADS):
                hs = slice(h * GLA_DV, (h + 1) * GLA_DV)
                o = o_inter[h * chunk:(h + 1) * chunk] + o_intra[:, hs]
                o_ref[si, c0:c0 + chunk, hs] = (_rms(o, gn) * r_ref[si, c0:c0 + chunk, hs].astype(F32)).astype(o_ref.dtype)
            state = decay * state + upd
        s_sc[si] = state

    @pl.when(j == pl.num_programs(1) - 1)
    def _():
        sfin_ref[...] = s_sc[...]


def _gla(q, k, v, lg, r, gn, s0, ns, tb, chunk):
    s, t, _ = q.shape
    sub = min(GLA_SUB, chunk)
    row = lambda w: pl.BlockSpec((ns, tb, w), lambda gi, j: (gi, j, 0))
    st = pl.BlockSpec((ns, GQK, GLA_DV), lambda gi, j: (gi, 0, 0))
    return pl.pallas_call(
        functools.partial(_gla_body, ns=ns, tb=tb, chunk=chunk, sub=sub),
        grid=(s // ns, t // tb),
        in_specs=[row(GQK), row(GQK), row(GV), row(GQK), row(GV), pl.BlockSpec(gn.shape, lambda gi, j: (0, 0)), st],
        out_specs=[row(GV), st],
        out_shape=[jax.ShapeDtypeStruct((s, t, GV), BF16), jax.ShapeDtypeStruct((s, GQK, GLA_DV), F32)],
        scratch_shapes=[pltpu.VMEM((ns, GQK, GLA_DV), F32)],
        compiler_params=_params("parallel", "arbitrary"),
        name="gla",
    )(q, k, v, lg, r, gn, s0)


def _mix_body(x_ref, sc_ref, sh_ref, ga_ref, gpre_ref, gpost_ref, on_ref, og_ref, wm_ref, wn_ref, wgo_ref, wo_ref,
              o_ref):
    x = x_ref[...]
    d = x.shape[1]
    h = _norm_mod(x, gpre_ref[...], sc_ref[...], sh_ref[...]).astype(BF16)
    m = _sigmoid(_dot(h, wm_ref[...]))
    y_a = _dot(on_ref[...], wn_ref[...])
    y_b = _dot(og_ref[...], wgo_ref[...])
    mixin = (m[:, 0:d] * y_a + m[:, d:2 * d] * y_b).astype(BF16)
    mix = _dot(mixin, wo_ref[...])
    o_ref[...] = x + ga_ref[...] * _rms(mix, gpost_ref[...])


def _mix(x, sc, sh, ga, gpre, gpost, o_nsa, o_gla, wm, wn, wgo, wo, tm, rows_per_mod):
    n, d = x.shape
    row = lambda w: pl.BlockSpec((tm, w), lambda i: (i, 0))
    full = lambda a: pl.BlockSpec(a.shape, lambda i: (0, 0))
    ms = lambda a: _mod_spec(a, tm, rows_per_mod)
    return pl.pallas_call(
        _mix_body,
        grid=(n // tm,),
        in_specs=[row(d), ms(sc), ms(sh), ms(ga), full(gpre), full(gpost), row(NQ), row(GV),
                  full(wm), full(wn), full(wgo), full(wo)],
        out_specs=row(d),
        out_shape=jax.ShapeDtypeStruct((n, d), F32),
        compiler_params=_params("parallel"),
        name="mix",
    )(x, sc, sh, ga, gpre, gpost, o_nsa, o_gla, wm, wn, wgo, wo)


def _ffn_body(x_ref, sc_ref, sh_ref, ga_ref, gpre_ref, gpost_ref, wg_ref, wu_ref, wd_ref, o_ref, h_sc, acc_sc):
    j = pl.program_id(1)

    @pl.when(j == 0)
    def _():
        h_sc[...] = _norm_mod(x_ref[...], gpre_ref[...], sc_ref[...], sh_ref[...]).astype(BF16)
        acc_sc[...] = jnp.zeros(acc_sc.shape, F32)

    h = h_sc[...]
    gt = _dot(h, wg_ref[...])
    up = _dot(h, wu_ref[...])
    a = (gt * _sigmoid(gt) * up).astype(BF16)
    acc_sc[...] += _dot(a, wd_ref[...])

    @pl.when(j == pl.num_programs(1) - 1)
    def _():
        o_ref[...] = x_ref[...] + ga_ref[...] * _rms(acc_sc[...], gpost_ref[...])


def _ffn(x, sc, sh, ga, gpre, gpost, wg, wu, wd, tm, rows_per_mod):
    n, d = x.shape
    dff = wg.shape[1]
    tf = dff // 2 if (dff // 2) % LANES == 0 else dff
    row = pl.BlockSpec((tm, d), lambda i, j: (i, 0))
    full = lambda a: pl.BlockSpec(a.shape, lambda i, j: (0, 0))
    ms = lambda a: pl.BlockSpec((None, a.shape[1], a.shape[2]), lambda i, j: ((i * tm) // rows_per_mod, 0, 0))
    return pl.pallas_call(
        _ffn_body,
        grid=(n // tm, dff // tf),
        in_specs=[row, ms(sc), ms(sh), ms(ga), full(gpre), full(gpost),
                  pl.BlockSpec((d, tf), lambda i, j: (0, j)), pl.BlockSpec((d, tf), lambda i, j: (0, j)),
                  pl.BlockSpec((tf, d), lambda i, j: (j, 0))],
        out_specs=row,
        out_shape=jax.ShapeDtypeStruct((n, d), F32),
        scratch_shapes=[pltpu.VMEM((tm, d), BF16), pltpu.VMEM((tm, d), F32)],
        compiler_params=_params("parallel", "arbitrary"),
        name="ffn",
    )(x, sc, sh, ga, gpre, gpost, wg, wu, wd)


def _split_w_in(w_in, d):
    sizes = (NQ, KV_W, KV_W, KV_W, 3 * NSA_HEADS, GQK, GQK, GV, 16, GV, 2 * d)
    out, o = [], 0
    for s in sizes:
        out.append(w_in[:, o:o + s])
        o += s
    return out


def _pad_cols(w, n):
    return jnp.pad(w, ((0, 0), (0, n - w.shape[1])))


def _compress_weights(w1k, b1k, w2k, pek, w1v, b1v, w2v, pev):
    hid = w2k.shape[0]
    npair = CMP_STRIDE // 2
    eye = np.eye(NSA_GROUPS, dtype=np.float32)

    def group_diag(w):
        out = jnp.einsum('gh,...ab->...gahb', eye, w)
        return out.reshape(w.shape[:-2] + (NSA_GROUPS * w.shape[-2], NSA_GROUPS * w.shape[-1]))

    def expand_w1(half):
        per_kv = [group_diag(w1.reshape(CMP_LEN, HEAD_DIM, hid)[half * CMP_STRIDE:(half + 1) * CMP_STRIDE])
                  for w1 in (w1k, w1v)]
        return jnp.stack(per_kv).reshape(2, npair, 2 * NSA_GROUPS * HEAD_DIM, NSA_GROUPS * hid)

    w1 = jnp.concatenate([expand_w1(0), expand_w1(1)], axis=-1).astype(BF16)

    def expand_pe(half):
        per_kv = [jnp.tile(pe[half * CMP_STRIDE:(half + 1) * CMP_STRIDE], (1, NSA_GROUPS)) for pe in (pek, pev)]
        return jnp.stack(per_kv).reshape(2, npair, 2 * NSA_GROUPS * HEAD_DIM)

    pe = jnp.pad(jnp.stack([expand_pe(0), expand_pe(1)], axis=2),
                 ((0, 0), (0, 0), (0, PE_ROWS - 2), (0, 0))).astype(BF16)

    b1 = jnp.concatenate([jnp.tile(b1k, NSA_GROUPS), jnp.tile(b1v, NSA_GROUPS)]).reshape(1, -1)
    w2d = [group_diag(w2k), group_diag(w2v)]
    zero = jnp.zeros_like(w2d[0])
    w2 = jnp.concatenate([jnp.concatenate([w2d[0], zero], axis=1),
                          jnp.concatenate([zero, w2d[1]], axis=1)], axis=0).astype(BF16)
    return pe, w1, b1, w2


def _overlap(n_cmp_pad, n_cmp, nb):
    n = np.arange(n_cmp_pad)[:, None]
    j = np.arange(nb)[None, :] * SLC_BLOCK
    ok = (n * CMP_STRIDE < j + SLC_BLOCK) & (n * CMP_STRIDE + CMP_LEN > j) & (n < n_cmp)
    return ok.astype(BF16)


def _block_onehot(nb, nkeys):
    return (np.arange(nb)[:, None] == (np.arange(nkeys)[None, :] // SLC_BLOCK)).astype(BF16)


def _tile_rows(n, cap):
    t = cap
    while n % t:
        t //= 2
    return t


def _feature_major(a):
    nd = a.ndim
    a = jnp.moveaxis(a, nd - 4, nd - 1)
    return a.reshape(a.shape[:nd - 4] + (KV_W, a.shape[-1]))


def _row_major(a_t):
    b, _, r = a_t.shape
    return jnp.moveaxis(a_t.reshape(b, 2, NSA_GROUPS, HEAD_DIM, r), 4, 1)


def kernel(x_prompt, x_sample, cache_cmp_kv, cache_slc_kv, state_win_kv, state_gla, page_table, c_prompt, c_sample, w_ada, b_ada, g_pre_mix, g_post_mix, w_in, cmp_w1_k, cmp_b1_k, cmp_w2_k, cmp_pe_k, cmp_w1_v, cmp_b1_v, cmp_w2_v, cmp_pe_v, w_gla_a2, b_gla_a, gla_norm_g, w_nsa_o, w_gla_o, w_out, g_pre_ffn, g_post_ffn, w_ff_gate, w_ff_up, w_ff_down):
    depth = w_ada.shape[0]
    bp, l, d = x_prompt.shape
    bd, s_new, _ = x_sample.shape
    n_pages = page_table.shape[1]
    p_len = n_pages * PAGE_SIZE
    w_buf = state_win_kv.shape[2]
    keep = min(WINDOW, l)
    np_, ns_ = bp * l, bd * s_new
    tmp = _tile_rows(np_ // bp, 512)
    tms = _tile_rows(ns_, 512)
    tq = _tile_rows(l, 256)
    tk = 2 * tq

    yp, ys = x_prompt.reshape(np_, d), x_sample.reshape(ns_, d)
    col = [[] for _ in range(8)]
    for li in range(depth):
        r_all = bp + bd
        r_pad = -(-r_all // 8) * 8
        c_all = jnp.pad(jnp.concatenate([c_prompt, c_sample], axis=0), ((0, r_pad - r_all), (0, 0)))
        mod = _ada(c_all, w_ada[li], b_ada[li])
        mods_p = [m.reshape(bp, 1, d) for m in jnp.split(mod[:bp], 6, axis=-1)]
        mods_s = [jnp.repeat(m, s_new, axis=0).reshape(ns_ // tms, tms, d) for m in jnp.split(mod[bp:r_all], 6, axis=-1)]

        (w_q, w_kc, w_ks, w_kw, w_gn, w_qg, w_kg, w_vg, w_ag, w_rg, w_mg) = _split_w_in(w_in[li], d)
        wq = w_q.astype(BF16)
        wkv = jnp.concatenate([w_kc, w_ks, w_kw], axis=1).astype(BF16)
        wgn = _pad_cols(w_gn, LANES).astype(BF16)
        wqkv = jnp.concatenate([w_qg, w_kg, w_vg], axis=1).astype(BF16)
        wa = _pad_cols(w_ag, LANES).astype(BF16)
        wr = w_rg.astype(BF16)
        a2 = jnp.pad(w_gla_a2[li], ((0, LANES - w_gla_a2.shape[1]), (0, 0)))
        ba = b_gla_a[li].reshape(1, GQK)
        cw = _compress_weights(cmp_w1_k[li], cmp_b1_k[li], cmp_w2_k[li], cmp_pe_k[li],
                               cmp_w1_v[li], cmp_b1_v[li], cmp_w2_v[li], cmp_pe_v[li])
        gpre = g_pre_mix[li].reshape(1, d)
        gpost = g_post_mix[li].reshape(1, d)
        gpre2 = g_pre_ffn[li].reshape(1, d)
        gpost2 = g_post_ffn[li].reshape(1, d)
        gn = gla_norm_g[li].reshape(1, GLA_DV)
        wm = w_mg.astype(BF16)
        wn = w_nsa_o[li].astype(BF16)
        wgo = w_gla_o[li].astype(BF16)
        wo = w_out[li].astype(BF16)
        wfg = w_ff_gate[li].astype(BF16)
        wfu = w_ff_up[li].astype(BF16)
        wfd = w_ff_down[li].astype(BF16)

        sh1, sc1, ga1, sh2, sc2, ga2 = mods_p
        (q, kvc, _, _, kvs_b, kvw_b, gates, kvc_t, kvs_t, kvw_t) = _pre_nsa(yp, sc1, sh1, gpre, wq, wkv, wgn, tmp, l,
                                                                            feature_major=(bp, l))
        qg, kg, vg, lg, rg = _pre_gla(yp, sc1, sh1, gpre, wqkv, wa, wr, a2, ba, tmp, l)
        ckv = _cmp_prompt(kvc.reshape(bp, l, KV_W), cw)
        nch = l // CMP_STRIDE
        nb = l // SLC_BLOCK
        o_nsa = _nsa_prompt(q.reshape(bp, l, NQ), gates.reshape(bp, l, LANES), ckv, kvs_b.reshape(bp, l, KV_W),
                            kvw_b.reshape(bp, l, KV_W), _overlap(nch, nch - 1, nb), _block_onehot(nb, l).T, tq, tk)
        chunk = GLA_CHUNK if l % GLA_CHUNK == 0 else l
        tb = 2 * LANES if (l % (2 * LANES) == 0 and LANES % chunk == 0) else chunk
        o_gla, s_fin_p = _gla(qg.reshape(bp, l, GQK), kg.reshape(bp, l, GQK), vg.reshape(bp, l, GV),
                              lg.reshape(bp, l, GQK), rg.reshape(bp, l, GV), gn,
                              jnp.zeros((bp, GQK, GLA_DV), F32), bp, tb, chunk)
        x1 = _mix(yp, sc1, sh1, ga1, gpre, gpost, o_nsa.reshape(np_, NQ), o_gla.reshape(np_, GV), wm, wn, wgo, wo, tmp, l)
        yp = _ffn(x1, sc2, sh2, ga2, gpre2, gpost2, wfg, wfu, wfd, tmp, l)
        col[0].append(_row_major(kvc_t))
        col[2].append(_row_major(kvs_t))
        col[4].append(_row_major(kvw_t[:, :, l - keep:]))
        col[6].append(s_fin_p.reshape(bp, GLA_HEADS, GLA_DK, GLA_DV).astype(state_gla.dtype))

        sh1, sc1, ga1, sh2, sc2, ga2 = mods_s
        q, kvc, kvs, kvw, _, _, gates = _pre_nsa(ys, sc1, sh1, gpre, wq, wkv, wgn, tms, tms)
        qg, kg, vg, lg, rg = _pre_gla(ys, sc1, sh1, gpre, wqkv, wa, wr, a2, ba, tms, tms)
        ckv = _cmp_sample(_feature_major(cache_cmp_kv[li]), page_table, cw)
        nch = p_len // CMP_STRIDE
        nb = p_len // SLC_BLOCK
        win_t = _feature_major(state_win_kv[li])
        kvw3 = kvw.reshape(bd, s_new, KV_W)
        o_nsa = _nsa_sample(q.reshape(bd, s_new, NQ), gates.reshape(bd, s_new, LANES), ckv,
                            _feature_major(cache_slc_kv[li]), page_table, win_t, kvs.reshape(bd, s_new, KV_W), kvw3,
                            _overlap(nch, nch - 1, nb), _block_onehot(nb, p_len))
        chunk = GLA_CHUNK if s_new % GLA_CHUNK == 0 else s_new
        o_gla, s_fin_s = _gla(qg.reshape(bd, s_new, GQK), kg.reshape(bd, s_new, GQK), vg.reshape(bd, s_new, GV),
                              lg.reshape(bd, s_new, GQK), rg.reshape(bd, s_new, GV), gn,
                              state_gla[li].astype(F32).reshape(bd, GQK, GLA_DV), _tile_rows(bd, 16), chunk, chunk)
        x1 = _mix(ys, sc1, sh1, ga1, gpre, gpost, o_nsa.reshape(ns_, NQ), o_gla.reshape(ns_, GV), wm, wn, wgo, wo, tms, tms)
        ys = _ffn(x1, sc2, sh2, ga2, gpre2, gpost2, wfg, wfu, wfd, tms, tms)
        win_new_t = jnp.concatenate([win_t, kvw3.transpose(0, 2, 1)], axis=2)[:, :, s_new:]
        win_new = jnp.moveaxis(win_new_t.reshape(bd, 2, NSA_GROUPS, HEAD_DIM, w_buf), 4, 1)
        col[1].append(kvc.reshape(bd, s_new, 2, NSA_GROUPS, HEAD_DIM))
        col[3].append(kvs.reshape(bd, s_new, 2, NSA_GROUPS, HEAD_DIM))
        col[5].append(win_new)
        col[7].append(s_fin_s.reshape(bd, GLA_HEADS, GLA_DK, GLA_DV).astype(state_gla.dtype))

    stacked = [c[0][None] if len(c) == 1 else jnp.stack(c) for c in col]
    return (yp.reshape(bp, l, d), ys.reshape(bd, s_new, d), *stacked)
```
